```python
import jax, jax.numpy as jnp
from jax import lax
import numpy as np

D_MODEL = 2048
BATCH = 8
SEQ = 2048
DEPTH = 1

MEM_LEN = 256
FOURIER_GROUPS = 4
FOURIER_GROUP_DIM = D_MODEL // 8
FOURIER_WIDTH = FOURIER_GROUPS * FOURIER_GROUP_DIM
N_HEADS = 32
N_KV_HEADS = 4
HEAD_DIM = D_MODEL // 32
ATTN_WIDTH = N_HEADS * HEAD_DIM
KV_WIDTH = N_KV_HEADS * HEAD_DIM
WINDOW = 128
ATT_BLOCK = WINDOW
MEM_HEADS = 4
MEM_HEAD_DIM = D_MODEL // 8
MEM_WIDTH = MEM_HEADS * MEM_HEAD_DIM
N_BRANCHES = 3
NUM_BUCKETS = 32
MAX_DISTANCE = 128
N_EXPERTS = 32
TOP_K = 4
D_FF = D_MODEL
SWIGLU_LIMIT = 7.0
SWIGLU_ALPHA = 1.702
MOE_BLOCK = 128
RMS_EPS = 1e-5
NEG_INF = -1e30

OFF_Q = FOURIER_WIDTH
OFF_K = OFF_Q + ATTN_WIDTH
OFF_V = OFF_K + KV_WIDTH
OFF_MQ = OFF_V + KV_WIDTH
OFF_GATE = OFF_MQ + MEM_WIDTH
IN_WIDTH = OFF_GATE + N_BRANCHES * D_MODEL

kernel_name = "hybrid_fnet_swa_memxattn_moe_encoder"


def rms_norm(x, g):
    x32 = x.astype(jnp.float32)
    y = x32 * lax.rsqrt(jnp.mean(x32 * x32, axis=-1, keepdims=True) + RMS_EPS)
    return (y * g.astype(jnp.float32)).astype(x.dtype)


def t5_bucket(rel):
    nb = NUM_BUCKETS // 2
    max_exact = nb // 2
    ret = (rel > 0).astype(jnp.int32) * nb
    n = jnp.abs(rel)
    nf = jnp.maximum(n, 1).astype(jnp.float32)
    large = max_exact + (jnp.log(nf / max_exact) / np.float32(np.log(MAX_DISTANCE / max_exact)) * (nb - max_exact)).astype(jnp.int32)
    large = jnp.minimum(large, nb - 1)
    return ret + jnp.where(n < max_exact, n, large)


def fourier_mix(f_in):
    b, s, _ = f_in.shape
    f = f_in.reshape(b, s, FOURIER_GROUPS, FOURIER_GROUP_DIM).astype(jnp.float32)
    f = jnp.real(jnp.fft.fft2(f, axes=(1, 3), norm="ortho"))
    return f.astype(f_in.dtype).reshape(b, s, FOURIER_WIDTH)


def windowed_gqa(q, k, v, sink, rel_table):
    b, s = q.shape[:2]
    nb = s // ATT_BLOCK
    grp = N_HEADS // N_KV_HEADS
    qb = q.reshape(b, nb, ATT_BLOCK, N_KV_HEADS, grp, HEAD_DIM)

    def band(t):
        tp = jnp.pad(t, ((0, 0), (ATT_BLOCK, ATT_BLOCK), (0, 0), (0, 0)))
        tp = tp.reshape(b, nb + 2, ATT_BLOCK, N_KV_HEADS, HEAD_DIM)
        return jnp.concatenate([tp[:, :-2], tp[:, 1:-1], tp[:, 2:]], axis=2)

    kb, vb = band(k), band(v)
    scores = jnp.einsum('bnqhgd,bnkhd->bnhgqk', qb, kb).astype(jnp.float32) * np.float32(HEAD_DIM ** -0.5)

    q_off = jnp.arange(ATT_BLOCK)[:, None]
    k_off = jnp.arange(3 * ATT_BLOCK)[None, :]
    rel = k_off - ATT_BLOCK - q_off
    bias = rel_table.astype(jnp.float32)[t5_bucket(rel)]
    bias = jnp.transpose(bias, (2, 0, 1)).reshape(N_KV_HEADS, grp, ATT_BLOCK, 3 * ATT_BLOCK)
    key_idx = jnp.arange(nb)[:, None] * ATT_BLOCK - ATT_BLOCK + k_off
    valid = (jnp.abs(rel) <= WINDOW)[None] & ((key_idx >= 0) & (key_idx < s))[:, None, :]
    scores = jnp.where(valid[None, :, None, None], scores + bias, NEG_INF)

    sink_l = sink.astype(jnp.float32).reshape(N_KV_HEADS, grp, 1, 1)
    m = jnp.maximum(jnp.max(scores, axis=-1, keepdims=True), sink_l)
    p = jnp.exp(scores - m)
    probs = p / (jnp.sum(p, axis=-1, keepdims=True) + jnp.exp(sink_l - m))
    out = jnp.einsum('bnhgqk,bnkhd->bnqhgd', probs.astype(v.dtype), vb)
    return out.reshape(b, s, ATTN_WIDTH)


def memory_xattn(qm, mem, w_mem_kv):
    b, s = qm.shape[:2]
    mlen = mem.shape[1]
    kv = mem @ w_mem_kv
    km = kv[..., :MEM_WIDTH].reshape(b, mlen, MEM_HEADS, MEM_HEAD_DIM)
    vm = kv[..., MEM_WIDTH:].reshape(b, mlen, MEM_HEADS, MEM_HEAD_DIM)
    qh = qm.reshape(b, s, MEM_HEADS, MEM_HEAD_DIM)
    scores = jnp.einsum('bqhd,bkhd->bhqk', qh, km).astype(jnp.float32) * np.float32(MEM_HEAD_DIM ** -0.5)
    probs = jax.nn.softmax(scores, axis=-1).astype(vm.dtype)
    return jnp.einsum('bhqk,bkhd->bqhd', probs, vm).reshape(b, s, MEM_WIDTH)


def sparse_moe(h, w_router, b_router, w_gu, b_gu, w_dn, b_dn):
    b, s, d = h.shape
    t = b * s
    hf = h.reshape(t, d)
    logits = (hf @ w_router + b_router).astype(jnp.float32)
    top_vals, top_idx = lax.top_k(logits, TOP_K)
    top_w = jax.nn.softmax(top_vals, axis=-1)

    n_assign = t * TOP_K
    e_flat = top_idx.reshape(-1).astype(jnp.int32)
    tok_flat = (jnp.arange(n_assign) // TOP_K).astype(jnp.int32)
    w_flat = top_w.reshape(-1)
    order = jnp.argsort(e_flat)
    e_sorted = e_flat[order]
    counts = jnp.bincount(e_flat, length=N_EXPERTS).astype(jnp.int32)
    starts = jnp.cumsum(counts) - counts
    padded = (counts + MOE_BLOCK - 1) // MOE_BLOCK * MOE_BLOCK
    padded_ends = jnp.cumsum(padded)
    padded_starts = padded_ends - padded
    dest = padded_starts[e_sorted] + (jnp.arange(n_assign, dtype=jnp.int32) - starts[e_sorted])

    n_slots = (-(-n_assign // MOE_BLOCK) + N_EXPERTS) * MOE_BLOCK
    n_blocks = n_slots // MOE_BLOCK
    slot_tok = jnp.full((n_slots,), t, jnp.int32).at[dest].set(tok_flat[order])
    slot_w = jnp.zeros((n_slots,), jnp.float32).at[dest].set(w_flat[order])
    block_e = jnp.minimum(jnp.searchsorted(padded_ends, jnp.arange(n_blocks, dtype=jnp.int32) * MOE_BLOCK, side='right'), N_EXPERTS - 1)

    h_pad = jnp.concatenate([hf, jnp.zeros((1, d), hf.dtype)], axis=0)
    xs = h_pad[slot_tok].reshape(n_blocks, MOE_BLOCK, d)

    def expert_block(args):
        xb, e = args
        gu = xb @ w_gu[e] + b_gu[e]
        gate = jnp.minimum(gu[:, ::2], SWIGLU_LIMIT)
        up = jnp.clip(gu[:, 1::2], -SWIGLU_LIMIT, SWIGLU_LIMIT)
        glu = gate * jax.nn.sigmoid(gate * SWIGLU_ALPHA)
        return ((up + 1.0) * glu) @ w_dn[e] + b_dn[e]

    ys = lax.map(expert_block, (xs, block_e)).reshape(n_slots, d)
    ys = ys * slot_w[:, None].astype(ys.dtype)
    out = jax.ops.segment_sum(ys, slot_tok, num_segments=t + 1)[:t]
    return out.reshape(b, s, d)


def setup_inputs(seed: int = 0) -> dict:
    key = jax.random.key(seed)
    ks = jax.random.split(key, 20)
    f32 = jnp.float32
    nrm = lambda k, shape, scale: jax.random.normal(k, shape, f32) * np.float32(scale)
    return {
        "x": nrm(ks[0], (BATCH, SEQ, D_MODEL), 1.0),
        "mem": nrm(ks[1], (BATCH, MEM_LEN, D_MODEL), 1.0),
        "rel_bias_table": nrm(ks[2], (NUM_BUCKETS, N_HEADS), 0.5),
        "norm1_g": 1.0 + nrm(ks[3], (DEPTH, D_MODEL), 0.01),
        "w_in": nrm(ks[4], (DEPTH, D_MODEL, IN_WIDTH), D_MODEL ** -0.5),
        "attn_sink": nrm(ks[5], (DEPTH, N_HEADS), 1.0),
        "w_mem_kv": nrm(ks[6], (DEPTH, D_MODEL, 2 * MEM_WIDTH), D_MODEL ** -0.5),
        "w_fourier_out": nrm(ks[7], (DEPTH, FOURIER_WIDTH, D_MODEL), FOURIER_WIDTH ** -0.5),
        "w_attn_out": nrm(ks[8], (DEPTH, ATTN_WIDTH, D_MODEL), ATTN_WIDTH ** -0.5),
        "w_mem_out": nrm(ks[9], (DEPTH, MEM_WIDTH, D_MODEL), MEM_WIDTH ** -0.5),
        "w_o": nrm(ks[10], (DEPTH, D_MODEL, D_MODEL), D_MODEL ** -0.5),
        "norm2_g": 1.0 + nrm(ks[11], (DEPTH, D_MODEL), 0.01),
        "w_router": nrm(ks[12], (DEPTH, D_MODEL, N_EXPERTS), D_MODEL ** -0.5),
        "b_router": nrm(ks[13], (DEPTH, N_EXPERTS), 0.01),
        "w_gate_up": nrm(ks[14], (DEPTH, N_EXPERTS, D_MODEL, 2 * D_FF), D_MODEL ** -0.5),
        "b_gate_up": nrm(ks[15], (DEPTH, N_EXPERTS, 2 * D_FF), 0.01),
        "w_down": nrm(ks[16], (DEPTH, N_EXPERTS, D_FF, D_MODEL), D_FF ** -0.5),
        "b_down": nrm(ks[17], (DEPTH, N_EXPERTS, D_MODEL), 0.01),
        "final_g": 1.0 + nrm(ks[18], (D_MODEL,), 0.01),
    }


def reference(x, mem, rel_bias_table, norm1_g, w_in, attn_sink, w_mem_kv, w_fourier_out, w_attn_out, w_mem_out, w_o, norm2_g, w_router, b_router, w_gate_up, b_gate_up, w_down, b_down, final_g):
    b, s, _ = x.shape
    for l in range(DEPTH):
        h = rms_norm(x, norm1_g[l])
        proj = h @ w_in[l]
        f_in, q, k, v, qm, gate_logits = jnp.split(proj, [OFF_Q, OFF_K, OFF_V, OFF_MQ, OFF_GATE], axis=-1)
        y_f = fourier_mix(f_in) @ w_fourier_out[l]
        att = windowed_gqa(q.reshape(b, s, N_HEADS, HEAD_DIM),
                           k.reshape(b, s, N_KV_HEADS, HEAD_DIM),
                           v.reshape(b, s, N_KV_HEADS, HEAD_DIM),
                           attn_sink[l], rel_bias_table)
        y_a = att @ w_attn_out[l]
        y_m = memory_xattn(qm, mem, w_mem_kv[l]) @ w_mem_out[l]
        gates = jax.nn.sigmoid(gate_logits.astype(jnp.float32)).astype(x.dtype).reshape(b, s, N_BRANCHES, D_MODEL)
        mixed = gates[:, :, 0] * y_f + gates[:, :, 1] * y_a + gates[:, :, 2] * y_m
        x = x + mixed @ w_o[l]
        x = x + sparse_moe(rms_norm(x, norm2_g[l]), w_router[l], b_router[l], w_gate_up[l], b_gate_up[l], w_down[l], b_down[l])
    return rms_norm(x, final_g)
```

```python
import functools

import numpy as np
import jax
import jax.numpy as jnp
from jax import lax
from jax.experimental import pallas as pl
from jax.experimental.pallas import tpu as pltpu

D_MODEL = 2048
MEM_LEN = 256
FOURIER_GROUPS = 4
FOURIER_GROUP_DIM = 256
FOURIER_WIDTH = 1024
N_HEADS = 32
N_KV_HEADS = 4
HEAD_DIM = 64
ATTN_WIDTH = 2048
KV_WIDTH = 256
WINDOW = 128
ATT_BLOCK = 128
MEM_HEADS = 4
MEM_HEAD_DIM = 256
MEM_WIDTH = 1024
N_BRANCHES = 3
NUM_BUCKETS = 32
MAX_DISTANCE = 128
N_EXPERTS = 32
TOP_K = 4
D_FF = 2048
SWIGLU_LIMIT = 7.0
SWIGLU_ALPHA = 1.702
RMS_EPS = 1e-5
NEG_INF = -1e30

GATE_WIDTH = N_BRANCHES * D_MODEL
COL_GATE = 0
COL_F = GATE_WIDTH
COL_MQ = COL_F + FOURIER_WIDTH
COL_Q = COL_MQ + MEM_WIDTH
COL_K = COL_Q + ATTN_WIDTH
COL_V = COL_K + KV_WIDTH
IN_WIDTH = COL_V + KV_WIDTH

_OFF_Q = FOURIER_WIDTH
_OFF_K = _OFF_Q + ATTN_WIDTH
_OFF_V = _OFF_K + KV_WIDTH
_OFF_MQ = _OFF_V + KV_WIDTH
_OFF_GATE = _OFF_MQ + MEM_WIDTH

LANES = 128
ROW_CHUNKS = D_MODEL // LANES
VMEM_LIMIT = 56 * 1024 * 1024

TM_IN, TN_IN = 512, 1536
TS_F = 512
TQ_MEM = 512
TM_MERGE = 256
TM_DISP = 256
TM_E, TF_E = 512, 512
TM_COMB = 128
ZERO_ROWS = 128

BF16 = jnp.bfloat16
F32 = jnp.float32


def _cparams(sem):
    return pltpu.CompilerParams(dimension_semantics=sem, vmem_limit_bytes=VMEM_LIMIT)


def _inproj_body(x_ref, g_ref, w_ref, o_ref, h_ref, *, n_gate_tiles):
    j = pl.program_id(1)

    @pl.when(j == 0)
    def _norm():
        x = x_ref[...]
        inv = lax.rsqrt(jnp.mean(x * x, axis=-1, keepdims=True) + RMS_EPS)
        h_ref[...] = (x * inv * g_ref[...]).astype(BF16)

    acc = jnp.dot(h_ref[...], w_ref[...], preferred_element_type=F32)

    @pl.when(j < n_gate_tiles)
    def _gate():
        o_ref[...] = (1.0 / (1.0 + jnp.exp(-acc))).astype(o_ref.dtype)

    @pl.when(j >= n_gate_tiles)
    def _plain():
        o_ref[...] = acc.astype(o_ref.dtype)


def _inproj(x2d, g, w):
    t = x2d.shape[0]
    return pl.pallas_call(
        functools.partial(_inproj_body, n_gate_tiles=GATE_WIDTH // TN_IN),
        grid=(t // TM_IN, IN_WIDTH // TN_IN),
        in_specs=[
            pl.BlockSpec((TM_IN, D_MODEL), lambda i, j: (i, 0)),
            pl.BlockSpec((1, D_MODEL), lambda i, j: (0, 0)),
            pl.BlockSpec((D_MODEL, TN_IN), lambda i, j: (0, j)),
        ],
        out_specs=pl.BlockSpec((TM_IN, TN_IN), lambda i, j: (i, j)),
        out_shape=jax.ShapeDtypeStruct((t, IN_WIDTH), BF16),
        scratch_shapes=[pltpu.VMEM((TM_IN, D_MODEL), BF16)],
        compiler_params=_cparams(("parallel", "arbitrary")),
        name="inproj",
    )(x2d, g, w)


def _fourier_body(f_ref, cc_ref, sc_ref, cs_ref, wfo_ref, o_ref, z_ref, *, seq):
    st = pl.program_id(1)

    @pl.when(st == 0)
    def _channel_dft():
        for g in range(FOURIER_GROUPS):
            cols = slice(g * FOURIER_GROUP_DIM, (g + 1) * FOURIER_GROUP_DIM)
            xg = f_ref[:, cols]
            z_ref[0:seq, cols] = jnp.dot(xg, cc_ref[...], preferred_element_type=F32).astype(BF16)
            z_ref[seq:2 * seq, cols] = jnp.dot(xg, sc_ref[...], preferred_element_type=F32).astype(BF16)

    y = jnp.dot(cs_ref[...], z_ref[...], preferred_element_type=F32)
    o_ref[...] = jnp.dot(y.astype(BF16), wfo_ref[...], preferred_element_type=F32).astype(o_ref.dtype)


def _dft_tables(seq):
    c = FOURIER_GROUP_DIM
    kc = (np.arange(c)[:, None] * np.arange(c)[None, :]) % c
    ac = 2.0 * np.pi * kc / c
    ks = (np.arange(seq)[:, None] * np.arange(seq)[None, :]) % seq
    a_s = 2.0 * np.pi * ks / seq
    cc = (np.cos(ac) / np.sqrt(c)).astype(np.float32)
    sc = (np.sin(ac) / np.sqrt(c)).astype(np.float32)
    cs = np.concatenate([np.cos(a_s), -np.sin(a_s)], axis=1) / np.sqrt(seq)
    return cc, sc, cs.astype(np.float32)


def _fourier(proj, w_fo, batch, seq):
    cc, sc, cs = _dft_tables(seq)
    cc = jnp.asarray(cc).astype(BF16)
    sc = jnp.asarray(sc).astype(BF16)
    cs = jnp.asarray(cs).astype(BF16)
    t = batch * seq
    return pl.pallas_call(
        functools.partial(_fourier_body, seq=seq),
        grid=(batch, seq // TS_F),
        in_specs=[
            pl.BlockSpec((seq, FOURIER_WIDTH), lambda b, s: (b, COL_F // FOURIER_WIDTH)),
            pl.BlockSpec((FOURIER_GROUP_DIM, FOURIER_GROUP_DIM), lambda b, s: (0, 0)),
            pl.BlockSpec((FOURIER_GROUP_DIM, FOURIER_GROUP_DIM), lambda b, s: (0, 0)),
            pl.BlockSpec((TS_F, 2 * seq), lambda b, s: (s, 0)),
            pl.BlockSpec((FOURIER_WIDTH, D_MODEL), lambda b, s: (0, 0)),
        ],
        out_specs=pl.BlockSpec((TS_F, D_MODEL), lambda b, s: (b * (seq // TS_F) + s, 0)),
        out_shape=jax.ShapeDtypeStruct((t, D_MODEL), BF16),
        scratch_shapes=[pltpu.VMEM((2 * seq, FOURIER_WIDTH), BF16)],
        compiler_params=_cparams(("parallel", "arbitrary")),
        name="fourier",
    )(proj, cc, sc, cs, w_fo)


def _swa_body(sink_ref, q_ref, k_ref, v_ref, bias_ref, o_ref):
    n = pl.program_id(1)
    nb = pl.num_programs(1)
    prev = pl.multiple_of(jnp.maximum(n - 1, 0) * ATT_BLOCK, ATT_BLOCK)
    own = pl.multiple_of(n * ATT_BLOCK, ATT_BLOCK)
    nxt = pl.multiple_of(jnp.minimum(n + 1, nb - 1) * ATT_BLOCK, ATT_BLOCK)

    def band(ref):
        return jnp.concatenate(
            [ref[pl.ds(prev, ATT_BLOCK), :], ref[pl.ds(own, ATT_BLOCK), :], ref[pl.ds(nxt, ATT_BLOCK), :]], axis=0)

    kb = band(k_ref)
    vb = band(v_ref)
    col = lax.broadcasted_iota(jnp.int32, (1, 3 * ATT_BLOCK), 1)
    valid = ((col >= ATT_BLOCK) | (n > 0)) & ((col < 2 * ATT_BLOCK) | (n < nb - 1))

    grp = N_HEADS // N_KV_HEADS
    outs = []
    for g in range(N_KV_HEADS):
        kg = kb[:, g * HEAD_DIM:(g + 1) * HEAD_DIM]
        vg = vb[:, g * HEAD_DIM:(g + 1) * HEAD_DIM]
        for j in range(grp):
            h = g * grp + j
            qh = q_ref[:, h * HEAD_DIM:(h + 1) * HEAD_DIM]
            s = lax.dot_general(qh, kg, (((1,), (1,)), ((), ())), preferred_element_type=F32)
            s = s * np.float32(HEAD_DIM ** -0.5)
            s = jnp.where(valid, s + bias_ref[h], NEG_INF)
            sink = sink_ref[h]
            m = jnp.maximum(jnp.max(s, axis=-1, keepdims=True), sink)
            p = jnp.exp(s - m)
            den = jnp.sum(p, axis=-1, keepdims=True) + jnp.exp(sink - m)
            o = jnp.dot(p.astype(BF16), vg, preferred_element_type=F32)
            outs.append(o / den)
    o_ref[...] = jnp.concatenate(outs, axis=1).astype(o_ref.dtype)


def _t5_bucket(rel):
    nb = NUM_BUCKETS // 2
    max_exact = nb // 2
    ret = (rel > 0).astype(jnp.int32) * nb
    n = jnp.abs(rel)
    nf = jnp.maximum(n, 1).astype(jnp.float32)
    large = max_exact + (jnp.log(nf / max_exact) / np.float32(np.log(MAX_DISTANCE / max_exact))
                         * (nb - max_exact)).astype(jnp.int32)
    large = jnp.minimum(large, nb - 1)
    return ret + jnp.where(n < max_exact, n, large)


def _band_bias(rel_table):
    q_off = jnp.arange(ATT_BLOCK)[:, None]
    k_off = jnp.arange(3 * ATT_BLOCK)[None, :]
    rel = k_off - ATT_BLOCK - q_off
    bias = rel_table.astype(F32)[_t5_bucket(rel)]
    bias = jnp.where((jnp.abs(rel) <= WINDOW)[..., None], bias, NEG_INF)
    return jnp.transpose(bias, (2, 0, 1))


def _swa(proj, sink, bias, batch, seq):
    nb = seq // ATT_BLOCK
    t = batch * seq
    return pl.pallas_call(
        _swa_body,
        grid_spec=pltpu.PrefetchScalarGridSpec(
            num_scalar_prefetch=0,
            grid=(batch, nb),
            in_specs=[
                pl.BlockSpec(memory_space=pltpu.SMEM),
                pl.BlockSpec((ATT_BLOCK, ATTN_WIDTH), lambda b, n: (b * nb + n, COL_Q // ATTN_WIDTH)),
                pl.BlockSpec((seq, KV_WIDTH), lambda b, n: (b, COL_K // KV_WIDTH)),
                pl.BlockSpec((seq, KV_WIDTH), lambda b, n: (b, COL_V // KV_WIDTH)),
                pl.BlockSpec((N_HEADS, ATT_BLOCK, 3 * ATT_BLOCK), lambda b, n: (0, 0, 0)),
            ],
            out_specs=pl.BlockSpec((ATT_BLOCK, ATTN_WIDTH), lambda b, n: (b * nb + n, 0)),
        ),
        out_shape=jax.ShapeDtypeStruct((t, ATTN_WIDTH), BF16),
        compiler_params=_cparams(("parallel", "arbitrary")),
        name="swa",
    )(sink, proj, proj, proj, bias)


def _memkv_body(m_ref, w_ref, o_ref):
    o_ref[...] = jnp.dot(m_ref[...].astype(BF16), w_ref[...], preferred_element_type=F32).astype(o_ref.dtype)


def _memkv(mem2d, w):
    rows = mem2d.shape[0]
    tm, tn = 512, 1024
    return pl.pallas_call(
        _memkv_body,
        grid=(rows // tm, (2 * MEM_WIDTH) // tn),
        in_specs=[
            pl.BlockSpec((tm, D_MODEL), lambda i, j: (i, 0)),
            pl.BlockSpec((D_MODEL, tn), lambda i, j: (0, j)),
        ],
        out_specs=pl.BlockSpec((tm, tn), lambda i, j: (i, j)),
        out_shape=jax.ShapeDtypeStruct((rows, 2 * MEM_WIDTH), BF16),
        compiler_params=_cparams(("parallel", "arbitrary")),
        name="memkv",
    )(mem2d, w)


def _memattn_body(q_ref, k_ref, v_ref, o_ref):
    outs = []
    for h in range(MEM_HEADS):
        cols = slice(h * MEM_HEAD_DIM, (h + 1) * MEM_HEAD_DIM)
        s = lax.dot_general(q_ref[:, cols], k_ref[:, cols], (((1,), (1,)), ((), ())), preferred_element_type=F32)
        s = s * np.float32(MEM_HEAD_DIM ** -0.5)
        m = jnp.max(s, axis=-1, keepdims=True)
        p = jnp.exp(s - m)
        den = jnp.sum(p, axis=-1, keepdims=True)
        o = jnp.dot(p.astype(BF16), v_ref[:, cols], preferred_element_type=F32)
        outs.append(o / den)
    o_ref[...] = jnp.concatenate(outs, axis=1).astype(o_ref.dtype)


def _memattn(proj, kv, batch, seq):
    t = batch * seq
    nq = seq // TQ_MEM
    return pl.pallas_call(
        _memattn_body,
        grid=(batch, nq),
        in_specs=[
            pl.BlockSpec((TQ_MEM, MEM_WIDTH), lambda b, i: (b * nq + i, COL_MQ // MEM_WIDTH)),
            pl.BlockSpec((MEM_LEN, MEM_WIDTH), lambda b, i: (b, 0)),
            pl.BlockSpec((MEM_LEN, MEM_WIDTH), lambda b, i: (b, 1)),
        ],
        out_specs=pl.BlockSpec((TQ_MEM, MEM_WIDTH), lambda b, i: (b * nq + i, 0)),
        out_shape=jax.ShapeDtypeStruct((t, MEM_WIDTH), BF16),
        compiler_params=_cparams(("parallel", "arbitrary")),
        name="memattn",
    )(proj, kv, kv)


def _merge_body(yf_ref, att_ref, mo_ref, g0_ref, g1_ref, g2_ref, x_ref, wao_ref, wmo_ref, wo_ref,
                n2_ref, wr_ref, br_ref,
                x1_ref, h2s_ref, te_ref, tp_ref, tw_ref, cnt_ref, carry_ref):
    i = pl.program_id(0)
    tm = x_ref.shape[0]

    @pl.when(i == 0)
    def _init():
        carry_ref[...] = jnp.zeros_like(carry_ref)

    y_a = jnp.dot(att_ref[...], wao_ref[...], preferred_element_type=F32)
    y_m = jnp.dot(mo_ref[...], wmo_ref[...], preferred_element_type=F32)
    mixed = (g0_ref[...].astype(F32) * yf_ref[...].astype(F32)
             + g1_ref[...].astype(F32) * y_a + g2_ref[...].astype(F32) * y_m)
    x1 = x_ref[...] + jnp.dot(mixed.astype(BF16), wo_ref[...], preferred_element_type=F32)
    x1_ref[...] = x1

    inv = lax.rsqrt(jnp.mean(x1 * x1, axis=-1, keepdims=True) + RMS_EPS)
    h2 = x1 * inv * n2_ref[...]
    for c in range(ROW_CHUNKS):
        h2s_ref[pl.ds(c, tm, stride=ROW_CHUNKS), :] = h2[:, c * LANES:(c + 1) * LANES]

    logits = jnp.dot(h2, wr_ref[...], preferred_element_type=F32, precision=lax.Precision.HIGHEST) + br_ref[...]
    lane = lax.broadcasted_iota(jnp.int32, (tm, N_EXPERTS), 1)
    vals, idxs, hots = [], [], []
    rest = logits
    for _ in range(TOP_K):
        mk = jnp.max(rest, axis=-1, keepdims=True)
        ik = jnp.min(jnp.where(rest == mk, lane, N_EXPERTS), axis=-1, keepdims=True)
        hot = lane == ik
        vals.append(mk)
        idxs.append(ik)
        hots.append(hot)
        rest = jnp.where(hot, -jnp.inf, rest)
    exps = [jnp.exp(v - vals[0]) for v in vals]
    den = exps[0] + exps[1] + exps[2] + exps[3]

    sel = (hots[0] | hots[1] | hots[2] | hots[3]).astype(F32)
    r_i = lax.broadcasted_iota(jnp.int32, (tm, tm), 0)
    c_i = lax.broadcasted_iota(jnp.int32, (tm, tm), 1)
    tri = (c_i < r_i).astype(BF16)
    pos = jnp.dot(tri, sel.astype(BF16), preferred_element_type=F32) + carry_ref[...]
    poss = [jnp.sum(jnp.where(hot, pos, 0.0), axis=-1, keepdims=True) for hot in hots]
    carry = carry_ref[...] + jnp.sum(sel, axis=0, keepdims=True)
    carry_ref[...] = carry

    te_ref[...] = jnp.concatenate(idxs, axis=1)
    tp_ref[...] = jnp.concatenate(poss, axis=1).astype(jnp.int32)
    tw_ref[...] = jnp.concatenate([e / den for e in exps], axis=1)
    cnt_ref[...] = carry.astype(jnp.int32)


def _merge(y_f, att, mo, proj, x2d, w_ao, w_mo, w_o, n2, w_r, b_r):
    t = x2d.shape[0]
    tm = TM_MERGE
    row = lambda i: (i, 0)
    const = lambda i: (0, 0)
    resident = dict(pipeline_mode=pl.Buffered(1))
    return pl.pallas_call(
        _merge_body,
        grid=(t // tm,),
        in_specs=[
            pl.BlockSpec((tm, D_MODEL), row),
            pl.BlockSpec((tm, ATTN_WIDTH), row),
            pl.BlockSpec((tm, MEM_WIDTH), row),
            pl.BlockSpec((tm, D_MODEL), lambda i: (i, 0)),
            pl.BlockSpec((tm, D_MODEL), lambda i: (i, 1)),
            pl.BlockSpec((tm, D_MODEL), lambda i: (i, 2)),
            pl.BlockSpec((tm, D_MODEL), row),
            pl.BlockSpec((ATTN_WIDTH, D_MODEL), const, **resident),
            pl.BlockSpec((MEM_WIDTH, D_MODEL), const, **resident),
            pl.BlockSpec((D_MODEL, D_MODEL), const, **resident),
            pl.BlockSpec((1, D_MODEL), const),
            pl.BlockSpec((D_MODEL, N_EXPERTS), const),
            pl.BlockSpec((1, N_EXPERTS), const),
        ],
        out_specs=[
            pl.BlockSpec((tm, D_MODEL), row),
            pl.BlockSpec((tm * ROW_CHUNKS, LANES), row),
            pl.BlockSpec((tm, TOP_K), row),
            pl.BlockSpec((tm, TOP_K), row),
            pl.BlockSpec((tm, TOP_K), row),
            pl.BlockSpec((1, N_EXPERTS), const),
        ],
        out_shape=[
            jax.ShapeDtypeStruct((t, D_MODEL), F32),
            jax.ShapeDtypeStruct((t * ROW_CHUNKS, LANES), F32),
            jax.ShapeDtypeStruct((t, TOP_K), jnp.int32),
            jax.ShapeDtypeStruct((t, TOP_K), jnp.int32),
            jax.ShapeDtypeStruct((t, TOP_K), F32),
            jax.ShapeDtypeStruct((1, N_EXPERTS), jnp.int32),
        ],
        scratch_shapes=[pltpu.VMEM((1, N_EXPERTS), F32)],
        compiler_params=_cparams(("arbitrary",)),
        name="merge",
    )(y_f, att, mo, proj, proj, proj, x2d, w_ao, w_mo, w_o, n2, w_r, b_r)


def _row_slice(ref, row, n_rows=1):
    start = pl.multiple_of(row * ROW_CHUNKS, ROW_CHUNKS)
    return ref.at[pl.ds(start, n_rows * ROW_CHUNKS), :]


def _dispatch_body(zstart_ref, dest_ref, h_ref, xs_ref, zero_ref, zsem, sem):
    i = pl.program_id(0)
    tm = h_ref.shape[0] // ROW_CHUNKS

    @pl.when(i == 0)
    def _zero_tails():
        zero_ref[...] = jnp.zeros_like(zero_ref)

        def tail_copies(e):
            s = zstart_ref[e]
            return [pltpu.make_async_copy(zero_ref, _row_slice(xs_ref, s + q * ZERO_ROWS, ZERO_ROWS), zsem)
                    for q in range(TM_E // ZERO_ROWS)]

        def start(e, c):
            @pl.when(zstart_ref[e] >= 0)
            def _():
                for cp in tail_copies(e):
                    cp.start()
            return c

        def wait(e, c):
            @pl.when(zstart_ref[e] >= 0)
            def _():
                for cp in tail_copies(e):
                    cp.wait()
            return c

        lax.fori_loop(0, zstart_ref.shape[0], start, 0)
        lax.fori_loop(0, zstart_ref.shape[0], wait, 0)

    def scatter(t, c):
        for k in range(TOP_K):
            d = dest_ref[t * TOP_K + k]
            pltpu.make_async_copy(_row_slice(h_ref, t), _row_slice(xs_ref, d), sem).start()
        return c

    lax.fori_loop(0, tm, scatter, 0)
    for _ in range(TOP_K):
        pltpu.make_async_copy(h_ref, _row_slice(xs_ref, 0, tm), sem).wait()


def _dispatch(zstart, dest_flat, h2s, n_slots):
    t = h2s.shape[0] // ROW_CHUNKS
    tm = TM_DISP
    return pl.pallas_call(
        _dispatch_body,
        grid_spec=pltpu.PrefetchScalarGridSpec(
            num_scalar_prefetch=1,
            grid=(t // tm,),
            in_specs=[
                pl.BlockSpec((tm * TOP_K,), lambda i, z: (i,), memory_space=pltpu.SMEM),
                pl.BlockSpec((tm * ROW_CHUNKS, LANES), lambda i, z: (i, 0)),
            ],
            out_specs=pl.BlockSpec(memory_space=pl.ANY),
            scratch_shapes=[
                pltpu.VMEM((ZERO_ROWS * ROW_CHUNKS, LANES), F32),
                pltpu.SemaphoreType.DMA(()),
                pltpu.SemaphoreType.DMA(()),
            ],
        ),
        out_shape=jax.ShapeDtypeStruct((n_slots * ROW_CHUNKS, LANES), F32),
        compiler_params=_cparams(("arbitrary",)),
        name="dispatch",
    )(zstart, dest_flat, h2s)


def _experts_body(be_ref, nused_ref, xs_ref, wg_ref, wu_ref, wd_ref, bg_ref, bu_ref, bd_ref, o_ref,
                  xb_ref, acc_ref):
    i = pl.program_id(0)
    f = pl.program_id(1)
    nf = pl.num_programs(1)
    tm = xb_ref.shape[0]

    @pl.when((i >= nused_ref[0]) & (f == 0))
    def _unused_block():
        o_ref[...] = jnp.zeros_like(o_ref)

    @pl.when(i < nused_ref[0])
    def _block():
        @pl.when(f == 0)
        def _load_rows():
            for c in range(ROW_CHUNKS):
                xb_ref[:, c * LANES:(c + 1) * LANES] = xs_ref[pl.ds(c, tm, stride=ROW_CHUNKS), :].astype(BF16)

        xb = xb_ref[...]
        gate = jnp.dot(xb, wg_ref[0], preferred_element_type=F32) + bg_ref[0]
        up = jnp.dot(xb, wu_ref[0], preferred_element_type=F32) + bu_ref[0]
        gate = jnp.minimum(gate, SWIGLU_LIMIT)
        up = jnp.clip(up, -SWIGLU_LIMIT, SWIGLU_LIMIT)
        glu = gate * (1.0 / (1.0 + jnp.exp(-gate * SWIGLU_ALPHA)))
        act = ((up + 1.0) * glu).astype(BF16)
        part = jnp.dot(act, wd_ref[0], preferred_element_type=F32)

        @pl.when(f == 0)
        def _first():
            acc_ref[...] = part

        @pl.when(f > 0)
        def _rest():
            acc_ref[...] += part

        @pl.when(f == nf - 1)
        def _store():
            y = acc_ref[...] + bd_ref[0]
            for c in range(ROW_CHUNKS):
                o_ref[pl.ds(c, tm, stride=ROW_CHUNKS), :] = y[:, c * LANES:(c + 1) * LANES]


def _experts(block_e, nused, xs, wg, wu, wd, bg, bu, bd, n_slots):
    nblk = n_slots // TM_E
    nf = D_FF // TF_E

    def blk(i, f, be, nu):
        return jnp.minimum(i, nu[0] - 1)

    return pl.pallas_call(
        _experts_body,
        grid_spec=pltpu.PrefetchScalarGridSpec(
            num_scalar_prefetch=2,
            grid=(nblk, nf),
            in_specs=[
                pl.BlockSpec((TM_E * ROW_CHUNKS, LANES), lambda i, f, be, nu: (blk(i, f, be, nu), 0)),
                pl.BlockSpec((1, D_MODEL, TF_E), lambda i, f, be, nu: (be[i], 0, jnp.where(i < nu[0], f, nf - 1))),
                pl.BlockSpec((1, D_MODEL, TF_E), lambda i, f, be, nu: (be[i], 0, jnp.where(i < nu[0], f, nf - 1))),
                pl.BlockSpec((1, TF_E, D_MODEL), lambda i, f, be, nu: (be[i], jnp.where(i < nu[0], f, nf - 1), 0)),
                pl.BlockSpec((1, 1, TF_E), lambda i, f, be, nu: (be[i], 0, jnp.where(i < nu[0], f, nf - 1))),
                pl.BlockSpec((1, 1, TF_E), lambda i, f, be, nu: (be[i], 0, jnp.where(i < nu[0], f, nf - 1))),
                pl.BlockSpec((1, 1, D_MODEL), lambda i, f, be, nu: (be[i], 0, 0)),
            ],
            out_specs=pl.BlockSpec((TM_E * ROW_CHUNKS, LANES), lambda i, f, be, nu: (i, 0)),
            scratch_shapes=[pltpu.VMEM((TM_E, D_MODEL), BF16), pltpu.VMEM((TM_E, D_MODEL), F32)],
        ),
        out_shape=jax.ShapeDtypeStruct((n_slots * ROW_CHUNKS, LANES), F32),
        compiler_params=_cparams(("arbitrary", "arbitrary")),
        name="experts",
    )(block_e, nused, xs, wg, wu, wd, bg, bu, bd)


def _combine_body(dest_ref, ys_ref, w_ref, x1_ref, g_ref, o_ref, buf_ref, sem):
    tm = x1_ref.shape[0]

    def gather(t, c):
        for k in range(TOP_K):
            j = t * TOP_K + k
            pltpu.make_async_copy(_row_slice(ys_ref, dest_ref[j]), _row_slice(buf_ref, j), sem).start()
        return c

    lax.fori_loop(0, tm, gather, 0)
    pltpu.make_async_copy(_row_slice(ys_ref, 0, tm * TOP_K), buf_ref, sem).wait()

    w = w_ref[...]
    cols = []
    for c in range(ROW_CHUNKS):
        acc = None
        for k in range(TOP_K):
            rows = buf_ref[pl.ds(k * ROW_CHUNKS + c, tm, stride=TOP_K * ROW_CHUNKS), :]
            term = rows * w[:, k:k + 1]
            acc = term if acc is None else acc + term
        cols.append(acc)
    x2 = x1_ref[...] + jnp.concatenate(cols, axis=1)
    inv = lax.rsqrt(jnp.mean(x2 * x2, axis=-1, keepdims=True) + RMS_EPS)
    o_ref[...] = x2 * inv * g_ref[...]


def _combine(dest_flat, ys, top_w, x1, g):
    t = x1.shape[0]
    tm = TM_COMB
    return pl.pallas_call(
        _combine_body,
        grid=(t // tm,),
        in_specs=[
            pl.BlockSpec((tm * TOP_K,), lambda i: (i,), memory_space=pltpu.SMEM),
            pl.BlockSpec(memory_space=pl.ANY),
            pl.BlockSpec((tm, TOP_K), lambda i: (i, 0)),
            pl.BlockSpec((tm, D_MODEL), lambda i: (i, 0)),
            pl.BlockSpec((1, D_MODEL), lambda i: (0, 0)),
        ],
        out_specs=pl.BlockSpec((tm, D_MODEL), lambda i: (i, 0)),
        out_shape=jax.ShapeDtypeStruct((t, D_MODEL), F32),
        scratch_shapes=[pltpu.VMEM((tm * TOP_K * ROW_CHUNKS, LANES), F32), pltpu.SemaphoreType.DMA(())],
        compiler_params=_cparams(("arbitrary",)),
        name="combine",
    )(dest_flat, ys, top_w, x1, g)


def _routing_tables(counts, top_e, top_pos, n_slots):
    nblk = n_slots // TM_E
    padded = (counts + TM_E - 1) // TM_E * TM_E
    ends = jnp.cumsum(padded)
    starts = ends - padded
    dest = (starts[top_e] + top_pos).reshape(-1).astype(jnp.int32)
    nused = (ends[-1] // TM_E).astype(jnp.int32).reshape(1)
    block_e = jnp.searchsorted(ends, jnp.arange(nblk, dtype=jnp.int32) * TM_E, side="right")
    block_e = jnp.minimum(block_e, N_EXPERTS - 1).astype(jnp.int32)
    last_e = block_e[jnp.maximum(nused[0] - 1, 0)]
    block_e = jnp.where(jnp.arange(nblk) < nused[0], block_e, last_e)
    unused = (nused[0] + jnp.arange(N_EXPERTS)) * TM_E
    zstart = jnp.concatenate([jnp.where(padded > 0, ends - TM_E, -1),
                              jnp.where(unused < n_slots, unused, -1)]).astype(jnp.int32)
    return dest, nused, block_e, zstart


def kernel(x, mem, rel_bias_table, norm1_g, w_in, attn_sink, w_mem_kv, w_fourier_out, w_attn_out, w_mem_out,
           w_o, norm2_g, w_router, b_router, w_gate_up, b_gate_up, w_down, b_down, final_g):
    batch, seq, _ = x.shape
    t = batch * seq
    depth = norm1_g.shape[0]
    assert depth == 1, "the combine kernel applies the final RMSNorm, so a single layer is supported"
    x2d = x.reshape(t, D_MODEL)
    mem2d = mem.reshape(batch * MEM_LEN, D_MODEL)
    bias = _band_bias(rel_bias_table)
    n_slots = t * TOP_K + N_EXPERTS * TM_E

    for l in range(depth):
        wi = w_in[l]
        wi = jnp.concatenate([wi[:, _OFF_GATE:], wi[:, :_OFF_Q], wi[:, _OFF_MQ:_OFF_GATE],
                              wi[:, _OFF_Q:_OFF_K], wi[:, _OFF_K:_OFF_V], wi[:, _OFF_V:_OFF_MQ]], axis=1).astype(BF16)
        proj = _inproj(x2d, norm1_g[l].reshape(1, D_MODEL), wi)
        y_f = _fourier(proj, w_fourier_out[l].astype(BF16), batch, seq)
        att = _swa(proj, attn_sink[l], bias, batch, seq)
        kv = _memkv(mem2d, w_mem_kv[l].astype(BF16))
        mo = _memattn(proj, kv, batch, seq)
        x1, h2s, top_e, top_pos, top_w, counts = _merge(
            y_f, att, mo, proj, x2d, w_attn_out[l].astype(BF16), w_mem_out[l].astype(BF16), w_o[l].astype(BF16),
            norm2_g[l].reshape(1, D_MODEL), w_router[l], b_router[l].reshape(1, N_EXPERTS))
        dest, nused, block_e, zstart = _routing_tables(counts[0], top_e, top_pos, n_slots)
        xs = _dispatch(zstart, dest, h2s, n_slots)
        wg = w_gate_up[l][:, :, 0::2].astype(BF16)
        wu = w_gate_up[l][:, :, 1::2].astype(BF16)
        bg = b_gate_up[l][:, 0::2].reshape(N_EXPERTS, 1, D_FF)
        bu = b_gate_up[l][:, 1::2].reshape(N_EXPERTS, 1, D_FF)
        ys = _experts(block_e, nused, xs, wg, wu, w_down[l].astype(BF16), bg, bu,
                      b_down[l].reshape(N_EXPERTS, 1, D_MODEL), n_slots)
        x2d = _combine(dest, ys, top_w, x1, final_g.reshape(1, D_MODEL))
    return x2d.reshape(batch, seq, D_MODEL)
```

```python
import functools

import numpy as np
import jax
import jax.numpy as jnp
from jax import lax
from jax.experimental import pallas as pl
from jax.experimental.pallas import tpu as pltpu

D_MODEL = 2048
MEM_LEN = 256
FOURIER_GROUPS = 4
FOURIER_GROUP_DIM = 256
FOURIER_WIDTH = 1024
N_HEADS = 32
N_KV_HEADS = 4
HEAD_DIM = 64
ATTN_WIDTH = 2048
KV_WIDTH = 256
WINDOW = 128
ATT_BLOCK = 128
MEM_HEADS = 4
MEM_HEAD_DIM = 256
MEM_WIDTH = 1024
N_BRANCHES = 3
NUM_BUCKETS = 32
MAX_DISTANCE = 128
N_EXPERTS = 32
TOP_K = 4
D_FF = 2048
SWIGLU_LIMIT = 7.0
SWIGLU_ALPHA = 1.702
RMS_EPS = 1e-5
NEG_INF = -1e30

GATE_WIDTH = N_BRANCHES * D_MODEL
COL_GATE = 0
COL_F = GATE_WIDTH
COL_MQ = COL_F + FOURIER_WIDTH
COL_Q = COL_MQ + MEM_WIDTH
COL_K = COL_Q + ATTN_WIDTH
COL_V = COL_K + KV_WIDTH
IN_WIDTH = COL_V + KV_WIDTH

_OFF_Q = FOURIER_WIDTH
_OFF_K = _OFF_Q + ATTN_WIDTH
_OFF_V = _OFF_K + KV_WIDTH
_OFF_MQ = _OFF_V + KV_WIDTH
_OFF_GATE = _OFF_MQ + MEM_WIDTH

LANES = 128
ROW_CHUNKS = D_MODEL // LANES
VMEM_LIMIT = 56 * 1024 * 1024

TM_IN, TN_IN = 512, 1536
TS_F = 512
TQ_MEM = 512
TM_MERGE = 256
TM_DISP = 256
TM_E, TF_E = 512, 256
TM_COMB = 128
ZERO_ROWS = 128

BF16 = jnp.bfloat16
F32 = jnp.float32


def _cparams(sem):
    return pltpu.CompilerParams(dimension_semantics=sem, vmem_limit_bytes=VMEM_LIMIT)


def _inproj_body(x_ref, g_ref, w_ref, o_ref, h_ref, *, n_gate_tiles):
    j = pl.program_id(1)

    @pl.when(j == 0)
    def _norm():
        x = x_ref[...]
        inv = lax.rsqrt(jnp.mean(x * x, axis=-1, keepdims=True) + RMS_EPS)
        h_ref[...] = (x * inv * g_ref[...]).astype(BF16)

    acc = jnp.dot(h_ref[...], w_ref[...], preferred_element_type=F32)

    @pl.when(j < n_gate_tiles)
    def _gate():
        o_ref[...] = (1.0 / (1.0 + jnp.exp(-acc))).astype(o_ref.dtype)

    @pl.when(j >= n_gate_tiles)
    def _plain():
        o_ref[...] = acc.astype(o_ref.dtype)


def _inproj(x2d, g, w):
    t = x2d.shape[0]
    return pl.pallas_call(
        functools.partial(_inproj_body, n_gate_tiles=GATE_WIDTH // TN_IN),
        grid=(t // TM_IN, IN_WIDTH // TN_IN),
        in_specs=[
            pl.BlockSpec((TM_IN, D_MODEL), lambda i, j: (i, 0)),
            pl.BlockSpec((1, D_MODEL), lambda i, j: (0, 0)),
            pl.BlockSpec((D_MODEL, TN_IN), lambda i, j: (0, j)),
        ],
        out_specs=pl.BlockSpec((TM_IN, TN_IN), lambda i, j: (i, j)),
        out_shape=jax.ShapeDtypeStruct((t, IN_WIDTH), BF16),
        scratch_shapes=[pltpu.VMEM((TM_IN, D_MODEL), BF16)],
        compiler_params=_cparams(("parallel", "arbitrary")),
        name="inproj",
    )(x2d, g, w)


def _fourier_body(f_ref, cc_ref, sc_ref, cs_ref, wfo_ref, o_ref, z_ref, *, seq):
    st = pl.program_id(1)

    @pl.when(st == 0)
    def _channel_dft():
        for g in range(FOURIER_GROUPS):
            cols = slice(g * FOURIER_GROUP_DIM, (g + 1) * FOURIER_GROUP_DIM)
            xg = f_ref[:, cols]
            z_ref[0:seq, cols] = jnp.dot(xg, cc_ref[...], preferred_element_type=F32).astype(BF16)
            z_ref[seq:2 * seq, cols] = jnp.dot(xg, sc_ref[...], preferred_element_type=F32).astype(BF16)

    y = jnp.dot(cs_ref[...], z_ref[...], preferred_element_type=F32)
    o_ref[...] = jnp.dot(y.astype(BF16), wfo_ref[...], preferred_element_type=F32).astype(o_ref.dtype)


def _dft_tables(seq):
    c = FOURIER_GROUP_DIM
    kc = (np.arange(c)[:, None] * np.arange(c)[None, :]) % c
    ac = 2.0 * np.pi * kc / c
    ks = (np.arange(seq)[:, None] * np.arange(seq)[None, :]) % seq
    a_s = 2.0 * np.pi * ks / seq
    cc = (np.cos(ac) / np.sqrt(c)).astype(np.float32)
    sc = (np.sin(ac) / np.sqrt(c)).astype(np.float32)
    cs = np.concatenate([np.cos(a_s), -np.sin(a_s)], axis=1) / np.sqrt(seq)
    return cc, sc, cs.astype(np.float32)


def _fourier(proj, w_fo, batch, seq):
    cc, sc, cs = _dft_tables(seq)
    cc = jnp.asarray(cc).astype(BF16)
    sc = jnp.asarray(sc).astype(BF16)
    cs = jnp.asarray(cs).astype(BF16)
    t = batch * seq
    return pl.pallas_call(
        functools.partial(_fourier_body, seq=seq),
        grid=(batch, seq // TS_F),
        in_specs=[
            pl.BlockSpec((seq, FOURIER_WIDTH), lambda b, s: (b, COL_F // FOURIER_WIDTH)),
            pl.BlockSpec((FOURIER_GROUP_DIM, FOURIER_GROUP_DIM), lambda b, s: (0, 0)),
            pl.BlockSpec((FOURIER_GROUP_DIM, FOURIER_GROUP_DIM), lambda b, s: (0, 0)),
            pl.BlockSpec((TS_F, 2 * seq), lambda b, s: (s, 0)),
            pl.BlockSpec((FOURIER_WIDTH, D_MODEL), lambda b, s: (0, 0)),
        ],
        out_specs=pl.BlockSpec((TS_F, D_MODEL), lambda b, s: (b * (seq // TS_F) + s, 0)),
        out_shape=jax.ShapeDtypeStruct((t, D_MODEL), BF16),
        scratch_shapes=[pltpu.VMEM((2 * seq, FOURIER_WIDTH), BF16)],
        compiler_params=_cparams(("parallel", "arbitrary")),
        name="fourier",
    )(proj, cc, sc, cs, w_fo)


def _swa_body(sink_ref, q_ref, k_ref, v_ref, bias_ref, o_ref):
    n = pl.program_id(1)
    nb = pl.num_programs(1)
    prev = pl.multiple_of(jnp.maximum(n - 1, 0) * ATT_BLOCK, ATT_BLOCK)
    own = pl.multiple_of(n * ATT_BLOCK, ATT_BLOCK)
    nxt = pl.multiple_of(jnp.minimum(n + 1, nb - 1) * ATT_BLOCK, ATT_BLOCK)

    def band(ref):
        return jnp.concatenate(
            [ref[pl.ds(prev, ATT_BLOCK), :], ref[pl.ds(own, ATT_BLOCK), :], ref[pl.ds(nxt, ATT_BLOCK), :]], axis=0)

    kb = band(k_ref)
    vb = band(v_ref)
    col = lax.broadcasted_iota(jnp.int32, (1, 3 * ATT_BLOCK), 1)
    valid = ((col >= ATT_BLOCK) | (n > 0)) & ((col < 2 * ATT_BLOCK) | (n < nb - 1))

    grp = N_HEADS // N_KV_HEADS
    outs = []
    for g in range(N_KV_HEADS):
        kg = kb[:, g * HEAD_DIM:(g + 1) * HEAD_DIM]
        vg = vb[:, g * HEAD_DIM:(g + 1) * HEAD_DIM]
        for j in range(grp):
            h = g * grp + j
            qh = q_ref[:, h * HEAD_DIM:(h + 1) * HEAD_DIM]
            s = lax.dot_general(qh, kg, (((1,), (1,)), ((), ())), preferred_element_type=F32)
            s = s * np.float32(HEAD_DIM ** -0.5)
            s = jnp.where(valid, s + bias_ref[h], NEG_INF)
            sink = sink_ref[h]
            m = jnp.maximum(jnp.max(s, axis=-1, keepdims=True), sink)
            p = jnp.exp(s - m)
            den = jnp.sum(p, axis=-1, keepdims=True) + jnp.exp(sink - m)
            o = jnp.dot(p.astype(BF16), vg, preferred_element_type=F32)
            outs.append(o / den)
    o_ref[...] = jnp.concatenate(outs, axis=1).astype(o_ref.dtype)


def _t5_bucket(rel):
    nb = NUM_BUCKETS // 2
    max_exact = nb // 2
    ret = (rel > 0).astype(jnp.int32) * nb
    n = jnp.abs(rel)
    nf = jnp.maximum(n, 1).astype(jnp.float32)
    large = max_exact + (jnp.log(nf / max_exact) / np.float32(np.log(MAX_DISTANCE / max_exact))
                         * (nb - max_exact)).astype(jnp.int32)
    large = jnp.minimum(large, nb - 1)
    return ret + jnp.where(n < max_exact, n, large)


def _band_bias(rel_table):
    q_off = jnp.arange(ATT_BLOCK)[:, None]
    k_off = jnp.arange(3 * ATT_BLOCK)[None, :]
    rel = k_off - ATT_BLOCK - q_off
    bias = rel_table.astype(F32)[_t5_bucket(rel)]
    bias = jnp.where((jnp.abs(rel) <= WINDOW)[..., None], bias, NEG_INF)
    return jnp.transpose(bias, (2, 0, 1))


def _swa(proj, sink, bias, batch, seq):
    nb = seq // ATT_BLOCK
    t = batch * seq
    return pl.pallas_call(
        _swa_body,
        grid_spec=pltpu.PrefetchScalarGridSpec(
            num_scalar_prefetch=0,
            grid=(batch, nb),
            in_specs=[
                pl.BlockSpec(memory_space=pltpu.SMEM),
                pl.BlockSpec((ATT_BLOCK, ATTN_WIDTH), lambda b, n: (b * nb + n, COL_Q // ATTN_WIDTH)),
                pl.BlockSpec((seq, KV_WIDTH), lambda b, n: (b, COL_K // KV_WIDTH)),
                pl.BlockSpec((seq, KV_WIDTH), lambda b, n: (b, COL_V // KV_WIDTH)),
                pl.BlockSpec((N_HEADS, ATT_BLOCK, 3 * ATT_BLOCK), lambda b, n: (0, 0, 0)),
            ],
            out_specs=pl.BlockSpec((ATT_BLOCK, ATTN_WIDTH), lambda b, n: (b * nb + n, 0)),
        ),
        out_shape=jax.ShapeDtypeStruct((t, ATTN_WIDTH), BF16),
        compiler_params=_cparams(("parallel", "arbitrary")),
        name="swa",
    )(sink, proj, proj, proj, bias)


def _memkv_body(m_ref, w_ref, o_ref):
    o_ref[...] = jnp.dot(m_ref[...].astype(BF16), w_ref[...], preferred_element_type=F32).astype(o_ref.dtype)


def _memkv(mem2d, w):
    rows = mem2d.shape[0]
    tm, tn = 512, 1024
    return pl.pallas_call(
        _memkv_body,
        grid=(rows // tm, (2 * MEM_WIDTH) // tn),
        in_specs=[
            pl.BlockSpec((tm, D_MODEL), lambda i, j: (i, 0)),
            pl.BlockSpec((D_MODEL, tn), lambda i, j: (0, j)),
        ],
        out_specs=pl.BlockSpec((tm, tn), lambda i, j: (i, j)),
        out_shape=jax.ShapeDtypeStruct((rows, 2 * MEM_WIDTH), BF16),
        compiler_params=_cparams(("parallel", "arbitrary")),
        name="memkv",
    )(mem2d, w)


def _memattn_body(q_ref, k_ref, v_ref, o_ref):
    outs = []
    for h in range(MEM_HEADS):
        cols = slice(h * MEM_HEAD_DIM, (h + 1) * MEM_HEAD_DIM)
        s = lax.dot_general(q_ref[:, cols], k_ref[:, cols], (((1,), (1,)), ((), ())), preferred_element_type=F32)
        s = s * np.float32(MEM_HEAD_DIM ** -0.5)
        m = jnp.max(s, axis=-1, keepdims=True)
        p = jnp.exp(s - m)
        den = jnp.sum(p, axis=-1, keepdims=True)
        o = jnp.dot(p.astype(BF16), v_ref[:, cols], preferred_element_type=F32)
        outs.append(o / den)
    o_ref[...] = jnp.concatenate(outs, axis=1).astype(o_ref.dtype)


def _memattn(proj, kv, batch, seq):
    t = batch * seq
    nq = seq // TQ_MEM
    return pl.pallas_call(
        _memattn_body,
        grid=(batch, nq),
        in_specs=[
            pl.BlockSpec((TQ_MEM, MEM_WIDTH), lambda b, i: (b * nq + i, COL_MQ // MEM_WIDTH)),
            pl.BlockSpec((MEM_LEN, MEM_WIDTH), lambda b, i: (b, 0)),
            pl.BlockSpec((MEM_LEN, MEM_WIDTH), lambda b, i: (b, 1)),
        ],
        out_specs=pl.BlockSpec((TQ_MEM, MEM_WIDTH), lambda b, i: (b * nq + i, 0)),
        out_shape=jax.ShapeDtypeStruct((t, MEM_WIDTH), BF16),
        compiler_params=_cparams(("parallel", "arbitrary")),
        name="memattn",
    )(proj, kv, kv)


def _merge_body(yf_ref, att_ref, mo_ref, g0_ref, g1_ref, g2_ref, x_ref, wao_ref, wmo_ref, wo_ref,
                n2_ref, wr_ref, br_ref,
                x1_ref, h2s_ref, te_ref, tp_ref, tw_ref, cnt_ref, carry_ref):
    i = pl.program_id(0)
    tm = x_ref.shape[0]

    @pl.when(i == 0)
    def _init():
        carry_ref[...] = jnp.zeros_like(carry_ref)

    y_a = jnp.dot(att_ref[...], wao_ref[...], preferred_element_type=F32)
    y_m = jnp.dot(mo_ref[...], wmo_ref[...], preferred_element_type=F32)
    mixed = (g0_ref[...].astype(F32) * yf_ref[...].astype(F32)
             + g1_ref[...].astype(F32) * y_a + g2_ref[...].astype(F32) * y_m)
    x1 = x_ref[...] + jnp.dot(mixed.astype(BF16), wo_ref[...], preferred_element_type=F32)
    x1_ref[...] = x1

    inv = lax.rsqrt(jnp.mean(x1 * x1, axis=-1, keepdims=True) + RMS_EPS)
    h2 = x1 * inv * n2_ref[...]
    for c in range(ROW_CHUNKS):
        h2s_ref[pl.ds(c, tm, stride=ROW_CHUNKS), :] = h2[:, c * LANES:(c + 1) * LANES]

    logits = jnp.dot(h2, wr_ref[...], preferred_element_type=F32, precision=lax.Precision.HIGHEST) + br_ref[...]
    lane = lax.broadcasted_iota(jnp.int32, (tm, N_EXPERTS), 1)
    vals, idxs, hots = [], [], []
    rest = logits
    for _ in range(TOP_K):
        mk = jnp.max(rest, axis=-1, keepdims=True)
        ik = jnp.min(jnp.where(rest == mk, lane, N_EXPERTS), axis=-1, keepdims=True)
        hot = lane == ik
        vals.append(mk)
        idxs.append(ik)
        hots.append(hot)
        rest = jnp.where(hot, -jnp.inf, rest)
    exps = [jnp.exp(v - vals[0]) for v in vals]
    den = exps[0] + exps[1] + exps[2] + exps[3]

    sel = (hots[0] | hots[1] | hots[2] | hots[3]).astype(F32)
    r_i = lax.broadcasted_iota(jnp.int32, (tm, tm), 0)
    c_i = lax.broadcasted_iota(jnp.int32, (tm, tm), 1)
    tri = (c_i < r_i).astype(BF16)
    pos = jnp.dot(tri, sel.astype(BF16), preferred_element_type=F32) + carry_ref[...]
    poss = [jnp.sum(jnp.where(hot, pos, 0.0), axis=-1, keepdims=True) for hot in hots]
    carry = carry_ref[...] + jnp.sum(sel, axis=0, keepdims=True)
    carry_ref[...] = carry

    te_ref[...] = jnp.concatenate(idxs, axis=1)
    tp_ref[...] = jnp.concatenate(poss, axis=1).astype(jnp.int32)
    tw_ref[...] = jnp.concatenate([e / den for e in exps], axis=1)
    cnt_ref[...] = carry.astype(jnp.int32)


def _merge(y_f, att, mo, proj, x2d, w_ao, w_mo, w_o, n2, w_r, b_r):
    t = x2d.shape[0]
    tm = TM_MERGE
    row = lambda i: (i, 0)
    const = lambda i: (0, 0)
    resident = dict(pipeline_mode=pl.Buffered(1))
    return pl.pallas_call(
        _merge_body,
        grid=(t // tm,),
        in_specs=[
            pl.BlockSpec((tm, D_MODEL), row),
            pl.BlockSpec((tm, ATTN_WIDTH), row),
            pl.BlockSpec((tm, MEM_WIDTH), row),
            pl.BlockSpec((tm, D_MODEL), lambda i: (i, 0)),
            pl.BlockSpec((tm, D_MODEL), lambda i: (i, 1)),
            pl.BlockSpec((tm, D_MODEL), lambda i: (i, 2)),
            pl.BlockSpec((tm, D_MODEL), row),
            pl.BlockSpec((ATTN_WIDTH, D_MODEL), const, **resident),
            pl.BlockSpec((MEM_WIDTH, D_MODEL), const, **resident),
            pl.BlockSpec((D_MODEL, D_MODEL), const, **resident),
            pl.BlockSpec((1, D_MODEL), const),
            pl.BlockSpec((D_MODEL, N_EXPERTS), const),
            pl.BlockSpec((1, N_EXPERTS), const),
        ],
        out_specs=[
            pl.BlockSpec((tm, D_MODEL), row),
            pl.BlockSpec((tm * ROW_CHUNKS, LANES), row),
            pl.BlockSpec((tm, TOP_K), row),
            pl.BlockSpec((tm, TOP_K), row),
            pl.BlockSpec((tm, TOP_K), row),
            pl.BlockSpec((1, N_EXPERTS), const),
        ],
        out_shape=[
            jax.ShapeDtypeStruct((t, D_MODEL), F32),
            jax.ShapeDtypeStruct((t * ROW_CHUNKS, LANES), F32),
            jax.ShapeDtypeStruct((t, TOP_K), jnp.int32),
            jax.ShapeDtypeStruct((t, TOP_K), jnp.int32),
            jax.ShapeDtypeStruct((t, TOP_K), F32),
            jax.ShapeDtypeStruct((1, N_EXPERTS), jnp.int32),
        ],
        scratch_shapes=[pltpu.VMEM((1, N_EXPERTS), F32)],
        compiler_params=_cparams(("arbitrary",)),
        name="merge",
    )(y_f, att, mo, proj, proj, proj, x2d, w_ao, w_mo, w_o, n2, w_r, b_r)


def _row_slice(ref, row, n_rows=1):
    start = pl.multiple_of(row * ROW_CHUNKS, ROW_CHUNKS)
    return ref.at[pl.ds(start, n_rows * ROW_CHUNKS), :]


def _dispatch_body(zstart_ref, dest_ref, h_ref, xs_ref, zero_ref, zsem, sem):
    i = pl.program_id(0)
    tm = h_ref.shape[0] // ROW_CHUNKS

    @pl.when(i == 0)
    def _zero_tails():
        zero_ref[...] = jnp.zeros_like(zero_ref)

        def tail_copies(e):
            s = zstart_ref[e]
            return [pltpu.make_async_copy(zero_ref, _row_slice(xs_ref, s + q * ZERO_ROWS, ZERO_ROWS), zsem)
                    for q in range(TM_E // ZERO_ROWS)]

        def start(e, c):
            @pl.when(zstart_ref[e] >= 0)
            def _():
                for cp in tail_copies(e):
                    cp.start()
            return c

        def wait(e, c):
            @pl.when(zstart_ref[e] >= 0)
            def _():
                for cp in tail_copies(e):
                    cp.wait()
            return c

        lax.fori_loop(0, zstart_ref.shape[0], start, 0)
        lax.fori_loop(0, zstart_ref.shape[0], wait, 0)

    def scatter(t, c):
        for k in range(TOP_K):
            d = dest_ref[t * TOP_K + k]
            pltpu.make_async_copy(_row_slice(h_ref, t), _row_slice(xs_ref, d), sem).start()
        return c

    lax.fori_loop(0, tm, scatter, 0)
    for _ in range(TOP_K):
        pltpu.make_async_copy(h_ref, _row_slice(xs_ref, 0, tm), sem).wait()


def _dispatch(zstart, dest_flat, h2s, n_slots):
    t = h2s.shape[0] // ROW_CHUNKS
    tm = TM_DISP
    return pl.pallas_call(
        _dispatch_body,
        grid_spec=pltpu.PrefetchScalarGridSpec(
            num_scalar_prefetch=1,
            grid=(t // tm,),
            in_specs=[
                pl.BlockSpec((tm * TOP_K,), lambda i, z: (i,), memory_space=pltpu.SMEM),
                pl.BlockSpec((tm * ROW_CHUNKS, LANES), lambda i, z: (i, 0)),
            ],
            out_specs=pl.BlockSpec(memory_space=pl.ANY),
            scratch_shapes=[
                pltpu.VMEM((ZERO_ROWS * ROW_CHUNKS, LANES), F32),
                pltpu.SemaphoreType.DMA(()),
                pltpu.SemaphoreType.DMA(()),
            ],
        ),
        out_shape=jax.ShapeDtypeStruct((n_slots * ROW_CHUNKS, LANES), F32),
        compiler_params=_cparams(("arbitrary",)),
        name="dispatch",
    )(zstart, dest_flat, h2s)


def _experts_body(be_ref, nused_ref, xs_ref, wgu_ref, wd_ref, bgu_ref, bd_ref, o_ref, xb_ref, acc_ref):
    i = pl.program_id(0)
    f = pl.program_id(1)
    nf = pl.num_programs(1)
    tm = xb_ref.shape[0]
    half = TF_E // 2

    @pl.when((i >= nused_ref[0]) & (f == 0))
    def _unused_block():
        o_ref[...] = jnp.zeros_like(o_ref)

    @pl.when(i < nused_ref[0])
    def _block():
        @pl.when(f == 0)
        def _load_rows():
            for c in range(ROW_CHUNKS):
                xb_ref[:, c * LANES:(c + 1) * LANES] = xs_ref[pl.ds(c, tm, stride=ROW_CHUNKS), :].astype(BF16)

        gu = jnp.dot(xb_ref[...], wgu_ref[...].astype(BF16), preferred_element_type=F32) + bgu_ref[...]
        gate = jnp.minimum(gu, SWIGLU_LIMIT)
        glu = gate * (1.0 / (1.0 + jnp.exp(-gate * SWIGLU_ALPHA)))
        up1 = jnp.clip(gu, -SWIGLU_LIMIT, SWIGLU_LIMIT) + 1.0
        prod = glu * pltpu.roll(up1, 2 * TF_E - 1, 1)
        even = (lax.broadcasted_iota(jnp.int32, (1, TF_E), 1) & 1) == 0
        act = jnp.where(even, prod[:, :TF_E], pltpu.roll(prod[:, TF_E:], 1, 1)).astype(BF16)
        lo = pltpu.bitcast(wd_ref[0:half, :].astype(BF16).astype(F32), jnp.uint32)
        hi = pltpu.bitcast(wd_ref[half:TF_E, :].astype(BF16).astype(F32), jnp.uint32)
        wd = pltpu.bitcast(lax.shift_right_logical(lo, jnp.uint32(16)) | hi, BF16)
        part = jnp.dot(act, wd, preferred_element_type=F32)

        @pl.when(f == 0)
        def _first():
            acc_ref[...] = part

        @pl.when(f > 0)
        def _rest():
            acc_ref[...] += part

        @pl.when(f == nf - 1)
        def _store():
            y = acc_ref[...] + bd_ref[...]
            for c in range(ROW_CHUNKS):
                o_ref[pl.ds(c, tm, stride=ROW_CHUNKS), :] = y[:, c * LANES:(c + 1) * LANES]


def _experts(block_e, nused, xs, w_gu, w_dn, b_gu, b_dn, layer, n_slots):
    nblk = n_slots // TM_E
    nf = D_FF // TF_E

    def fchunk(i, f, nu):
        return jnp.where(i < nu[0], f, nf - 1)

    return pl.pallas_call(
        _experts_body,
        grid_spec=pltpu.PrefetchScalarGridSpec(
            num_scalar_prefetch=2,
            grid=(nblk, nf),
            in_specs=[
                pl.BlockSpec((TM_E * ROW_CHUNKS, LANES), lambda i, f, be, nu: (jnp.minimum(i, nu[0] - 1), 0)),
                pl.BlockSpec((None, None, D_MODEL, 2 * TF_E), lambda i, f, be, nu: (layer, be[i], 0, fchunk(i, f, nu))),
                pl.BlockSpec((None, None, TF_E, D_MODEL), lambda i, f, be, nu: (layer, be[i], fchunk(i, f, nu), 0)),
                pl.BlockSpec((None, None, 1, 2 * TF_E), lambda i, f, be, nu: (layer, be[i], 0, fchunk(i, f, nu))),
                pl.BlockSpec((None, None, 1, D_MODEL), lambda i, f, be, nu: (layer, be[i], 0, 0)),
            ],
            out_specs=pl.BlockSpec((TM_E * ROW_CHUNKS, LANES), lambda i, f, be, nu: (i, 0)),
            scratch_shapes=[pltpu.VMEM((TM_E, D_MODEL), BF16), pltpu.VMEM((TM_E, D_MODEL), F32)],
        ),
        out_shape=jax.ShapeDtypeStruct((n_slots * ROW_CHUNKS, LANES), F32),
        compiler_params=_cparams(("arbitrary", "arbitrary")),
        name="experts",
    )(block_e, nused, xs, w_gu, w_dn, b_gu, b_dn)


def _combine_body(dest_ref, ys_ref, w_ref, x1_ref, g_ref, o_ref, buf_ref, sem):
    tm = x1_ref.shape[0]

    def gather(t, c):
        for k in range(TOP_K):
            j = t * TOP_K + k
            pltpu.make_async_copy(_row_slice(ys_ref, dest_ref[j]), _row_slice(buf_ref, j), sem).start()
        return c

    lax.fori_loop(0, tm, gather, 0)
    pltpu.make_async_copy(_row_slice(ys_ref, 0, tm * TOP_K), buf_ref, sem).wait()

    w = w_ref[...]
    cols = []
    for c in range(ROW_CHUNKS):
        acc = None
        for k in range(TOP_K):
            rows = buf_ref[pl.ds(k * ROW_CHUNKS + c, tm, stride=TOP_K * ROW_CHUNKS), :]
            term = rows * w[:, k:k + 1]
            acc = term if acc is None else acc + term
        cols.append(acc)
    x2 = x1_ref[...] + jnp.concatenate(cols, axis=1)
    inv = lax.rsqrt(jnp.mean(x2 * x2, axis=-1, keepdims=True) + RMS_EPS)
    o_ref[...] = x2 * inv * g_ref[...]


def _combine(dest_flat, ys, top_w, x1, g):
    t = x1.shape[0]
    tm = TM_COMB
    return pl.pallas_call(
        _combine_body,
        grid=(t // tm,),
        in_specs=[
            pl.BlockSpec((tm * TOP_K,), lambda i: (i,), memory_space=pltpu.SMEM),
            pl.BlockSpec(memory_space=pl.ANY),
            pl.BlockSpec((tm, TOP_K), lambda i: (i, 0)),
            pl.BlockSpec((tm, D_MODEL), lambda i: (i, 0)),
            pl.BlockSpec((1, D_MODEL), lambda i: (0, 0)),
        ],
        out_specs=pl.BlockSpec((tm, D_MODEL), lambda i: (i, 0)),
        out_shape=jax.ShapeDtypeStruct((t, D_MODEL), F32),
        scratch_shapes=[pltpu.VMEM((tm * TOP_K * ROW_CHUNKS, LANES), F32), pltpu.SemaphoreType.DMA(())],
        compiler_params=_cparams(("arbitrary",)),
        name="combine",
    )(dest_flat, ys, top_w, x1, g)


def _routing_tables(counts, top_e, top_pos, n_slots):
    nblk = n_slots // TM_E
    padded = (counts + TM_E - 1) // TM_E * TM_E
    ends = jnp.cumsum(padded)
    starts = ends - padded
    dest = (starts[top_e] + top_pos).reshape(-1).astype(jnp.int32)
    nused = (ends[-1] // TM_E).astype(jnp.int32).reshape(1)
    block_e = jnp.searchsorted(ends, jnp.arange(nblk, dtype=jnp.int32) * TM_E, side="right")
    block_e = jnp.minimum(block_e, N_EXPERTS - 1).astype(jnp.int32)
    last_e = block_e[jnp.maximum(nused[0] - 1, 0)]
    block_e = jnp.where(jnp.arange(nblk) < nused[0], block_e, last_e)
    unused = (nused[0] + jnp.arange(N_EXPERTS)) * TM_E
    zstart = jnp.concatenate([jnp.where(padded > 0, ends - TM_E, -1),
                              jnp.where(unused < n_slots, unused, -1)]).astype(jnp.int32)
    return dest, nused, block_e, zstart


def kernel(x, mem, rel_bias_table, norm1_g, w_in, attn_sink, w_mem_kv, w_fourier_out, w_attn_out, w_mem_out,
           w_o, norm2_g, w_router, b_router, w_gate_up, b_gate_up, w_down, b_down, final_g):
    batch, seq, _ = x.shape
    t = batch * seq
    depth = norm1_g.shape[0]
    assert depth == 1, "the combine kernel applies the final RMSNorm, so a single layer is supported"
    x2d = x.reshape(t, D_MODEL)
    mem2d = mem.reshape(batch * MEM_LEN, D_MODEL)
    bias = _band_bias(rel_bias_table)
    n_slots = t * TOP_K + N_EXPERTS * TM_E

    for l in range(depth):
        wi = w_in[l]
        wi = jnp.concatenate([wi[:, _OFF_GATE:], wi[:, :_OFF_Q], wi[:, _OFF_MQ:_OFF_GATE],
                              wi[:, _OFF_Q:_OFF_K], wi[:, _OFF_K:_OFF_V], wi[:, _OFF_V:_OFF_MQ]], axis=1).astype(BF16)
        proj = _inproj(x2d, norm1_g[l].reshape(1, D_MODEL), wi)
        y_f = _fourier(proj, w_fourier_out[l].astype(BF16), batch, seq)
        att = _swa(proj, attn_sink[l], bias, batch, seq)
        kv = _memkv(mem2d, w_mem_kv[l].astype(BF16))
        mo = _memattn(proj, kv, batch, seq)
        x1, h2s, top_e, top_pos, top_w, counts = _merge(
            y_f, att, mo, proj, x2d, w_attn_out[l].astype(BF16), w_mem_out[l].astype(BF16), w_o[l].astype(BF16),
            norm2_g[l].reshape(1, D_MODEL), w_router[l], b_router[l].reshape(1, N_EXPERTS))
        dest, nused, block_e, zstart = _routing_tables(counts[0], top_e, top_pos, n_slots)
        xs = _dispatch(zstart, dest, h2s, n_slots)
        ys = _experts(block_e, nused, xs, w_gate_up, w_down,
                      b_gate_up.reshape(depth, N_EXPERTS, 1, 2 * D_FF), b_down.reshape(depth, N_EXPERTS, 1, D_MODEL),
                      l, n_slots)
        x2d = _combine(dest, ys, top_w, x1, final_g.reshape(1, D_MODEL))
    return x2d.reshape(batch, seq, D_MODEL)
```

```python
import functools

import numpy as np
import jax
import jax.numpy as jnp
from jax import lax
from jax.experimental import pallas as pl
from jax.experimental.pallas import tpu as pltpu

D_MODEL = 2048
MEM_LEN = 256
FOURIER_GROUPS = 4
FOURIER_GROUP_DIM = 256
FOURIER_WIDTH = 1024
N_HEADS = 32
N_KV_HEADS = 4
HEAD_DIM = 64
ATTN_WIDTH = 2048
KV_WIDTH = 256
WINDOW = 128
ATT_BLOCK = 128
MEM_HEADS = 4
MEM_HEAD_DIM = 256
MEM_WIDTH = 1024
N_BRANCHES = 3
NUM_BUCKETS = 32
MAX_DISTANCE = 128
N_EXPERTS = 32
TOP_K = 4
D_FF = 2048
SWIGLU_LIMIT = 7.0
SWIGLU_ALPHA = 1.702
RMS_EPS = 1e-5
NEG_INF = -1e30

GATE_WIDTH = N_BRANCHES * D_MODEL
COL_GATE = 0
COL_F = GATE_WIDTH
COL_MQ = COL_F + FOURIER_WIDTH
COL_Q = COL_MQ + MEM_WIDTH
COL_K = COL_Q + ATTN_WIDTH
COL_V = COL_K + KV_WIDTH
IN_WIDTH = COL_V + KV_WIDTH

_OFF_Q = FOURIER_WIDTH
_OFF_K = _OFF_Q + ATTN_WIDTH
_OFF_V = _OFF_K + KV_WIDTH
_OFF_MQ = _OFF_V + KV_WIDTH
_OFF_GATE = _OFF_MQ + MEM_WIDTH

LANES = 128
PACK_ROWS = D_MODEL // (2 * LANES)
VMEM_LIMIT = 56 * 1024 * 1024

TM_IN, TN_IN = 512, 1536
TS_F = 512
TQ_MEM = 512
TM_MERGE = 256
TM_DISP = 256
GROUP_E, SUB_E, TF_E = 1024, 256, 256
NSUB_E = GROUP_E // SUB_E
FULL_E = 1024
TM_COMB = 256

BF16 = jnp.bfloat16
F32 = jnp.float32
U32 = jnp.uint32


def _cparams(sem):
    return pltpu.CompilerParams(dimension_semantics=sem, vmem_limit_bytes=VMEM_LIMIT)


def _bf16_bits(v):
    return pltpu.bitcast(v.astype(BF16).astype(F32), U32)


def _pack_rows(v, out_ref, first_row, rows):
    for r in range(PACK_ROWS):
        lo = _bf16_bits(v[:, (2 * r) * LANES:(2 * r + 1) * LANES])
        hi = _bf16_bits(v[:, (2 * r + 1) * LANES:(2 * r + 2) * LANES])
        out_ref[pl.ds(first_row * PACK_ROWS + r, rows, stride=PACK_ROWS), :] = lax.shift_right_logical(lo, U32(16)) | hi


def _unpack_rows(ref, first_row, rows):
    chunks = []
    for r in range(PACK_ROWS):
        w = ref[pl.ds(first_row * PACK_ROWS + r, rows, stride=PACK_ROWS), :]
        chunks.append(pltpu.bitcast(lax.shift_left(w, U32(16)), F32))
        chunks.append(pltpu.bitcast(w & U32(0xFFFF0000), F32))
    return chunks


def _inproj_body(x_ref, g_ref, w_ref, o_ref, h_ref, *, n_gate_tiles):
    j = pl.program_id(1)

    @pl.when(j == 0)
    def _norm():
        x = x_ref[...]
        inv = lax.rsqrt(jnp.mean(x * x, axis=-1, keepdims=True) + RMS_EPS)
        h_ref[...] = (x * inv * g_ref[...]).astype(BF16)

    acc = jnp.dot(h_ref[...], w_ref[...], preferred_element_type=F32)

    @pl.when(j < n_gate_tiles)
    def _gate():
        o_ref[...] = (1.0 / (1.0 + jnp.exp(-acc))).astype(o_ref.dtype)

    @pl.when(j >= n_gate_tiles)
    def _plain():
        o_ref[...] = acc.astype(o_ref.dtype)


def _inproj(x2d, g, w):
    t = x2d.shape[0]
    return pl.pallas_call(
        functools.partial(_inproj_body, n_gate_tiles=GATE_WIDTH // TN_IN),
        grid=(t // TM_IN, IN_WIDTH // TN_IN),
        in_specs=[
            pl.BlockSpec((TM_IN, D_MODEL), lambda i, j: (i, 0)),
            pl.BlockSpec((1, D_MODEL), lambda i, j: (0, 0)),
            pl.BlockSpec((D_MODEL, TN_IN), lambda i, j: (0, j)),
        ],
        out_specs=pl.BlockSpec((TM_IN, TN_IN), lambda i, j: (i, j)),
        out_shape=jax.ShapeDtypeStruct((t, IN_WIDTH), BF16),
        scratch_shapes=[pltpu.VMEM((TM_IN, D_MODEL), BF16)],
        compiler_params=_cparams(("parallel", "arbitrary")),
        name="inproj",
    )(x2d, g, w)


def _fourier_body(f_ref, cc_ref, sc_ref, cs_ref, wfo_ref, o_ref, z_ref, *, seq):
    st = pl.program_id(1)

    @pl.when(st == 0)
    def _channel_dft():
        for g in range(FOURIER_GROUPS):
            cols = slice(g * FOURIER_GROUP_DIM, (g + 1) * FOURIER_GROUP_DIM)
            xg = f_ref[:, cols]
            z_ref[0:seq, cols] = jnp.dot(xg, cc_ref[...], preferred_element_type=F32).astype(BF16)
            z_ref[seq:2 * seq, cols] = jnp.dot(xg, sc_ref[...], preferred_element_type=F32).astype(BF16)

    y = jnp.dot(cs_ref[...], z_ref[...], preferred_element_type=F32)
    o_ref[...] = jnp.dot(y.astype(BF16), wfo_ref[...], preferred_element_type=F32).astype(o_ref.dtype)


def _dft_tables(seq):
    c = FOURIER_GROUP_DIM
    kc = (np.arange(c)[:, None] * np.arange(c)[None, :]) % c
    ac = 2.0 * np.pi * kc / c
    ks = (np.arange(seq)[:, None] * np.arange(seq)[None, :]) % seq
    a_s = 2.0 * np.pi * ks / seq
    cc = (np.cos(ac) / np.sqrt(c)).astype(np.float32)
    sc = (np.sin(ac) / np.sqrt(c)).astype(np.float32)
    cs = np.concatenate([np.cos(a_s), -np.sin(a_s)], axis=1) / np.sqrt(seq)
    return cc, sc, cs.astype(np.float32)


def _fourier(proj, w_fo, batch, seq):
    cc, sc, cs = _dft_tables(seq)
    cc = jnp.asarray(cc).astype(BF16)
    sc = jnp.asarray(sc).astype(BF16)
    cs = jnp.asarray(cs).astype(BF16)
    t = batch * seq
    return pl.pallas_call(
        functools.partial(_fourier_body, seq=seq),
        grid=(batch, seq // TS_F),
        in_specs=[
            pl.BlockSpec((seq, FOURIER_WIDTH), lambda b, s: (b, COL_F // FOURIER_WIDTH)),
            pl.BlockSpec((FOURIER_GROUP_DIM, FOURIER_GROUP_DIM), lambda b, s: (0, 0)),
            pl.BlockSpec((FOURIER_GROUP_DIM, FOURIER_GROUP_DIM), lambda b, s: (0, 0)),
            pl.BlockSpec((TS_F, 2 * seq), lambda b, s: (s, 0)),
            pl.BlockSpec((FOURIER_WIDTH, D_MODEL), lambda b, s: (0, 0)),
        ],
        out_specs=pl.BlockSpec((TS_F, D_MODEL), lambda b, s: (b * (seq // TS_F) + s, 0)),
        out_shape=jax.ShapeDtypeStruct((t, D_MODEL), BF16),
        scratch_shapes=[pltpu.VMEM((2 * seq, FOURIER_WIDTH), BF16)],
        compiler_params=_cparams(("parallel", "arbitrary")),
        name="fourier",
    )(proj, cc, sc, cs, w_fo)


def _swa_body(sink_ref, q_ref, k_ref, v_ref, bias_ref, o_ref):
    n = pl.program_id(1)
    nb = pl.num_programs(1)
    prev = pl.multiple_of(jnp.maximum(n - 1, 0) * ATT_BLOCK, ATT_BLOCK)
    own = pl.multiple_of(n * ATT_BLOCK, ATT_BLOCK)
    nxt = pl.multiple_of(jnp.minimum(n + 1, nb - 1) * ATT_BLOCK, ATT_BLOCK)

    def band(ref):
        return jnp.concatenate(
            [ref[pl.ds(prev, ATT_BLOCK), :], ref[pl.ds(own, ATT_BLOCK), :], ref[pl.ds(nxt, ATT_BLOCK), :]], axis=0)

    kb = band(k_ref)
    vb = band(v_ref)
    col = lax.broadcasted_iota(jnp.int32, (1, 3 * ATT_BLOCK), 1)
    valid = ((col >= ATT_BLOCK) | (n > 0)) & ((col < 2 * ATT_BLOCK) | (n < nb - 1))

    grp = N_HEADS // N_KV_HEADS
    outs = []
    for g in range(N_KV_HEADS):
        kg = kb[:, g * HEAD_DIM:(g + 1) * HEAD_DIM]
        vg = vb[:, g * HEAD_DIM:(g + 1) * HEAD_DIM]
        for j in range(grp):
            h = g * grp + j
            qh = q_ref[:, h * HEAD_DIM:(h + 1) * HEAD_DIM]
            s = lax.dot_general(qh, kg, (((1,), (1,)), ((), ())), preferred_element_type=F32)
            s = s * np.float32(HEAD_DIM ** -0.5)
            s = jnp.where(valid, s + bias_ref[h], NEG_INF)
            sink = sink_ref[h]
            m = jnp.maximum(jnp.max(s, axis=-1, keepdims=True), sink)
            p = jnp.exp(s - m)
            den = jnp.sum(p, axis=-1, keepdims=True) + jnp.exp(sink - m)
            o = jnp.dot(p.astype(BF16), vg, preferred_element_type=F32)
            outs.append(o / den)
    o_ref[...] = jnp.concatenate(outs, axis=1).astype(o_ref.dtype)


def _t5_bucket(rel):
    nb = NUM_BUCKETS // 2
    max_exact = nb // 2
    ret = (rel > 0).astype(jnp.int32) * nb
    n = jnp.abs(rel)
    nf = jnp.maximum(n, 1).astype(jnp.float32)
    large = max_exact + (jnp.log(nf / max_exact) / np.float32(np.log(MAX_DISTANCE / max_exact))
                         * (nb - max_exact)).astype(jnp.int32)
    large = jnp.minimum(large, nb - 1)
    return ret + jnp.where(n < max_exact, n, large)


def _band_bias(rel_table):
    q_off = jnp.arange(ATT_BLOCK)[:, None]
    k_off = jnp.arange(3 * ATT_BLOCK)[None, :]
    rel = k_off - ATT_BLOCK - q_off
    bias = rel_table.astype(F32)[_t5_bucket(rel)]
    bias = jnp.where((jnp.abs(rel) <= WINDOW)[..., None], bias, NEG_INF)
    return jnp.transpose(bias, (2, 0, 1))


def _swa(proj, sink, bias, batch, seq):
    nb = seq // ATT_BLOCK
    t = batch * seq
    return pl.pallas_call(
        _swa_body,
        grid_spec=pltpu.PrefetchScalarGridSpec(
            num_scalar_prefetch=0,
            grid=(batch, nb),
            in_specs=[
                pl.BlockSpec(memory_space=pltpu.SMEM),
                pl.BlockSpec((ATT_BLOCK, ATTN_WIDTH), lambda b, n: (b * nb + n, COL_Q // ATTN_WIDTH)),
                pl.BlockSpec((seq, KV_WIDTH), lambda b, n: (b, COL_K // KV_WIDTH)),
                pl.BlockSpec((seq, KV_WIDTH), lambda b, n: (b, COL_V // KV_WIDTH)),
                pl.BlockSpec((N_HEADS, ATT_BLOCK, 3 * ATT_BLOCK), lambda b, n: (0, 0, 0)),
            ],
            out_specs=pl.BlockSpec((ATT_BLOCK, ATTN_WIDTH), lambda b, n: (b * nb + n, 0)),
        ),
        out_shape=jax.ShapeDtypeStruct((t, ATTN_WIDTH), BF16),
        compiler_params=_cparams(("parallel", "arbitrary")),
        name="swa",
    )(sink, proj, proj, proj, bias)


def _memkv_body(m_ref, w_ref, o_ref):
    o_ref[...] = jnp.dot(m_ref[...].astype(BF16), w_ref[...], preferred_element_type=F32).astype(o_ref.dtype)


def _memkv(mem2d, w):
    rows = mem2d.shape[0]
    tm, tn = 512, 1024
    return pl.pallas_call(
        _memkv_body,
        grid=(rows // tm, (2 * MEM_WIDTH) // tn),
        in_specs=[
            pl.BlockSpec((tm, D_MODEL), lambda i, j: (i, 0)),
            pl.BlockSpec((D_MODEL, tn), lambda i, j: (0, j)),
        ],
        out_specs=pl.BlockSpec((tm, tn), lambda i, j: (i, j)),
        out_shape=jax.ShapeDtypeStruct((rows, 2 * MEM_WIDTH), BF16),
        compiler_params=_cparams(("parallel", "arbitrary")),
        name="memkv",
    )(mem2d, w)


def _memattn_body(q_ref, k_ref, v_ref, o_ref):
    outs = []
    for h in range(MEM_HEADS):
        cols = slice(h * MEM_HEAD_DIM, (h + 1) * MEM_HEAD_DIM)
        s = lax.dot_general(q_ref[:, cols], k_ref[:, cols], (((1,), (1,)), ((), ())), preferred_element_type=F32)
        s = s * np.float32(MEM_HEAD_DIM ** -0.5)
        m = jnp.max(s, axis=-1, keepdims=True)
        p = jnp.exp(s - m)
        den = jnp.sum(p, axis=-1, keepdims=True)
        o = jnp.dot(p.astype(BF16), v_ref[:, cols], preferred_element_type=F32)
        outs.append(o / den)
    o_ref[...] = jnp.concatenate(outs, axis=1).astype(o_ref.dtype)


def _memattn(proj, kv, batch, seq):
    t = batch * seq
    nq = seq // TQ_MEM
    return pl.pallas_call(
        _memattn_body,
        grid=(batch, nq),
        in_specs=[
            pl.BlockSpec((TQ_MEM, MEM_WIDTH), lambda b, i: (b * nq + i, COL_MQ // MEM_WIDTH)),
            pl.BlockSpec((MEM_LEN, MEM_WIDTH), lambda b, i: (b, 0)),
            pl.BlockSpec((MEM_LEN, MEM_WIDTH), lambda b, i: (b, 1)),
        ],
        out_specs=pl.BlockSpec((TQ_MEM, MEM_WIDTH), lambda b, i: (b * nq + i, 0)),
        out_shape=jax.ShapeDtypeStruct((t, MEM_WIDTH), BF16),
        compiler_params=_cparams(("parallel", "arbitrary")),
        name="memattn",
    )(proj, kv, kv)


def _merge_body(yf_ref, att_ref, mo_ref, g0_ref, g1_ref, g2_ref, x_ref, wao_ref, wmo_ref, wo_ref,
                n2_ref, wr_ref, br_ref,
                x1_ref, h2s_ref, te_ref, tp_ref, tw_ref, cnt_ref, carry_ref):
    i = pl.program_id(0)
    tm = x_ref.shape[0]

    @pl.when(i == 0)
    def _init():
        carry_ref[...] = jnp.zeros_like(carry_ref)

    y_a = jnp.dot(att_ref[...], wao_ref[...], preferred_element_type=F32)
    y_m = jnp.dot(mo_ref[...], wmo_ref[...], preferred_element_type=F32)
    mixed = (g0_ref[...].astype(F32) * yf_ref[...].astype(F32)
             + g1_ref[...].astype(F32) * y_a + g2_ref[...].astype(F32) * y_m)
    x1 = x_ref[...] + jnp.dot(mixed.astype(BF16), wo_ref[...], preferred_element_type=F32)
    x1_ref[...] = x1

    inv = lax.rsqrt(jnp.mean(x1 * x1, axis=-1, keepdims=True) + RMS_EPS)
    h2 = x1 * inv * n2_ref[...]
    _pack_rows(h2, h2s_ref, 0, tm)

    logits = jnp.dot(h2, wr_ref[...], preferred_element_type=F32, precision=lax.Precision.HIGHEST) + br_ref[...]
    lane = lax.broadcasted_iota(jnp.int32, (tm, N_EXPERTS), 1)
    vals, idxs, hots = [], [], []
    rest = logits
    for _ in range(TOP_K):
        mk = jnp.max(rest, axis=-1, keepdims=True)
        ik = jnp.min(jnp.where(rest == mk, lane, N_EXPERTS), axis=-1, keepdims=True)
        hot = lane == ik
        vals.append(mk)
        idxs.append(ik)
        hots.append(hot)
        rest = jnp.where(hot, -jnp.inf, rest)
    exps = [jnp.exp(v - vals[0]) for v in vals]
    den = exps[0] + exps[1] + exps[2] + exps[3]

    sel = (hots[0] | hots[1] | hots[2] | hots[3]).astype(F32)
    r_i = lax.broadcasted_iota(jnp.int32, (tm, tm), 0)
    c_i = lax.broadcasted_iota(jnp.int32, (tm, tm), 1)
    tri = (c_i < r_i).astype(BF16)
    pos = jnp.dot(tri, sel.astype(BF16), preferred_element_type=F32) + carry_ref[...]
    poss = [jnp.sum(jnp.where(hot, pos, 0.0), axis=-1, keepdims=True) for hot in hots]
    carry = carry_ref[...] + jnp.sum(sel, axis=0, keepdims=True)
    carry_ref[...] = carry

    te_ref[...] = jnp.concatenate(idxs, axis=1)
    tp_ref[...] = jnp.concatenate(poss, axis=1).astype(jnp.int32)
    tw_ref[...] = jnp.concatenate([e / den for e in exps], axis=1)
    cnt_ref[...] = carry.astype(jnp.int32)


def _merge(y_f, att, mo, proj, x2d, w_ao, w_mo, w_o, n2, w_r, b_r):
    t = x2d.shape[0]
    tm = TM_MERGE
    row = lambda i: (i, 0)
    const = lambda i: (0, 0)
    resident = dict(pipeline_mode=pl.Buffered(1))
    return pl.pallas_call(
        _merge_body,
        grid=(t // tm,),
        in_specs=[
            pl.BlockSpec((tm, D_MODEL), row),
            pl.BlockSpec((tm, ATTN_WIDTH), row),
            pl.BlockSpec((tm, MEM_WIDTH), row),
            pl.BlockSpec((tm, D_MODEL), lambda i: (i, 0)),
            pl.BlockSpec((tm, D_MODEL), lambda i: (i, 1)),
            pl.BlockSpec((tm, D_MODEL), lambda i: (i, 2)),
            pl.BlockSpec((tm, D_MODEL), row),
            pl.BlockSpec((ATTN_WIDTH, D_MODEL), const, **resident),
            pl.BlockSpec((MEM_WIDTH, D_MODEL), const, **resident),
            pl.BlockSpec((D_MODEL, D_MODEL), const, **resident),
            pl.BlockSpec((1, D_MODEL), const),
            pl.BlockSpec((D_MODEL, N_EXPERTS), const),
            pl.BlockSpec((1, N_EXPERTS), const),
        ],
        out_specs=[
            pl.BlockSpec((tm, D_MODEL), row),
            pl.BlockSpec((tm * PACK_ROWS, LANES), row),
            pl.BlockSpec((tm, TOP_K), row),
            pl.BlockSpec((tm, TOP_K), row),
            pl.BlockSpec((tm, TOP_K), row),
            pl.BlockSpec((1, N_EXPERTS), const),
        ],
        out_shape=[
            jax.ShapeDtypeStruct((t, D_MODEL), F32),
            jax.ShapeDtypeStruct((t * PACK_ROWS, LANES), jnp.uint32),
            jax.ShapeDtypeStruct((t, TOP_K), jnp.int32),
            jax.ShapeDtypeStruct((t, TOP_K), jnp.int32),
            jax.ShapeDtypeStruct((t, TOP_K), F32),
            jax.ShapeDtypeStruct((1, N_EXPERTS), jnp.int32),
        ],
        scratch_shapes=[pltpu.VMEM((1, N_EXPERTS), F32)],
        compiler_params=_cparams(("arbitrary",)),
        name="merge",
    )(y_f, att, mo, proj, proj, proj, x2d, w_ao, w_mo, w_o, n2, w_r, b_r)


def _row_slice(ref, row, n_rows=1):
    start = pl.multiple_of(row * PACK_ROWS, PACK_ROWS)
    return ref.at[pl.ds(start, n_rows * PACK_ROWS), :]


def _dispatch_body(zstart_ref, dest_ref, h_ref, xs_ref, zero_ref, zsem, sem):
    i = pl.program_id(0)
    tm = h_ref.shape[0] // PACK_ROWS

    @pl.when(i == 0)
    def _zero_tails():
        zero_ref[...] = jnp.zeros_like(zero_ref)

        def tail_copy(e):
            return pltpu.make_async_copy(zero_ref, _row_slice(xs_ref, zstart_ref[e], SUB_E), zsem)

        def start(e, c):
            @pl.when(zstart_ref[e] >= 0)
            def _():
                tail_copy(e).start()
            return c

        def wait(e, c):
            @pl.when(zstart_ref[e] >= 0)
            def _():
                tail_copy(e).wait()
            return c

        lax.fori_loop(0, zstart_ref.shape[0], start, 0)
        lax.fori_loop(0, zstart_ref.shape[0], wait, 0)

    def scatter(t, c):
        for k in range(TOP_K):
            d = dest_ref[t * TOP_K + k]
            pltpu.make_async_copy(_row_slice(h_ref, t), _row_slice(xs_ref, d), sem).start()
        return c

    lax.fori_loop(0, tm, scatter, 0)
    for _ in range(TOP_K):
        pltpu.make_async_copy(h_ref, _row_slice(xs_ref, 0, tm), sem).wait()


def _dispatch(zstart, dest_flat, h2s, n_slots):
    t = h2s.shape[0] // PACK_ROWS
    tm = TM_DISP
    return pl.pallas_call(
        _dispatch_body,
        grid_spec=pltpu.PrefetchScalarGridSpec(
            num_scalar_prefetch=1,
            grid=(t // tm,),
            in_specs=[
                pl.BlockSpec((tm * TOP_K,), lambda i, z: (i,), memory_space=pltpu.SMEM),
                pl.BlockSpec((tm * PACK_ROWS, LANES), lambda i, z: (i, 0)),
            ],
            out_specs=pl.BlockSpec(memory_space=pl.ANY),
            scratch_shapes=[
                pltpu.VMEM((SUB_E * PACK_ROWS, LANES), U32),
                pltpu.SemaphoreType.DMA(()),
                pltpu.SemaphoreType.DMA(()),
            ],
        ),
        out_shape=jax.ShapeDtypeStruct((n_slots * PACK_ROWS, LANES), U32),
        compiler_params=_cparams(("arbitrary",)),
        name="dispatch",
    )(zstart, dest_flat, h2s)


def _experts_body(ge_ref, gn_ref, nu_ref, xs_ref, wgu_ref, wd_ref, bgu_ref, bd_ref, o_ref,
                  xb_ref, acc_ref, wgu_bf_ref, wd_bf_ref):
    g = pl.program_id(0)
    f = pl.program_id(1)
    nf = pl.num_programs(1)
    nsub = gn_ref[g]
    half = TF_E // 2

    @pl.when((nsub == 0) & (f == 0))
    def _unused_group():
        o_ref[...] = jnp.zeros_like(o_ref)

    @pl.when(nsub > 0)
    def _group():
        @pl.when(f == 0)
        def _load_rows():
            for r in range(PACK_ROWS):
                w = xs_ref[pl.ds(r, GROUP_E, stride=PACK_ROWS), :]
                lo = pltpu.bitcast(lax.shift_left(w, U32(16)), F32)
                hi = pltpu.bitcast(w & U32(0xFFFF0000), F32)
                xb_ref[:, (2 * r) * LANES:(2 * r + 1) * LANES] = lo.astype(BF16)
                xb_ref[:, (2 * r + 1) * LANES:(2 * r + 2) * LANES] = hi.astype(BF16)
            acc_ref[...] = jnp.zeros_like(acc_ref)

        wgu_bf_ref[...] = wgu_ref[...].astype(BF16)
        lo = _bf16_bits(wd_ref[0:half, :])
        hi = _bf16_bits(wd_ref[half:TF_E, :])
        wd_bf_ref[...] = pltpu.bitcast(lax.shift_right_logical(lo, U32(16)) | hi, BF16)
        even = (lax.broadcasted_iota(jnp.int32, (1, TF_E), 1) & 1) == 0

        def sub_block(s, n_rows=SUB_E):
            rows = pl.ds(pl.multiple_of(s * n_rows, n_rows), n_rows)
            gu = jnp.dot(xb_ref[rows, :], wgu_bf_ref[...], preferred_element_type=F32) + bgu_ref[...]
            gate = jnp.minimum(gu, SWIGLU_LIMIT)
            glu = gate * (1.0 / (1.0 + jnp.exp(-gate * SWIGLU_ALPHA)))
            up1 = jnp.clip(gu, -SWIGLU_LIMIT, SWIGLU_LIMIT) + 1.0
            prod = glu * pltpu.roll(up1, 2 * TF_E - 1, 1)
            act = jnp.where(even, prod[:, :TF_E], pltpu.roll(prod[:, TF_E:], 1, 1)).astype(BF16)
            acc_ref[rows, :] += jnp.dot(act, wd_bf_ref[...], preferred_element_type=F32)

        @pl.when(nsub == NSUB_E)
        def _full_group():
            for s in range(GROUP_E // FULL_E):
                sub_block(s, FULL_E)

        @pl.when(nsub < NSUB_E)
        def _partial_group():
            def body(s, c):
                sub_block(s)
                return c
            lax.fori_loop(0, nsub, body, 0)

        @pl.when(f == nf - 1)
        def _store():
            for s in range(NSUB_E):
                y = acc_ref[s * SUB_E:(s + 1) * SUB_E, :] + bd_ref[...]
                _pack_rows(y, o_ref, s * SUB_E, SUB_E)


def _experts(group_e, group_nsub, n_used, xs, w_gu, w_dn, b_gu, b_dn, layer, n_slots):
    ngrp = n_slots // GROUP_E
    nf = D_FF // TF_E

    def fchunk(g, f, nu):
        return jnp.where(g < nu[0], f, nf - 1)

    return pl.pallas_call(
        _experts_body,
        grid_spec=pltpu.PrefetchScalarGridSpec(
            num_scalar_prefetch=3,
            grid=(ngrp, nf),
            in_specs=[
                pl.BlockSpec((GROUP_E * PACK_ROWS, LANES), lambda g, f, ge, gn, nu: (jnp.minimum(g, nu[0] - 1), 0)),
                pl.BlockSpec((None, None, D_MODEL, 2 * TF_E),
                             lambda g, f, ge, gn, nu: (layer, ge[g], 0, fchunk(g, f, nu))),
                pl.BlockSpec((None, None, TF_E, D_MODEL),
                             lambda g, f, ge, gn, nu: (layer, ge[g], fchunk(g, f, nu), 0)),
                pl.BlockSpec((None, None, 1, 2 * TF_E),
                             lambda g, f, ge, gn, nu: (layer, ge[g], 0, fchunk(g, f, nu))),
                pl.BlockSpec((None, None, 1, D_MODEL), lambda g, f, ge, gn, nu: (layer, ge[g], 0, 0)),
            ],
            out_specs=pl.BlockSpec((GROUP_E * PACK_ROWS, LANES), lambda g, f, ge, gn, nu: (g, 0)),
            scratch_shapes=[
                pltpu.VMEM((GROUP_E, D_MODEL), BF16),
                pltpu.VMEM((GROUP_E, D_MODEL), F32),
                pltpu.VMEM((D_MODEL, 2 * TF_E), BF16),
                pltpu.VMEM((TF_E, D_MODEL), BF16),
            ],
        ),
        out_shape=jax.ShapeDtypeStruct((n_slots * PACK_ROWS, LANES), U32),
        compiler_params=_cparams(("arbitrary", "arbitrary")),
        name="experts",
    )(group_e, group_nsub, n_used, xs, w_gu, w_dn, b_gu, b_dn)


def _combine_body(dest_ref, dnext_ref, ys_ref, w_ref, x1_ref, g_ref, o_ref, buf_ref, sems):
    i = pl.program_id(0)
    n = pl.num_programs(0)
    tm = x1_ref.shape[0]
    buf_rows = tm * TOP_K
    slot = i % 2

    def issue(d_ref, b):
        def body(t, c):
            for k in range(TOP_K):
                src = _row_slice(ys_ref, d_ref[t * TOP_K + k])
                pltpu.make_async_copy(src, _row_slice(buf_ref, b * buf_rows + k * tm + t), sems.at[b]).start()
            return c
        lax.fori_loop(0, tm, body, 0)

    @pl.when(i == 0)
    def _first_tile():
        issue(dest_ref, 0)

    @pl.when(i + 1 < n)
    def _next_tile():
        issue(dnext_ref, 1 - slot)

    base = slot * buf_rows
    pltpu.make_async_copy(_row_slice(ys_ref, 0, buf_rows), _row_slice(buf_ref, base, buf_rows), sems.at[slot]).wait()

    w = w_ref[...]
    wk = [jnp.broadcast_to(w[:, k:k + 1], (tm, LANES)) for k in range(TOP_K)]
    ssq = jnp.zeros((tm, 1), F32)
    for r in range(PACK_ROWS):
        acc_lo = acc_hi = None
        for k in range(TOP_K):
            word = buf_ref[pl.ds((base + k * tm) * PACK_ROWS + r, tm, stride=PACK_ROWS), :]
            lo = pltpu.bitcast(lax.shift_left(word, U32(16)), F32) * wk[k]
            hi = pltpu.bitcast(word & U32(0xFFFF0000), F32) * wk[k]
            acc_lo = lo if acc_lo is None else acc_lo + lo
            acc_hi = hi if acc_hi is None else acc_hi + hi
        for c, moe in ((2 * r, acc_lo), (2 * r + 1, acc_hi)):
            cols = slice(c * LANES, (c + 1) * LANES)
            v = x1_ref[:, cols] + moe
            o_ref[:, cols] = v
            ssq = ssq + jnp.sum(v * v, axis=-1, keepdims=True)
    inv = lax.rsqrt(ssq * np.float32(1.0 / D_MODEL) + RMS_EPS)
    o_ref[...] = o_ref[...] * inv * g_ref[...]


def _combine(dest_flat, ys, top_w, x1, g):
    t = x1.shape[0]
    tm = TM_COMB
    nt = t // tm
    return pl.pallas_call(
        _combine_body,
        grid=(nt,),
        in_specs=[
            pl.BlockSpec((tm * TOP_K,), lambda i: (i,), memory_space=pltpu.SMEM),
            pl.BlockSpec((tm * TOP_K,), lambda i: (jnp.minimum(i + 1, nt - 1),), memory_space=pltpu.SMEM),
            pl.BlockSpec(memory_space=pl.ANY),
            pl.BlockSpec((tm, TOP_K), lambda i: (i, 0)),
            pl.BlockSpec((tm, D_MODEL), lambda i: (i, 0)),
            pl.BlockSpec((1, D_MODEL), lambda i: (0, 0)),
        ],
        out_specs=pl.BlockSpec((tm, D_MODEL), lambda i: (i, 0)),
        out_shape=jax.ShapeDtypeStruct((t, D_MODEL), F32),
        scratch_shapes=[pltpu.VMEM((2 * tm * TOP_K * PACK_ROWS, LANES), U32), pltpu.SemaphoreType.DMA((2,))],
        compiler_params=_cparams(("arbitrary",)),
        name="combine",
    )(dest_flat, dest_flat, ys, top_w, x1, g)


def _routing_tables(counts, top_e, top_pos, n_slots):
    i32 = jnp.int32
    ngrp = n_slots // GROUP_E
    experts = jnp.arange(N_EXPERTS, dtype=i32)
    padded = (counts + GROUP_E - 1) // GROUP_E * GROUP_E
    ends = jnp.cumsum(padded)
    starts = ends - padded
    total = ends[-1]
    dest = jnp.sum(jnp.where(top_e[..., None] == experts, starts, 0), axis=-1) + top_pos
    gstart = jnp.arange(ngrp, dtype=i32) * GROUP_E
    used = gstart < total
    of_group = gstart[:, None] >= ends[None, :]
    last_e = jnp.max(jnp.where(padded > 0, experts, 0))
    group_e = jnp.where(used, jnp.minimum(jnp.sum(of_group, axis=1), N_EXPERTS - 1), last_e)
    valid_end = jnp.sum(jnp.where(group_e[:, None] == experts, starts + counts, 0), axis=-1)
    rows = jnp.clip(valid_end - gstart, 0, GROUP_E)
    group_nsub = jnp.where(used, (rows + SUB_E - 1) // SUB_E, 0)
    n_used = (total // GROUP_E).reshape(1)
    sub = jnp.arange(NSUB_E, dtype=i32) * SUB_E
    tail = (starts + counts // SUB_E * SUB_E)[:, None] + sub
    tail = jnp.where(tail < ends[:, None], tail, -1)
    spare = (n_used[0] + experts)[:, None] * GROUP_E + sub
    spare = jnp.where(spare < n_slots, spare, -1)
    zstart = jnp.concatenate([tail.reshape(-1), spare.reshape(-1)])
    return (dest.reshape(-1).astype(i32), group_e.astype(i32), group_nsub.astype(i32), n_used.astype(i32),
            zstart.astype(i32))


def kernel(x, mem, rel_bias_table, norm1_g, w_in, attn_sink, w_mem_kv, w_fourier_out, w_attn_out, w_mem_out,
           w_o, norm2_g, w_router, b_router, w_gate_up, b_gate_up, w_down, b_down, final_g):
    batch, seq, _ = x.shape
    t = batch * seq
    depth = norm1_g.shape[0]
    assert depth == 1, "the combine kernel applies the final RMSNorm, so a single layer is supported"
    x2d = x.reshape(t, D_MODEL)
    mem2d = mem.reshape(batch * MEM_LEN, D_MODEL)
    bias = _band_bias(rel_bias_table)
    n_slots = t * TOP_K + N_EXPERTS * GROUP_E

    for l in range(depth):
        wi = w_in[l]
        wi = jnp.concatenate([wi[:, _OFF_GATE:], wi[:, :_OFF_Q], wi[:, _OFF_MQ:_OFF_GATE],
                              wi[:, _OFF_Q:_OFF_K], wi[:, _OFF_K:_OFF_V], wi[:, _OFF_V:_OFF_MQ]], axis=1).astype(BF16)
        proj = _inproj(x2d, norm1_g[l].reshape(1, D_MODEL), wi)
        y_f = _fourier(proj, w_fourier_out[l].astype(BF16), batch, seq)
        att = _swa(proj, attn_sink[l], bias, batch, seq)
        kv = _memkv(mem2d, w_mem_kv[l].astype(BF16))
        mo = _memattn(proj, kv, batch, seq)
        x1, h2s, top_e, top_pos, top_w, counts = _merge(
            y_f, att, mo, proj, x2d, w_attn_out[l].astype(BF16), w_mem_out[l].astype(BF16), w_o[l].astype(BF16),
            norm2_g[l].reshape(1, D_MODEL), w_router[l], b_router[l].reshape(1, N_EXPERTS))
        dest, group_e, group_nsub, n_used, zstart = _routing_tables(counts[0], top_e, top_pos, n_slots)
        xs = _dispatch(zstart, dest, h2s, n_slots)
        ys = _experts(group_e, group_nsub, n_used, xs, w_gate_up, w_down,
                      b_gate_up.reshape(depth, N_EXPERTS, 1, 2 * D_FF), b_down.reshape(depth, N_EXPERTS, 1, D_MODEL),
                      l, n_slots)
        x2d = _combine(dest, ys, top_w, x1, final_g.reshape(1, D_MODEL))
    return x2d.reshape(batch, seq, D_MODEL)
```

```python
import functools

import numpy as np
import jax
import jax.numpy as jnp
from jax import lax
from jax.experimental import pallas as pl
from jax.experimental.pallas import tpu as pltpu

D_MODEL = 2048
MEM_LEN = 256
FOURIER_GROUPS = 4
FOURIER_GROUP_DIM = 256
FOURIER_WIDTH = 1024
N_HEADS = 32
N_KV_HEADS = 4
HEAD_DIM = 64
ATTN_WIDTH = 2048
KV_WIDTH = 256
WINDOW = 128
ATT_BLOCK = 128
MEM_HEADS = 4
MEM_HEAD_DIM = 256
MEM_WIDTH = 1024
N_BRANCHES = 3
NUM_BUCKETS = 32
MAX_DISTANCE = 128
N_EXPERTS = 32
TOP_K = 4
D_FF = 2048
SWIGLU_LIMIT = 7.0
SWIGLU_ALPHA = 1.702
RMS_EPS = 1e-5
NEG_INF = -1e30

GATE_WIDTH = N_BRANCHES * D_MODEL
COL_GATE = 0
COL_F = GATE_WIDTH
COL_MQ = COL_F + FOURIER_WIDTH
COL_Q = COL_MQ + MEM_WIDTH
COL_K = COL_Q + ATTN_WIDTH
COL_V = COL_K + KV_WIDTH
IN_WIDTH = COL_V + KV_WIDTH

_OFF_Q = FOURIER_WIDTH
_OFF_K = _OFF_Q + ATTN_WIDTH
_OFF_V = _OFF_K + KV_WIDTH
_OFF_MQ = _OFF_V + KV_WIDTH
_OFF_GATE = _OFF_MQ + MEM_WIDTH

LANES = 128
PACK_ROWS = D_MODEL // (2 * LANES)
VMEM_LIMIT = 56 * 1024 * 1024

TM_IN, TN_IN = 512, 1536
TS_F = 512
TQ_MEM = 512
TM_MERGE = 256
TM_DISP = 256
GROUP_E, SUB_E, TF_E = 1024, 256, 256
NSUB_E = GROUP_E // SUB_E
FULL_E = 1024
TM_COMB = 256

BF16 = jnp.bfloat16
F32 = jnp.float32
U32 = jnp.uint32


def _cparams(sem):
    return pltpu.CompilerParams(dimension_semantics=sem, vmem_limit_bytes=VMEM_LIMIT)


def _bf16_bits(v):
    return pltpu.bitcast(v.astype(BF16).astype(F32), U32)


def _pack_rows(v, out_ref, first_row, rows):
    for r in range(PACK_ROWS):
        lo = _bf16_bits(v[:, (2 * r) * LANES:(2 * r + 1) * LANES])
        hi = _bf16_bits(v[:, (2 * r + 1) * LANES:(2 * r + 2) * LANES])
        out_ref[pl.ds(first_row * PACK_ROWS + r, rows, stride=PACK_ROWS), :] = lax.shift_right_logical(lo, U32(16)) | hi


def _unpack_rows(ref, first_row, rows):
    chunks = []
    for r in range(PACK_ROWS):
        w = ref[pl.ds(first_row * PACK_ROWS + r, rows, stride=PACK_ROWS), :]
        chunks.append(pltpu.bitcast(lax.shift_left(w, U32(16)), F32))
        chunks.append(pltpu.bitcast(w & U32(0xFFFF0000), F32))
    return chunks


def _inproj_body(x_ref, g_ref, w_ref, o_ref, h_ref, *, n_gate_tiles):
    j = pl.program_id(1)

    @pl.when(j == 0)
    def _norm():
        x = x_ref[...]
        inv = lax.rsqrt(jnp.mean(x * x, axis=-1, keepdims=True) + RMS_EPS)
        h_ref[...] = (x * inv * g_ref[...]).astype(BF16)

    acc = jnp.dot(h_ref[...], w_ref[...], preferred_element_type=F32)

    @pl.when(j < n_gate_tiles)
    def _gate():
        o_ref[...] = (1.0 / (1.0 + jnp.exp(-acc))).astype(o_ref.dtype)

    @pl.when(j >= n_gate_tiles)
    def _plain():
        o_ref[...] = acc.astype(o_ref.dtype)


def _inproj(x2d, g, w):
    t = x2d.shape[0]
    return pl.pallas_call(
        functools.partial(_inproj_body, n_gate_tiles=GATE_WIDTH // TN_IN),
        grid=(t // TM_IN, IN_WIDTH // TN_IN),
        in_specs=[
            pl.BlockSpec((TM_IN, D_MODEL), lambda i, j: (i, 0)),
            pl.BlockSpec((1, D_MODEL), lambda i, j: (0, 0)),
            pl.BlockSpec((D_MODEL, TN_IN), lambda i, j: (0, j)),
        ],
        out_specs=pl.BlockSpec((TM_IN, TN_IN), lambda i, j: (i, j)),
        out_shape=jax.ShapeDtypeStruct((t, IN_WIDTH), BF16),
        scratch_shapes=[pltpu.VMEM((TM_IN, D_MODEL), BF16)],
        compiler_params=_cparams(("parallel", "arbitrary")),
        name="inproj",
    )(x2d, g, w)


def _fourier_body(f_ref, cc_ref, sc_ref, cs_ref, wfo_ref, o_ref, z_ref, *, seq):
    st = pl.program_id(1)

    @pl.when(st == 0)
    def _channel_dft():
        for g in range(FOURIER_GROUPS):
            cols = slice(g * FOURIER_GROUP_DIM, (g + 1) * FOURIER_GROUP_DIM)
            xg = f_ref[:, cols]
            z_ref[0:seq, cols] = jnp.dot(xg, cc_ref[...], preferred_element_type=F32).astype(BF16)
            z_ref[seq:2 * seq, cols] = jnp.dot(xg, sc_ref[...], preferred_element_type=F32).astype(BF16)

    y = jnp.dot(cs_ref[...], z_ref[...], preferred_element_type=F32)
    o_ref[...] = jnp.dot(y.astype(BF16), wfo_ref[...], preferred_element_type=F32).astype(o_ref.dtype)


def _dft_tables(seq):
    c = FOURIER_GROUP_DIM
    kc = (np.arange(c)[:, None] * np.arange(c)[None, :]) % c
    ac = 2.0 * np.pi * kc / c
    ks = (np.arange(seq)[:, None] * np.arange(seq)[None, :]) % seq
    a_s = 2.0 * np.pi * ks / seq
    cc = (np.cos(ac) / np.sqrt(c)).astype(np.float32)
    sc = (np.sin(ac) / np.sqrt(c)).astype(np.float32)
    cs = np.concatenate([np.cos(a_s), -np.sin(a_s)], axis=1) / np.sqrt(seq)
    return cc, sc, cs.astype(np.float32)


def _fourier(proj, w_fo, batch, seq):
    cc, sc, cs = _dft_tables(seq)
    cc = jnp.asarray(cc).astype(BF16)
    sc = jnp.asarray(sc).astype(BF16)
    cs = jnp.asarray(cs).astype(BF16)
    t = batch * seq
    return pl.pallas_call(
        functools.partial(_fourier_body, seq=seq),
        grid=(batch, seq // TS_F),
        in_specs=[
            pl.BlockSpec((seq, FOURIER_WIDTH), lambda b, s: (b, COL_F // FOURIER_WIDTH)),
            pl.BlockSpec((FOURIER_GROUP_DIM, FOURIER_GROUP_DIM), lambda b, s: (0, 0)),
            pl.BlockSpec((FOURIER_GROUP_DIM, FOURIER_GROUP_DIM), lambda b, s: (0, 0)),
            pl.BlockSpec((TS_F, 2 * seq), lambda b, s: (s, 0)),
            pl.BlockSpec((FOURIER_WIDTH, D_MODEL), lambda b, s: (0, 0)),
        ],
        out_specs=pl.BlockSpec((TS_F, D_MODEL), lambda b, s: (b * (seq // TS_F) + s, 0)),
        out_shape=jax.ShapeDtypeStruct((t, D_MODEL), BF16),
        scratch_shapes=[pltpu.VMEM((2 * seq, FOURIER_WIDTH), BF16)],
        compiler_params=_cparams(("parallel", "arbitrary")),
        name="fourier",
    )(proj, cc, sc, cs, w_fo)


def _swa_body(sink_ref, q_ref, k_ref, v_ref, bias_ref, o_ref):
    n = pl.program_id(1)
    nb = pl.num_programs(1)
    prev = pl.multiple_of(jnp.maximum(n - 1, 0) * ATT_BLOCK, ATT_BLOCK)
    own = pl.multiple_of(n * ATT_BLOCK, ATT_BLOCK)
    nxt = pl.multiple_of(jnp.minimum(n + 1, nb - 1) * ATT_BLOCK, ATT_BLOCK)

    def band(ref):
        return jnp.concatenate(
            [ref[pl.ds(prev, ATT_BLOCK), :], ref[pl.ds(own, ATT_BLOCK), :], ref[pl.ds(nxt, ATT_BLOCK), :]], axis=0)

    kb = band(k_ref).astype(F32)
    vb = band(v_ref).astype(F32)
    key = lax.broadcasted_iota(jnp.int32, (3 * ATT_BLOCK, 1), 0)
    valid = ((key >= ATT_BLOCK) | (n > 0)) & ((key < 2 * ATT_BLOCK) | (n < nb - 1))
    low = lax.broadcasted_iota(jnp.int32, (1, LANES), 1) < HEAD_DIM
    low_row = lax.broadcasted_iota(jnp.int32, (LANES, 1), 0) < HEAD_DIM

    def halves(x, c, kv_is_low):
        a = x[:, c * LANES:(c + 1) * LANES]
        r = pltpu.roll(a, HEAD_DIM, 1)
        in_low, in_high = (a, r) if kv_is_low else (r, a)
        return jnp.where(low, in_low, 0.0), jnp.where(low, 0.0, in_high)

    pairs = N_HEADS // N_KV_HEADS // 2
    for g in range(N_KV_HEADS):
        k_par = [h.astype(BF16) for h in halves(kb, g // 2, g % 2 == 0)]
        vt_par = [h.T.astype(BF16) for h in halves(vb, g // 2, g % 2 == 0)]
        base = g * pairs * LANES
        qp = jnp.concatenate([q_ref[:, base + p * LANES:base + (p + 1) * LANES] for p in range(pairs)], axis=0)
        qp = qp * jnp.asarray(HEAD_DIM ** -0.5, BF16)
        probs, inv_den = [], []
        for e in range(2):
            s = lax.dot_general(k_par[e], qp, (((1,), (1,)), ((), ())), preferred_element_type=F32)
            s = jnp.where(valid, s + bias_ref[g, e], NEG_INF)
            sink = jnp.concatenate(
                [jnp.full((1, ATT_BLOCK), sink_ref[(g * pairs + p) * 2 + e], F32) for p in range(pairs)], axis=1)
            m = jnp.maximum(jnp.max(s, axis=0, keepdims=True), sink)
            p_un = jnp.exp(s - m)
            den = jnp.sum(p_un, axis=0, keepdims=True) + jnp.exp(sink - m)
            probs.append(p_un.astype(BF16))
            inv_den.append(1.0 / den)
        o_t = (jnp.dot(vt_par[0], probs[0], preferred_element_type=F32)
               + jnp.dot(vt_par[1], probs[1], preferred_element_type=F32))
        o = (o_t * jnp.where(low_row, inv_den[0], inv_den[1])).T
        for p in range(pairs):
            o_ref[:, base + p * LANES:base + (p + 1) * LANES] = o[p * ATT_BLOCK:(p + 1) * ATT_BLOCK].astype(o_ref.dtype)


def _t5_bucket(rel):
    nb = NUM_BUCKETS // 2
    max_exact = nb // 2
    ret = (rel > 0).astype(jnp.int32) * nb
    n = jnp.abs(rel)
    nf = jnp.maximum(n, 1).astype(jnp.float32)
    large = max_exact + (jnp.log(nf / max_exact) / np.float32(np.log(MAX_DISTANCE / max_exact))
                         * (nb - max_exact)).astype(jnp.int32)
    large = jnp.minimum(large, nb - 1)
    return ret + jnp.where(n < max_exact, n, large)


def _band_bias(rel_table):
    q_off = jnp.arange(ATT_BLOCK)[:, None]
    k_off = jnp.arange(3 * ATT_BLOCK)[None, :]
    rel = k_off - ATT_BLOCK - q_off
    hot = (_t5_bucket(rel)[..., None] == jnp.arange(NUM_BUCKETS)).astype(F32)
    bias = jnp.einsum("qkb,bh->hqk", hot, rel_table.astype(F32), precision=lax.Precision.HIGHEST)
    bias = jnp.where(jnp.abs(rel) <= WINDOW, bias, NEG_INF)
    pairs = N_HEADS // N_KV_HEADS // 2
    bias = bias.reshape(N_KV_HEADS, pairs, 2, ATT_BLOCK, 3 * ATT_BLOCK).transpose(0, 2, 4, 1, 3)
    return bias.reshape(N_KV_HEADS, 2, 3 * ATT_BLOCK, pairs * ATT_BLOCK)


def _swa(proj, sink, bias, batch, seq):
    nb = seq // ATT_BLOCK
    t = batch * seq
    return pl.pallas_call(
        _swa_body,
        grid_spec=pltpu.PrefetchScalarGridSpec(
            num_scalar_prefetch=0,
            grid=(batch, nb),
            in_specs=[
                pl.BlockSpec(memory_space=pltpu.SMEM),
                pl.BlockSpec((ATT_BLOCK, ATTN_WIDTH), lambda b, n: (b * nb + n, COL_Q // ATTN_WIDTH)),
                pl.BlockSpec((seq, KV_WIDTH), lambda b, n: (b, COL_K // KV_WIDTH)),
                pl.BlockSpec((seq, KV_WIDTH), lambda b, n: (b, COL_V // KV_WIDTH)),
                pl.BlockSpec(bias.shape, lambda b, n: (0, 0, 0, 0)),
            ],
            out_specs=pl.BlockSpec((ATT_BLOCK, ATTN_WIDTH), lambda b, n: (b * nb + n, 0)),
        ),
        out_shape=jax.ShapeDtypeStruct((t, ATTN_WIDTH), BF16),
        compiler_params=_cparams(("parallel", "arbitrary")),
        name="swa",
    )(sink, proj, proj, proj, bias)


def _memkv_body(m_ref, w_ref, o_ref):
    o_ref[...] = jnp.dot(m_ref[...].astype(BF16), w_ref[...], preferred_element_type=F32).astype(o_ref.dtype)


def _memkv(mem2d, w):
    rows = mem2d.shape[0]
    tm, tn = 512, 1024
    return pl.pallas_call(
        _memkv_body,
        grid=(rows // tm, (2 * MEM_WIDTH) // tn),
        in_specs=[
            pl.BlockSpec((tm, D_MODEL), lambda i, j: (i, 0)),
            pl.BlockSpec((D_MODEL, tn), lambda i, j: (0, j)),
        ],
        out_specs=pl.BlockSpec((tm, tn), lambda i, j: (i, j)),
        out_shape=jax.ShapeDtypeStruct((rows, 2 * MEM_WIDTH), BF16),
        compiler_params=_cparams(("parallel", "arbitrary")),
        name="memkv",
    )(mem2d, w)


def _memattn_body(q_ref, k_ref, v_ref, o_ref):
    outs = []
    for h in range(MEM_HEADS):
        cols = slice(h * MEM_HEAD_DIM, (h + 1) * MEM_HEAD_DIM)
        s = lax.dot_general(q_ref[:, cols], k_ref[:, cols], (((1,), (1,)), ((), ())), preferred_element_type=F32)
        s = s * np.float32(MEM_HEAD_DIM ** -0.5)
        m = jnp.max(s, axis=-1, keepdims=True)
        p = jnp.exp(s - m)
        den = jnp.sum(p, axis=-1, keepdims=True)
        o = jnp.dot(p.astype(BF16), v_ref[:, cols], preferred_element_type=F32)
        outs.append(o / den)
    o_ref[...] = jnp.concatenate(outs, axis=1).astype(o_ref.dtype)


def _memattn(proj, kv, batch, seq):
    t = batch * seq
    nq = seq // TQ_MEM
    return pl.pallas_call(
        _memattn_body,
        grid=(batch, nq),
        in_specs=[
            pl.BlockSpec((TQ_MEM, MEM_WIDTH), lambda b, i: (b * nq + i, COL_MQ // MEM_WIDTH)),
            pl.BlockSpec((MEM_LEN, MEM_WIDTH), lambda b, i: (b, 0)),
            pl.BlockSpec((MEM_LEN, MEM_WIDTH), lambda b, i: (b, 1)),
        ],
        out_specs=pl.BlockSpec((TQ_MEM, MEM_WIDTH), lambda b, i: (b * nq + i, 0)),
        out_shape=jax.ShapeDtypeStruct((t, MEM_WIDTH), BF16),
        compiler_params=_cparams(("parallel", "arbitrary")),
        name="memattn",
    )(proj, kv, kv)


def _merge_body(yf_ref, att_ref, mo_ref, g0_ref, g1_ref, g2_ref, x_ref, wao_ref, wmo_ref, wo_ref,
                n2_ref, wr_ref, br_ref,
                x1_ref, h2s_ref, te_ref, tp_ref, tw_ref, cnt_ref, carry_ref):
    i = pl.program_id(0)
    tm = x_ref.shape[0]

    @pl.when(i == 0)
    def _init():
        carry_ref[...] = jnp.zeros_like(carry_ref)

    y_a = jnp.dot(att_ref[...], wao_ref[...], preferred_element_type=F32)
    y_m = jnp.dot(mo_ref[...], wmo_ref[...], preferred_element_type=F32)
    mixed = (g0_ref[...].astype(F32) * yf_ref[...].astype(F32)
             + g1_ref[...].astype(F32) * y_a + g2_ref[...].astype(F32) * y_m)
    x1 = x_ref[...] + jnp.dot(mixed.astype(BF16), wo_ref[...], preferred_element_type=F32)
    x1_ref[...] = x1

    inv = lax.rsqrt(jnp.mean(x1 * x1, axis=-1, keepdims=True) + RMS_EPS)
    h2 = x1 * inv * n2_ref[...]
    _pack_rows(h2, h2s_ref, 0, tm)

    logits = jnp.dot(h2, wr_ref[...], preferred_element_type=F32, precision=lax.Precision.HIGHEST) + br_ref[...]
    lane = lax.broadcasted_iota(jnp.int32, (tm, N_EXPERTS), 1)
    vals, idxs, hots = [], [], []
    rest = logits
    for _ in range(TOP_K):
        mk = jnp.max(rest, axis=-1, keepdims=True)
        ik = jnp.min(jnp.where(rest == mk, lane, N_EXPERTS), axis=-1, keepdims=True)
        hot = lane == ik
        vals.append(mk)
        idxs.append(ik)
        hots.append(hot)
        rest = jnp.where(hot, -jnp.inf, rest)
    exps = [jnp.exp(v - vals[0]) for v in vals]
    den = exps[0] + exps[1] + exps[2] + exps[3]

    sel = (hots[0] | hots[1] | hots[2] | hots[3]).astype(F32)
    r_i = lax.broadcasted_iota(jnp.int32, (tm, tm), 0)
    c_i = lax.broadcasted_iota(jnp.int32, (tm, tm), 1)
    tri = (c_i < r_i).astype(BF16)
    pos = jnp.dot(tri, sel.astype(BF16), preferred_element_type=F32) + carry_ref[...]
    poss = [jnp.sum(jnp.where(hot, pos, 0.0), axis=-1, keepdims=True) for hot in hots]
    carry = carry_ref[...] + jnp.sum(sel, axis=0, keepdims=True)
    carry_ref[...] = carry

    te_ref[...] = jnp.concatenate(idxs, axis=1)
    tp_ref[...] = jnp.concatenate(poss, axis=1).astype(jnp.int32)
    tw_ref[...] = jnp.concatenate([e / den for e in exps], axis=1)
    cnt_ref[...] = carry.astype(jnp.int32)


def _merge(y_f, att, mo, proj, x2d, w_ao, w_mo, w_o, n2, w_r, b_r):
    t = x2d.shape[0]
    tm = TM_MERGE
    row = lambda i: (i, 0)
    const = lambda i: (0, 0)
    resident = dict(pipeline_mode=pl.Buffered(1))
    return pl.pallas_call(
        _merge_body,
        grid=(t // tm,),
        in_specs=[
            pl.BlockSpec((tm, D_MODEL), row),
            pl.BlockSpec((tm, ATTN_WIDTH), row),
            pl.BlockSpec((tm, MEM_WIDTH), row),
            pl.BlockSpec((tm, D_MODEL), lambda i: (i, 0)),
            pl.BlockSpec((tm, D_MODEL), lambda i: (i, 1)),
            pl.BlockSpec((tm, D_MODEL), lambda i: (i, 2)),
            pl.BlockSpec((tm, D_MODEL), row),
            pl.BlockSpec((ATTN_WIDTH, D_MODEL), const, **resident),
            pl.BlockSpec((MEM_WIDTH, D_MODEL), const, **resident),
            pl.BlockSpec((D_MODEL, D_MODEL), const, **resident),
            pl.BlockSpec((1, D_MODEL), const),
            pl.BlockSpec((D_MODEL, N_EXPERTS), const),
            pl.BlockSpec((1, N_EXPERTS), const),
        ],
        out_specs=[
            pl.BlockSpec((tm, D_MODEL), row),
            pl.BlockSpec((tm * PACK_ROWS, LANES), row),
            pl.BlockSpec((tm, TOP_K), row),
            pl.BlockSpec((tm, TOP_K), row),
            pl.BlockSpec((tm, TOP_K), row),
            pl.BlockSpec((1, N_EXPERTS), const),
        ],
        out_shape=[
            jax.ShapeDtypeStruct((t, D_MODEL), F32),
            jax.ShapeDtypeStruct((t * PACK_ROWS, LANES), jnp.uint32),
            jax.ShapeDtypeStruct((t, TOP_K), jnp.int32),
            jax.ShapeDtypeStruct((t, TOP_K), jnp.int32),
            jax.ShapeDtypeStruct((t, TOP_K), F32),
            jax.ShapeDtypeStruct((1, N_EXPERTS), jnp.int32),
        ],
        scratch_shapes=[pltpu.VMEM((1, N_EXPERTS), F32)],
        compiler_params=_cparams(("arbitrary",)),
        name="merge",
    )(y_f, att, mo, proj, proj, proj, x2d, w_ao, w_mo, w_o, n2, w_r, b_r)


def _row_slice(ref, row, n_rows=1):
    start = pl.multiple_of(row * PACK_ROWS, PACK_ROWS)
    return ref.at[pl.ds(start, n_rows * PACK_ROWS), :]


def _dispatch_body(zstart_ref, dest_ref, h_ref, xs_ref, zero_ref, zsem, sem):
    i = pl.program_id(0)
    tm = h_ref.shape[0] // PACK_ROWS

    @pl.when(i == 0)
    def _zero_tails():
        zero_ref[...] = jnp.zeros_like(zero_ref)

        def tail_copy(e):
            return pltpu.make_async_copy(zero_ref, _row_slice(xs_ref, zstart_ref[e], SUB_E), zsem)

        def start(e, c):
            @pl.when(zstart_ref[e] >= 0)
            def _():
                tail_copy(e).start()
            return c

        def wait(e, c):
            @pl.when(zstart_ref[e] >= 0)
            def _():
                tail_copy(e).wait()
            return c

        lax.fori_loop(0, zstart_ref.shape[0], start, 0)
        lax.fori_loop(0, zstart_ref.shape[0], wait, 0)

    def scatter(t, c):
        for k in range(TOP_K):
            d = dest_ref[t * TOP_K + k]
            pltpu.make_async_copy(_row_slice(h_ref, t), _row_slice(xs_ref, d), sem).start()
        return c

    lax.fori_loop(0, tm, scatter, 0)
    for _ in range(TOP_K):
        pltpu.make_async_copy(h_ref, _row_slice(xs_ref, 0, tm), sem).wait()


def _dispatch(zstart, dest_flat, h2s, n_slots):
    t = h2s.shape[0] // PACK_ROWS
    tm = TM_DISP
    return pl.pallas_call(
        _dispatch_body,
        grid_spec=pltpu.PrefetchScalarGridSpec(
            num_scalar_prefetch=1,
            grid=(t // tm,),
            in_specs=[
                pl.BlockSpec((tm * TOP_K,), lambda i, z: (i,), memory_space=pltpu.SMEM),
                pl.BlockSpec((tm * PACK_ROWS, LANES), lambda i, z: (i, 0)),
            ],
            out_specs=pl.BlockSpec(memory_space=pl.ANY),
            scratch_shapes=[
                pltpu.VMEM((SUB_E * PACK_ROWS, LANES), U32),
                pltpu.SemaphoreType.DMA(()),
                pltpu.SemaphoreType.DMA(()),
            ],
        ),
        out_shape=jax.ShapeDtypeStruct((n_slots * PACK_ROWS, LANES), U32),
        compiler_params=_cparams(("arbitrary",)),
        name="dispatch",
    )(zstart, dest_flat, h2s)


def _experts_body(ge_ref, gn_ref, nu_ref, xs_ref, wgu_ref, wd_ref, bgu_ref, bd_ref, o_ref,
                  xb_ref, acc_ref, wgu_bf_ref, wd_bf_ref):
    g = pl.program_id(0)
    f = pl.program_id(1)
    nf = pl.num_programs(1)
    nsub = gn_ref[g]
    half = TF_E // 2

    @pl.when((nsub == 0) & (f == 0))
    def _unused_group():
        o_ref[...] = jnp.zeros_like(o_ref)

    @pl.when(nsub > 0)
    def _group():
        @pl.when(f == 0)
        def _load_rows():
            for r in range(PACK_ROWS):
                w = xs_ref[pl.ds(r, GROUP_E, stride=PACK_ROWS), :]
                lo = pltpu.bitcast(lax.shift_left(w, U32(16)), F32)
                hi = pltpu.bitcast(w & U32(0xFFFF0000), F32)
                xb_ref[:, (2 * r) * LANES:(2 * r + 1) * LANES] = lo.astype(BF16)
                xb_ref[:, (2 * r + 1) * LANES:(2 * r + 2) * LANES] = hi.astype(BF16)
            acc_ref[...] = jnp.zeros_like(acc_ref)

        wgu_bf_ref[...] = wgu_ref[...].astype(BF16)
        lo = _bf16_bits(wd_ref[0:half, :])
        hi = _bf16_bits(wd_ref[half:TF_E, :])
        wd_bf_ref[...] = pltpu.bitcast(lax.shift_right_logical(lo, U32(16)) | hi, BF16)
        even = (lax.broadcasted_iota(jnp.int32, (1, TF_E), 1) & 1) == 0

        def sub_block(s, n_rows=SUB_E):
            rows = pl.ds(pl.multiple_of(s * n_rows, n_rows), n_rows)
            gu = jnp.dot(xb_ref[rows, :], wgu_bf_ref[...], preferred_element_type=F32) + bgu_ref[...]
            gate = jnp.minimum(gu, SWIGLU_LIMIT)
            glu = gate * (1.0 / (1.0 + jnp.exp(-gate * SWIGLU_ALPHA)))
            up1 = jnp.clip(gu, -SWIGLU_LIMIT, SWIGLU_LIMIT) + 1.0
            prod = glu * pltpu.roll(up1, 2 * TF_E - 1, 1)
            act = jnp.where(even, prod[:, :TF_E], pltpu.roll(prod[:, TF_E:], 1, 1)).astype(BF16)
            acc_ref[rows, :] += jnp.dot(act, wd_bf_ref[...], preferred_element_type=F32)

        @pl.when(nsub == NSUB_E)
        def _full_group():
            for s in range(GROUP_E // FULL_E):
                sub_block(s, FULL_E)

        @pl.when(nsub < NSUB_E)
        def _partial_group():
            def body(s, c):
                sub_block(s)
                return c
            lax.fori_loop(0, nsub, body, 0)

        @pl.when(f == nf - 1)
        def _store():
            for s in range(NSUB_E):
                y = acc_ref[s * SUB_E:(s + 1) * SUB_E, :] + bd_ref[...]
                _pack_rows(y, o_ref, s * SUB_E, SUB_E)


def _experts(group_e, group_nsub, n_used, xs, w_gu, w_dn, b_gu, b_dn, layer, n_slots):
    ngrp = n_slots // GROUP_E
    nf = D_FF // TF_E

    def fchunk(g, f, nu):
        return jnp.where(g < nu[0], f, nf - 1)

    return pl.pallas_call(
        _experts_body,
        grid_spec=pltpu.PrefetchScalarGridSpec(
            num_scalar_prefetch=3,
            grid=(ngrp, nf),
            in_specs=[
                pl.BlockSpec((GROUP_E * PACK_ROWS, LANES), lambda g, f, ge, gn, nu: (jnp.minimum(g, nu[0] - 1), 0)),
                pl.BlockSpec((None, None, D_MODEL, 2 * TF_E),
                             lambda g, f, ge, gn, nu: (layer, ge[g], 0, fchunk(g, f, nu))),
                pl.BlockSpec((None, None, TF_E, D_MODEL),
                             lambda g, f, ge, gn, nu: (layer, ge[g], fchunk(g, f, nu), 0)),
                pl.BlockSpec((None, None, 1, 2 * TF_E),
                             lambda g, f, ge, gn, nu: (layer, ge[g], 0, fchunk(g, f, nu))),
                pl.BlockSpec((None, None, 1, D_MODEL), lambda g, f, ge, gn, nu: (layer, ge[g], 0, 0)),
            ],
            out_specs=pl.BlockSpec((GROUP_E * PACK_ROWS, LANES), lambda g, f, ge, gn, nu: (g, 0)),
            scratch_shapes=[
                pltpu.VMEM((GROUP_E, D_MODEL), BF16),
                pltpu.VMEM((GROUP_E, D_MODEL), F32),
                pltpu.VMEM((D_MODEL, 2 * TF_E), BF16),
                pltpu.VMEM((TF_E, D_MODEL), BF16),
            ],
        ),
        out_shape=jax.ShapeDtypeStruct((n_slots * PACK_ROWS, LANES), U32),
        compiler_params=_cparams(("arbitrary", "arbitrary")),
        name="experts",
    )(group_e, group_nsub, n_used, xs, w_gu, w_dn, b_gu, b_dn)


def _combine_body(dest_ref, dnext_ref, ys_ref, w_ref, x1_ref, g_ref, o_ref, buf_ref, sems):
    i = pl.program_id(0)
    n = pl.num_programs(0)
    tm = x1_ref.shape[0]
    buf_rows = tm * TOP_K
    slot = i % 2

    def issue(d_ref, b):
        def body(t, c):
            for k in range(TOP_K):
                src = _row_slice(ys_ref, d_ref[t * TOP_K + k])
                pltpu.make_async_copy(src, _row_slice(buf_ref, b * buf_rows + k * tm + t), sems.at[b]).start()
            return c
        lax.fori_loop(0, tm, body, 0)

    @pl.when(i == 0)
    def _first_tile():
        issue(dest_ref, 0)

    @pl.when(i + 1 < n)
    def _next_tile():
        issue(dnext_ref, 1 - slot)

    base = slot * buf_rows
    pltpu.make_async_copy(_row_slice(ys_ref, 0, buf_rows), _row_slice(buf_ref, base, buf_rows), sems.at[slot]).wait()

    w = w_ref[...]
    wk = [jnp.broadcast_to(w[:, k:k + 1], (tm, LANES)) for k in range(TOP_K)]
    ssq = jnp.zeros((tm, 1), F32)
    for r in range(PACK_ROWS):
        acc_lo = acc_hi = None
        for k in range(TOP_K):
            word = buf_ref[pl.ds((base + k * tm) * PACK_ROWS + r, tm, stride=PACK_ROWS), :]
            lo = pltpu.bitcast(lax.shift_left(word, U32(16)), F32) * wk[k]
            hi = pltpu.bitcast(word & U32(0xFFFF0000), F32) * wk[k]
            acc_lo = lo if acc_lo is None else acc_lo + lo
            acc_hi = hi if acc_hi is None else acc_hi + hi
        for c, moe in ((2 * r, acc_lo), (2 * r + 1, acc_hi)):
            cols = slice(c * LANES, (c + 1) * LANES)
            v = x1_ref[:, cols] + moe
            o_ref[:, cols] = v
            ssq = ssq + jnp.sum(v * v, axis=-1, keepdims=True)
    inv = lax.rsqrt(ssq * np.float32(1.0 / D_MODEL) + RMS_EPS)
    o_ref[...] = o_ref[...] * inv * g_ref[...]


def _combine(dest_flat, ys, top_w, x1, g):
    t = x1.shape[0]
    tm = TM_COMB
    nt = t // tm
    return pl.pallas_call(
        _combine_body,
        grid=(nt,),
        in_specs=[
            pl.BlockSpec((tm * TOP_K,), lambda i: (i,), memory_space=pltpu.SMEM),
            pl.BlockSpec((tm * TOP_K,), lambda i: (jnp.minimum(i + 1, nt - 1),), memory_space=pltpu.SMEM),
            pl.BlockSpec(memory_space=pl.ANY),
            pl.BlockSpec((tm, TOP_K), lambda i: (i, 0)),
            pl.BlockSpec((tm, D_MODEL), lambda i: (i, 0)),
            pl.BlockSpec((1, D_MODEL), lambda i: (0, 0)),
        ],
        out_specs=pl.BlockSpec((tm, D_MODEL), lambda i: (i, 0)),
        out_shape=jax.ShapeDtypeStruct((t, D_MODEL), F32),
        scratch_shapes=[pltpu.VMEM((2 * tm * TOP_K * PACK_ROWS, LANES), U32), pltpu.SemaphoreType.DMA((2,))],
        compiler_params=_cparams(("arbitrary",)),
        name="combine",
    )(dest_flat, dest_flat, ys, top_w, x1, g)


def _routing_tables(counts, top_e, top_pos, n_slots):
    i32 = jnp.int32
    ngrp = n_slots // GROUP_E
    experts = jnp.arange(N_EXPERTS, dtype=i32)
    padded = (counts + GROUP_E - 1) // GROUP_E * GROUP_E
    ends = jnp.cumsum(padded)
    starts = ends - padded
    total = ends[-1]
    dest = jnp.sum(jnp.where(top_e[..., None] == experts, starts, 0), axis=-1) + top_pos
    gstart = jnp.arange(ngrp, dtype=i32) * GROUP_E
    used = gstart < total
    of_group = gstart[:, None] >= ends[None, :]
    last_e = jnp.max(jnp.where(padded > 0, experts, 0))
    group_e = jnp.where(used, jnp.minimum(jnp.sum(of_group, axis=1), N_EXPERTS - 1), last_e)
    valid_end = jnp.sum(jnp.where(group_e[:, None] == experts, starts + counts, 0), axis=-1)
    rows = jnp.clip(valid_end - gstart, 0, GROUP_E)
    group_nsub = jnp.where(used, (rows + SUB_E - 1) // SUB_E, 0)
    n_used = (total // GROUP_E).reshape(1)
    sub = jnp.arange(NSUB_E, dtype=i32) * SUB_E
    tail = (starts + counts // SUB_E * SUB_E)[:, None] + sub
    tail = jnp.where(tail < ends[:, None], tail, -1)
    spare = (n_used[0] + experts)[:, None] * GROUP_E + sub
    spare = jnp.where(spare < n_slots, spare, -1)
    zstart = jnp.concatenate([tail.reshape(-1), spare.reshape(-1)])
    return (dest.reshape(-1).astype(i32), group_e.astype(i32), group_nsub.astype(i32), n_used.astype(i32),
            zstart.astype(i32))


def kernel(x, mem, rel_bias_table, norm1_g, w_in, attn_sink, w_mem_kv, w_fourier_out, w_attn_out, w_mem_out,
           w_o, norm2_g, w_router, b_router, w_gate_up, b_gate_up, w_down, b_down, final_g):
    batch, seq, _ = x.shape
    t = batch * seq
    depth = norm1_g.shape[0]
    assert depth == 1, "the combine kernel applies the final RMSNorm, so a single layer is supported"
    x2d = x.reshape(t, D_MODEL)
    mem2d = mem.reshape(batch * MEM_LEN, D_MODEL)
    bias = _band_bias(rel_bias_table)
    n_slots = t * TOP_K + N_EXPERTS * GROUP_E

    for l in range(depth):
        wi = w_in[l]
        wi = jnp.concatenate([wi[:, _OFF_GATE:], wi[:, :_OFF_Q], wi[:, _OFF_MQ:_OFF_GATE],
                              wi[:, _OFF_Q:_OFF_K], wi[:, _OFF_K:_OFF_V], wi[:, _OFF_V:_OFF_MQ]], axis=1).astype(BF16)
        proj = _inproj(x2d, norm1_g[l].reshape(1, D_MODEL), wi)
        y_f = _fourier(proj, w_fourier_out[l].astype(BF16), batch, seq)
        att = _swa(proj, attn_sink[l], bias, batch, seq)
        kv = _memkv(mem2d, w_mem_kv[l].astype(BF16))
        mo = _memattn(proj, kv, batch, seq)
        x1, h2s, top_e, top_pos, top_w, counts = _merge(
            y_f, att, mo, proj, x2d, w_attn_out[l].astype(BF16), w_mem_out[l].astype(BF16), w_o[l].astype(BF16),
            norm2_g[l].reshape(1, D_MODEL), w_router[l], b_router[l].reshape(1, N_EXPERTS))
        dest, group_e, group_nsub, n_used, zstart = _routing_tables(counts[0], top_e, top_pos, n_slots)
        xs = _dispatch(zstart, dest, h2s, n_slots)
        ys = _experts(group_e, group_nsub, n_used, xs, w_gate_up, w_down,
                      b_gate_up.reshape(depth, N_EXPERTS, 1, 2 * D_FF), b_down.reshape(depth, N_EXPERTS, 1, D_MODEL),
                      l, n_slots)
        x2d = _combine(dest, ys, top_w, x1, final_g.reshape(1, D_MODEL))
    return x2d.reshape(batch, seq, D_MODEL)
```

```python
import functools

import numpy as np
import jax
import jax.numpy as jnp
from jax import lax
from jax.experimental import pallas as pl
from jax.experimental.pallas import tpu as pltpu

D_MODEL = 2048
MEM_LEN = 256
FOURIER_GROUPS = 4
FOURIER_GROUP_DIM = 256
FOURIER_WIDTH = 1024
N_HEADS = 32
N_KV_HEADS = 4
HEAD_DIM = 64
ATTN_WIDTH = 2048
KV_WIDTH = 256
WINDOW = 128
ATT_BLOCK = 128
MEM_HEADS = 4
MEM_HEAD_DIM = 256
MEM_WIDTH = 1024
N_BRANCHES = 3
NUM_BUCKETS = 32
MAX_DISTANCE = 128
N_EXPERTS = 32
TOP_K = 4
D_FF = 2048
SWIGLU_LIMIT = 7.0
SWIGLU_ALPHA = 1.702
RMS_EPS = 1e-5
NEG_INF = -1e30

GATE_WIDTH = N_BRANCHES * D_MODEL
COL_GATE = 0
COL_F = GATE_WIDTH
COL_MQ = COL_F + FOURIER_WIDTH
COL_Q = COL_MQ + MEM_WIDTH
COL_K = COL_Q + ATTN_WIDTH
COL_V = COL_K + KV_WIDTH
IN_WIDTH = COL_V + KV_WIDTH

_OFF_Q = FOURIER_WIDTH
_OFF_K = _OFF_Q + ATTN_WIDTH
_OFF_V = _OFF_K + KV_WIDTH
_OFF_MQ = _OFF_V + KV_WIDTH
_OFF_GATE = _OFF_MQ + MEM_WIDTH

LANES = 128
PACK_ROWS = D_MODEL // (2 * LANES)
VMEM_LIMIT = 56 * 1024 * 1024

TM_IN, TN_IN = 512, 1536
TS_F = 512
TQ_MEM = 512
TM_MERGE = 512
TM_DISP = 256
GROUP_E, SUB_E, TF_E = 1024, 128, 256
NSUB_E = GROUP_E // SUB_E
NF_E = D_FF // TF_E
DOWN_E = 512
TM_COMB = 256

BF16 = jnp.bfloat16
F32 = jnp.float32
U32 = jnp.uint32


def _cparams(sem):
    return pltpu.CompilerParams(dimension_semantics=sem, vmem_limit_bytes=VMEM_LIMIT)


def _bf16_bits(v):
    return pltpu.bitcast(v.astype(BF16).astype(F32), U32)


def _pack_rows(v, out_ref, first_row, rows):
    for r in range(PACK_ROWS):
        lo = _bf16_bits(v[:, (2 * r) * LANES:(2 * r + 1) * LANES])
        hi = _bf16_bits(v[:, (2 * r + 1) * LANES:(2 * r + 2) * LANES])
        out_ref[pl.ds(first_row * PACK_ROWS + r, rows, stride=PACK_ROWS), :] = lax.shift_right_logical(lo, U32(16)) | hi


def _unpack_rows(ref, first_row, rows):
    chunks = []
    for r in range(PACK_ROWS):
        w = ref[pl.ds(first_row * PACK_ROWS + r, rows, stride=PACK_ROWS), :]
        chunks.append(pltpu.bitcast(lax.shift_left(w, U32(16)), F32))
        chunks.append(pltpu.bitcast(w & U32(0xFFFF0000), F32))
    return chunks


def _inproj_body(x_ref, g_ref, w_ref, o_ref, h_ref, *, n_gate_tiles):
    j = pl.program_id(1)

    @pl.when(j == 0)
    def _norm():
        x = x_ref[...]
        inv = lax.rsqrt(jnp.mean(x * x, axis=-1, keepdims=True) + RMS_EPS)
        h_ref[...] = (x * inv * g_ref[...]).astype(BF16)

    acc = jnp.dot(h_ref[...], w_ref[...], preferred_element_type=F32)

    @pl.when(j < n_gate_tiles)
    def _gate():
        o_ref[...] = (1.0 / (1.0 + jnp.exp(-acc))).astype(o_ref.dtype)

    @pl.when(j >= n_gate_tiles)
    def _plain():
        o_ref[...] = acc.astype(o_ref.dtype)


def _inproj(x2d, g, w):
    t = x2d.shape[0]
    return pl.pallas_call(
        functools.partial(_inproj_body, n_gate_tiles=GATE_WIDTH // TN_IN),
        grid=(t // TM_IN, IN_WIDTH // TN_IN),
        in_specs=[
            pl.BlockSpec((TM_IN, D_MODEL), lambda i, j: (i, 0)),
            pl.BlockSpec((1, D_MODEL), lambda i, j: (0, 0)),
            pl.BlockSpec((D_MODEL, TN_IN), lambda i, j: (0, j)),
        ],
        out_specs=pl.BlockSpec((TM_IN, TN_IN), lambda i, j: (i, j)),
        out_shape=jax.ShapeDtypeStruct((t, IN_WIDTH), BF16),
        scratch_shapes=[pltpu.VMEM((TM_IN, D_MODEL), BF16)],
        compiler_params=_cparams(("parallel", "arbitrary")),
        name="inproj",
    )(x2d, g, w)


def _fourier_body(f_ref, cc_ref, sc_ref, cs_ref, wfo_ref, o_ref, z_ref, *, seq):
    st = pl.program_id(1)

    @pl.when(st == 0)
    def _channel_dft():
        for g in range(FOURIER_GROUPS):
            cols = slice(g * FOURIER_GROUP_DIM, (g + 1) * FOURIER_GROUP_DIM)
            xg = f_ref[:, cols]
            z_ref[0:seq, cols] = jnp.dot(xg, cc_ref[...], preferred_element_type=F32).astype(BF16)
            z_ref[seq:2 * seq, cols] = jnp.dot(xg, sc_ref[...], preferred_element_type=F32).astype(BF16)

    y = jnp.dot(cs_ref[...], z_ref[...], preferred_element_type=F32)
    o_ref[...] = jnp.dot(y.astype(BF16), wfo_ref[...], preferred_element_type=F32).astype(o_ref.dtype)


def _dft_tables(seq):
    c = FOURIER_GROUP_DIM
    kc = (np.arange(c)[:, None] * np.arange(c)[None, :]) % c
    ac = 2.0 * np.pi * kc / c
    ks = (np.arange(seq)[:, None] * np.arange(seq)[None, :]) % seq
    a_s = 2.0 * np.pi * ks / seq
    cc = (np.cos(ac) / np.sqrt(c)).astype(np.float32)
    sc = (np.sin(ac) / np.sqrt(c)).astype(np.float32)
    cs = np.concatenate([np.cos(a_s), -np.sin(a_s)], axis=1) / np.sqrt(seq)
    return cc, sc, cs.astype(np.float32)


def _fourier(proj, w_fo, batch, seq):
    cc, sc, cs = _dft_tables(seq)
    cc = jnp.asarray(cc).astype(BF16)
    sc = jnp.asarray(sc).astype(BF16)
    cs = jnp.asarray(cs).astype(BF16)
    t = batch * seq
    return pl.pallas_call(
        functools.partial(_fourier_body, seq=seq),
        grid=(batch, seq // TS_F),
        in_specs=[
            pl.BlockSpec((seq, FOURIER_WIDTH), lambda b, s: (b, COL_F // FOURIER_WIDTH)),
            pl.BlockSpec((FOURIER_GROUP_DIM, FOURIER_GROUP_DIM), lambda b, s: (0, 0)),
            pl.BlockSpec((FOURIER_GROUP_DIM, FOURIER_GROUP_DIM), lambda b, s: (0, 0)),
            pl.BlockSpec((TS_F, 2 * seq), lambda b, s: (s, 0)),
            pl.BlockSpec((FOURIER_WIDTH, D_MODEL), lambda b, s: (0, 0)),
        ],
        out_specs=pl.BlockSpec((TS_F, D_MODEL), lambda b, s: (b * (seq // TS_F) + s, 0)),
        out_shape=jax.ShapeDtypeStruct((t, D_MODEL), BF16),
        scratch_shapes=[pltpu.VMEM((2 * seq, FOURIER_WIDTH), BF16)],
        compiler_params=_cparams(("parallel", "arbitrary")),
        name="fourier",
    )(proj, cc, sc, cs, w_fo)


def _swa_body(sink_ref, q_ref, k_ref, v_ref, bias_ref, o_ref):
    n = pl.program_id(1)
    nb = pl.num_programs(1)
    prev = pl.multiple_of(jnp.maximum(n - 1, 0) * ATT_BLOCK, ATT_BLOCK)
    own = pl.multiple_of(n * ATT_BLOCK, ATT_BLOCK)
    nxt = pl.multiple_of(jnp.minimum(n + 1, nb - 1) * ATT_BLOCK, ATT_BLOCK)

    def band(ref):
        return jnp.concatenate(
            [ref[pl.ds(prev, ATT_BLOCK), :], ref[pl.ds(own, ATT_BLOCK), :], ref[pl.ds(nxt, ATT_BLOCK), :]], axis=0)

    kb = band(k_ref).astype(F32)
    vb = band(v_ref).astype(F32)
    key = lax.broadcasted_iota(jnp.int32, (3 * ATT_BLOCK, 1), 0)
    valid = ((key >= ATT_BLOCK) | (n > 0)) & ((key < 2 * ATT_BLOCK) | (n < nb - 1))
    low = lax.broadcasted_iota(jnp.int32, (1, LANES), 1) < HEAD_DIM
    low_row = lax.broadcasted_iota(jnp.int32, (LANES, 1), 0) < HEAD_DIM

    def halves(x, c, kv_is_low):
        a = x[:, c * LANES:(c + 1) * LANES]
        r = pltpu.roll(a, HEAD_DIM, 1)
        in_low, in_high = (a, r) if kv_is_low else (r, a)
        return jnp.where(low, in_low, 0.0), jnp.where(low, 0.0, in_high)

    pairs = N_HEADS // N_KV_HEADS // 2
    for g in range(N_KV_HEADS):
        k_par = [h.astype(BF16) for h in halves(kb, g // 2, g % 2 == 0)]
        vt_par = [h.T.astype(BF16) for h in halves(vb, g // 2, g % 2 == 0)]
        base = g * pairs * LANES
        qp = jnp.concatenate([q_ref[:, base + p * LANES:base + (p + 1) * LANES] for p in range(pairs)], axis=0)
        qp = qp * jnp.asarray(HEAD_DIM ** -0.5, BF16)
        probs, inv_den = [], []
        for e in range(2):
            s = lax.dot_general(k_par[e], qp, (((1,), (1,)), ((), ())), preferred_element_type=F32)
            s = jnp.where(valid, s + bias_ref[g, e], NEG_INF)
            sink = jnp.concatenate(
                [jnp.full((1, ATT_BLOCK), sink_ref[(g * pairs + p) * 2 + e], F32) for p in range(pairs)], axis=1)
            m = jnp.maximum(jnp.max(s, axis=0, keepdims=True), sink)
            p_un = jnp.exp(s - m)
            den = jnp.sum(p_un, axis=0, keepdims=True) + jnp.exp(sink - m)
            probs.append(p_un.astype(BF16))
            inv_den.append(1.0 / den)
        o_t = (jnp.dot(vt_par[0], probs[0], preferred_element_type=F32)
               + jnp.dot(vt_par[1], probs[1], preferred_element_type=F32))
        o = (o_t * jnp.where(low_row, inv_den[0], inv_den[1])).T
        for p in range(pairs):
            o_ref[:, base + p * LANES:base + (p + 1) * LANES] = o[p * ATT_BLOCK:(p + 1) * ATT_BLOCK].astype(o_ref.dtype)


def _t5_bucket(rel):
    nb = NUM_BUCKETS // 2
    max_exact = nb // 2
    ret = (rel > 0).astype(jnp.int32) * nb
    n = jnp.abs(rel)
    nf = jnp.maximum(n, 1).astype(jnp.float32)
    large = max_exact + (jnp.log(nf / max_exact) / np.float32(np.log(MAX_DISTANCE / max_exact))
                         * (nb - max_exact)).astype(jnp.int32)
    large = jnp.minimum(large, nb - 1)
    return ret + jnp.where(n < max_exact, n, large)


def _band_bias(rel_table):
    q_off = jnp.arange(ATT_BLOCK)[:, None]
    k_off = jnp.arange(3 * ATT_BLOCK)[None, :]
    rel = k_off - ATT_BLOCK - q_off
    hot = (_t5_bucket(rel)[..., None] == jnp.arange(NUM_BUCKETS)).astype(F32)
    bias = jnp.einsum("qkb,bh->hqk", hot, rel_table.astype(F32), precision=lax.Precision.HIGHEST)
    bias = jnp.where(jnp.abs(rel) <= WINDOW, bias, NEG_INF)
    pairs = N_HEADS // N_KV_HEADS // 2
    bias = bias.reshape(N_KV_HEADS, pairs, 2, ATT_BLOCK, 3 * ATT_BLOCK).transpose(0, 2, 4, 1, 3)
    return bias.reshape(N_KV_HEADS, 2, 3 * ATT_BLOCK, pairs * ATT_BLOCK)


def _swa(proj, sink, bias, batch, seq):
    nb = seq // ATT_BLOCK
    t = batch * seq
    return pl.pallas_call(
        _swa_body,
        grid_spec=pltpu.PrefetchScalarGridSpec(
            num_scalar_prefetch=0,
            grid=(batch, nb),
            in_specs=[
                pl.BlockSpec(memory_space=pltpu.SMEM),
                pl.BlockSpec((ATT_BLOCK, ATTN_WIDTH), lambda b, n: (b * nb + n, COL_Q // ATTN_WIDTH)),
                pl.BlockSpec((seq, KV_WIDTH), lambda b, n: (b, COL_K // KV_WIDTH)),
                pl.BlockSpec((seq, KV_WIDTH), lambda b, n: (b, COL_V // KV_WIDTH)),
                pl.BlockSpec(bias.shape, lambda b, n: (0, 0, 0, 0)),
            ],
            out_specs=pl.BlockSpec((ATT_BLOCK, ATTN_WIDTH), lambda b, n: (b * nb + n, 0)),
        ),
        out_shape=jax.ShapeDtypeStruct((t, ATTN_WIDTH), BF16),
        compiler_params=_cparams(("parallel", "arbitrary")),
        name="swa",
    )(sink, proj, proj, proj, bias)


def _memkv_body(m_ref, w_ref, o_ref):
    o_ref[...] = jnp.dot(m_ref[...].astype(BF16), w_ref[...], preferred_element_type=F32).astype(o_ref.dtype)


def _memkv(mem2d, w):
    rows = mem2d.shape[0]
    tm, tn = 512, 1024
    return pl.pallas_call(
        _memkv_body,
        grid=(rows // tm, (2 * MEM_WIDTH) // tn),
        in_specs=[
            pl.BlockSpec((tm, D_MODEL), lambda i, j: (i, 0)),
            pl.BlockSpec((D_MODEL, tn), lambda i, j: (0, j)),
        ],
        out_specs=pl.BlockSpec((tm, tn), lambda i, j: (i, j)),
        out_shape=jax.ShapeDtypeStruct((rows, 2 * MEM_WIDTH), BF16),
        compiler_params=_cparams(("parallel", "arbitrary")),
        name="memkv",
    )(mem2d, w)


def _memattn_body(q_ref, k_ref, v_ref, o_ref):
    outs = []
    for h in range(MEM_HEADS):
        cols = slice(h * MEM_HEAD_DIM, (h + 1) * MEM_HEAD_DIM)
        s = lax.dot_general(q_ref[:, cols], k_ref[:, cols], (((1,), (1,)), ((), ())), preferred_element_type=F32)
        s = s * np.float32(MEM_HEAD_DIM ** -0.5)
        m = jnp.max(s, axis=-1, keepdims=True)
        p = jnp.exp(s - m)
        den = jnp.sum(p, axis=-1, keepdims=True)
        o = jnp.dot(p.astype(BF16), v_ref[:, cols], preferred_element_type=F32)
        outs.append(o / den)
    o_ref[...] = jnp.concatenate(outs, axis=1).astype(o_ref.dtype)


def _memattn(proj, kv, batch, seq):
    t = batch * seq
    nq = seq // TQ_MEM
    return pl.pallas_call(
        _memattn_body,
        grid=(batch, nq),
        in_specs=[
            pl.BlockSpec((TQ_MEM, MEM_WIDTH), lambda b, i: (b * nq + i, COL_MQ // MEM_WIDTH)),
            pl.BlockSpec((MEM_LEN, MEM_WIDTH), lambda b, i: (b, 0)),
            pl.BlockSpec((MEM_LEN, MEM_WIDTH), lambda b, i: (b, 1)),
        ],
        out_specs=pl.BlockSpec((TQ_MEM, MEM_WIDTH), lambda b, i: (b * nq + i, 0)),
        out_shape=jax.ShapeDtypeStruct((t, MEM_WIDTH), BF16),
        compiler_params=_cparams(("parallel", "arbitrary")),
        name="memattn",
    )(proj, kv, kv)


def _branches_body(yf_ref, att_ref, mo_ref, g0_ref, g1_ref, g2_ref, wao_ref, wmo_ref, o_ref):
    y_a = jnp.dot(att_ref[...], wao_ref[...], preferred_element_type=F32)
    y_m = jnp.dot(mo_ref[...], wmo_ref[...], preferred_element_type=F32)
    mixed = (g0_ref[...].astype(F32) * yf_ref[...].astype(F32)
             + g1_ref[...].astype(F32) * y_a + g2_ref[...].astype(F32) * y_m)
    o_ref[...] = mixed.astype(o_ref.dtype)


def _branches(y_f, att, mo, proj, w_ao, w_mo):
    t = y_f.shape[0]
    tm = TM_MERGE
    row = lambda i: (i, 0)
    const = lambda i: (0, 0)
    resident = dict(pipeline_mode=pl.Buffered(1))
    return pl.pallas_call(
        _branches_body,
        grid=(t // tm,),
        in_specs=[
            pl.BlockSpec((tm, D_MODEL), row),
            pl.BlockSpec((tm, ATTN_WIDTH), row),
            pl.BlockSpec((tm, MEM_WIDTH), row),
            pl.BlockSpec((tm, D_MODEL), lambda i: (i, 0)),
            pl.BlockSpec((tm, D_MODEL), lambda i: (i, 1)),
            pl.BlockSpec((tm, D_MODEL), lambda i: (i, 2)),
            pl.BlockSpec((ATTN_WIDTH, D_MODEL), const, **resident),
            pl.BlockSpec((MEM_WIDTH, D_MODEL), const, **resident),
        ],
        out_specs=pl.BlockSpec((tm, D_MODEL), row),
        out_shape=jax.ShapeDtypeStruct((t, D_MODEL), BF16),
        compiler_params=_cparams(("parallel",)),
        name="branches",
    )(y_f, att, mo, proj, proj, proj, w_ao, w_mo)


def _merge_body(mixed_ref, x_ref, wo_ref, n2_ref, wr_ref, br_ref,
                x1_ref, h2s_ref, te_ref, tp_ref, tw_ref, cnt_ref, carry_ref):
    i = pl.program_id(0)
    tm = x_ref.shape[0]

    @pl.when(i == 0)
    def _init():
        carry_ref[...] = jnp.zeros_like(carry_ref)

    x1 = x_ref[...] + jnp.dot(mixed_ref[...], wo_ref[...], preferred_element_type=F32)
    x1_ref[...] = x1

    inv = lax.rsqrt(jnp.mean(x1 * x1, axis=-1, keepdims=True) + RMS_EPS)
    h2 = x1 * inv * n2_ref[...]
    _pack_rows(h2, h2s_ref, 0, tm)

    h_hi = h2.astype(BF16)
    h_lo = (h2 - h_hi.astype(F32)).astype(BF16)
    w_r = wr_ref[...]
    w_hi = w_r.astype(BF16)
    w_lo = (w_r - w_hi.astype(F32)).astype(BF16)
    part = jnp.dot(h_hi, jnp.concatenate([w_hi, w_lo], axis=1), preferred_element_type=F32)
    logits = (part[:, :N_EXPERTS] + part[:, N_EXPERTS:]
              + jnp.dot(h_lo, w_hi, preferred_element_type=F32) + br_ref[...])
    lane = lax.broadcasted_iota(jnp.int32, (tm, N_EXPERTS), 1)
    vals, idxs, hots = [], [], []
    rest = logits
    for _ in range(TOP_K):
        mk = jnp.max(rest, axis=-1, keepdims=True)
        ik = jnp.min(jnp.where(rest == mk, lane, N_EXPERTS), axis=-1, keepdims=True)
        hot = lane == ik
        vals.append(mk)
        idxs.append(ik)
        hots.append(hot)
        rest = jnp.where(hot, -jnp.inf, rest)
    exps = [jnp.exp(v - vals[0]) for v in vals]
    den = exps[0] + exps[1] + exps[2] + exps[3]

    sel = (hots[0] | hots[1] | hots[2] | hots[3]).astype(F32)
    r_i = lax.broadcasted_iota(jnp.int32, (tm, tm), 0)
    c_i = lax.broadcasted_iota(jnp.int32, (tm, tm), 1)
    tri = (c_i < r_i).astype(BF16)
    pos = jnp.dot(tri, sel.astype(BF16), preferred_element_type=F32) + carry_ref[...]
    poss = [jnp.sum(jnp.where(hot, pos, 0.0), axis=-1, keepdims=True) for hot in hots]
    carry = carry_ref[...] + jnp.sum(sel, axis=0, keepdims=True)
    carry_ref[...] = carry

    te_ref[...] = jnp.concatenate(idxs, axis=1)
    tp_ref[...] = jnp.concatenate(poss, axis=1).astype(jnp.int32)
    tw_ref[...] = jnp.concatenate([e / den for e in exps], axis=1)
    cnt_ref[...] = carry.astype(jnp.int32)


def _merge(mixed, x2d, w_o, n2, w_r, b_r):
    t = x2d.shape[0]
    tm = TM_MERGE
    row = lambda i: (i, 0)
    const = lambda i: (0, 0)
    resident = dict(pipeline_mode=pl.Buffered(1))
    return pl.pallas_call(
        _merge_body,
        grid=(t // tm,),
        in_specs=[
            pl.BlockSpec((tm, D_MODEL), row),
            pl.BlockSpec((tm, D_MODEL), row),
            pl.BlockSpec((D_MODEL, D_MODEL), const, **resident),
            pl.BlockSpec((1, D_MODEL), const),
            pl.BlockSpec((D_MODEL, N_EXPERTS), const),
            pl.BlockSpec((1, N_EXPERTS), const),
        ],
        out_specs=[
            pl.BlockSpec((tm, D_MODEL), row),
            pl.BlockSpec((tm * PACK_ROWS, LANES), row),
            pl.BlockSpec((tm, TOP_K), row),
            pl.BlockSpec((tm, TOP_K), row),
            pl.BlockSpec((tm, TOP_K), row),
            pl.BlockSpec((1, N_EXPERTS), const),
        ],
        out_shape=[
            jax.ShapeDtypeStruct((t, D_MODEL), F32),
            jax.ShapeDtypeStruct((t * PACK_ROWS, LANES), jnp.uint32),
            jax.ShapeDtypeStruct((t, TOP_K), jnp.int32),
            jax.ShapeDtypeStruct((t, TOP_K), jnp.int32),
            jax.ShapeDtypeStruct((t, TOP_K), F32),
            jax.ShapeDtypeStruct((1, N_EXPERTS), jnp.int32),
        ],
        scratch_shapes=[pltpu.VMEM((1, N_EXPERTS), F32)],
        compiler_params=_cparams(("arbitrary",)),
        name="merge",
    )(mixed, x2d, w_o, n2, w_r, b_r)


def _row_slice(ref, row, n_rows=1):
    start = pl.multiple_of(row * PACK_ROWS, PACK_ROWS)
    return ref.at[pl.ds(start, n_rows * PACK_ROWS), :]


def _dispatch_body(zstart_ref, dest_ref, h_ref, xs_ref, zero_ref, zsem, sem):
    i = pl.program_id(0)
    tm = h_ref.shape[0] // PACK_ROWS

    @pl.when(i == 0)
    def _zero_tails():
        zero_ref[...] = jnp.zeros_like(zero_ref)

        def tail_copy(e):
            return pltpu.make_async_copy(zero_ref, _row_slice(xs_ref, zstart_ref[e], SUB_E), zsem)

        def start(e, c):
            @pl.when(zstart_ref[e] >= 0)
            def _():
                tail_copy(e).start()
            return c

        def wait(e, c):
            @pl.when(zstart_ref[e] >= 0)
            def _():
                tail_copy(e).wait()
            return c

        lax.fori_loop(0, zstart_ref.shape[0], start, 0)
        lax.fori_loop(0, zstart_ref.shape[0], wait, 0)

    def scatter(t, c):
        for k in range(TOP_K):
            d = dest_ref[t * TOP_K + k]
            pltpu.make_async_copy(_row_slice(h_ref, t), _row_slice(xs_ref, d), sem).start()
        return c

    lax.fori_loop(0, tm, scatter, 0)
    for _ in range(TOP_K):
        pltpu.make_async_copy(h_ref, _row_slice(xs_ref, 0, tm), sem).wait()


def _dispatch(zstart, dest_flat, h2s, n_slots):
    t = h2s.shape[0] // PACK_ROWS
    tm = TM_DISP
    return pl.pallas_call(
        _dispatch_body,
        grid_spec=pltpu.PrefetchScalarGridSpec(
            num_scalar_prefetch=1,
            grid=(t // tm,),
            in_specs=[
                pl.BlockSpec((tm * TOP_K,), lambda i, z: (i,), memory_space=pltpu.SMEM),
                pl.BlockSpec((tm * PACK_ROWS, LANES), lambda i, z: (i, 0)),
            ],
            out_specs=pl.BlockSpec(memory_space=pl.ANY),
            scratch_shapes=[
                pltpu.VMEM((SUB_E * PACK_ROWS, LANES), U32),
                pltpu.SemaphoreType.DMA(()),
                pltpu.SemaphoreType.DMA(()),
            ],
        ),
        out_shape=jax.ShapeDtypeStruct((n_slots * PACK_ROWS, LANES), U32),
        compiler_params=_cparams(("arbitrary",)),
        name="dispatch",
    )(zstart, dest_flat, h2s)


def _experts_body(ge_ref, gn_ref, nu_ref, xs_ref, wgu_ref, wd_ref, bgu_ref, bd_ref, o_ref,
                  xb_ref, wgu_bf_ref, wd_bf_ref, gu_even_ref, gu_odd_ref, act_ref):
    g = pl.program_id(0)
    f = pl.program_id(1)
    nsub = gn_ref[g]
    half = TF_E // 2
    gu_bufs = (gu_even_ref, gu_odd_ref)

    @pl.when((nsub == 0) & (f == 0))
    def _unused_group():
        o_ref[...] = jnp.zeros_like(o_ref)

    def load_rows():
        for r in range(PACK_ROWS):
            w = xs_ref[pl.ds(r, GROUP_E, stride=PACK_ROWS), :]
            lo = pltpu.bitcast(lax.shift_left(w, U32(16)), F32)
            hi = pltpu.bitcast(w & U32(0xFFFF0000), F32)
            xb_ref[:, (2 * r) * LANES:(2 * r + 1) * LANES] = lo.astype(BF16)
            xb_ref[:, (2 * r + 1) * LANES:(2 * r + 2) * LANES] = hi.astype(BF16)

    def convert_weights():
        wgu_bf_ref[...] = wgu_ref[...].astype(BF16)
        lo = _bf16_bits(wd_ref[0:half, :])
        hi = _bf16_bits(wd_ref[half:TF_E, :])
        chunk_rows = pl.ds(pl.multiple_of(f * TF_E, TF_E), TF_E)
        wd_bf_ref[chunk_rows, :] = pltpu.bitcast(lax.shift_right_logical(lo, U32(16)) | hi, BF16)

    def up_matmul(rows, parity):
        gu_bufs[parity][rows, :] = (jnp.dot(xb_ref[rows, :], wgu_bf_ref[...], preferred_element_type=F32)
                                    + bgu_ref[...])

    def activation(rows, parity):
        gu = gu_bufs[1 - parity][rows, :]
        gate = jnp.minimum(gu, SWIGLU_LIMIT)
        glu = gate * (1.0 / (1.0 + jnp.exp(-gate * SWIGLU_ALPHA)))
        up1 = jnp.clip(gu, -SWIGLU_LIMIT, SWIGLU_LIMIT) + 1.0
        prod = glu * pltpu.roll(up1, 2 * TF_E - 1, 1)
        even = (lax.broadcasted_iota(jnp.int32, (1, TF_E), 1) & 1) == 0
        act_ref[f - 1, rows, :] = jnp.where(even, prod[:, :TF_E], pltpu.roll(prod[:, TF_E:], 1, 1)).astype(BF16)

    def down_and_store(first_row, n_rows):
        rows = pl.ds(first_row, n_rows)
        act = jnp.concatenate([act_ref[c, rows, :] for c in range(NF_E)], axis=1)
        y = jnp.dot(act, wd_bf_ref[...], preferred_element_type=F32) + bd_ref[...]
        _pack_rows(y, o_ref, first_row, n_rows)

    def over_rows(parity, first, last):
        @pl.when(nsub > NSUB_E // 2)
        def _mostly_full():
            rows = pl.ds(0, GROUP_E)
            if first:
                load_rows()
            if not last:
                convert_weights()
                up_matmul(rows, parity)
            if not first:
                activation(rows, parity)
            if last:
                for s in range(GROUP_E // DOWN_E):
                    down_and_store(s * DOWN_E, DOWN_E)

        @pl.when(nsub <= NSUB_E // 2)
        def _few_rows():
            if first:
                load_rows()
            if not last:
                convert_weights()
            if last:
                o_ref[...] = jnp.zeros_like(o_ref)

            def body(s, c):
                first_row = pl.multiple_of(s * SUB_E, SUB_E)
                rows = pl.ds(first_row, SUB_E)
                if not last:
                    up_matmul(rows, parity)
                if not first:
                    activation(rows, parity)
                if last:
                    down_and_store(first_row, SUB_E)
                return c
            lax.fori_loop(0, nsub, body, 0)

    @pl.when(nsub > 0)
    def _group():
        @pl.when(f == 0)
        def _first_chunk():
            over_rows(0, True, False)

        for parity in range(2):
            @pl.when((f > 0) & (f < NF_E) & (f % 2 == parity))
            def _middle_chunk():
                over_rows(parity, False, False)

        @pl.when(f == NF_E)
        def _drain():
            over_rows(NF_E % 2, False, True)


def _experts(group_e, group_nsub, n_used, xs, w_gu, w_dn, b_gu, b_dn, layer, n_slots):
    ngrp = n_slots // GROUP_E
    nf = D_FF // TF_E

    def fchunk(g, f, nu):
        return jnp.where(g < nu[0], jnp.minimum(f, nf - 1), nf - 1)

    return pl.pallas_call(
        _experts_body,
        grid_spec=pltpu.PrefetchScalarGridSpec(
            num_scalar_prefetch=3,
            grid=(ngrp, nf + 1),
            in_specs=[
                pl.BlockSpec((GROUP_E * PACK_ROWS, LANES), lambda g, f, ge, gn, nu: (jnp.minimum(g, nu[0] - 1), 0)),
                pl.BlockSpec((None, None, D_MODEL, 2 * TF_E),
                             lambda g, f, ge, gn, nu: (layer, ge[g], 0, fchunk(g, f, nu))),
                pl.BlockSpec((None, None, TF_E, D_MODEL),
                             lambda g, f, ge, gn, nu: (layer, ge[g], fchunk(g, f, nu), 0)),
                pl.BlockSpec((None, None, 1, 2 * TF_E),
                             lambda g, f, ge, gn, nu: (layer, ge[g], 0, fchunk(g, f, nu))),
                pl.BlockSpec((None, None, 1, D_MODEL), lambda g, f, ge, gn, nu: (layer, ge[g], 0, 0)),
            ],
            out_specs=pl.BlockSpec((GROUP_E * PACK_ROWS, LANES), lambda g, f, ge, gn, nu: (g, 0)),
            scratch_shapes=[
                pltpu.VMEM((GROUP_E, D_MODEL), BF16),
                pltpu.VMEM((D_MODEL, 2 * TF_E), BF16),
                pltpu.VMEM((D_FF, D_MODEL), BF16),
                pltpu.VMEM((GROUP_E, 2 * TF_E), F32),
                pltpu.VMEM((GROUP_E, 2 * TF_E), F32),
                pltpu.VMEM((NF_E, GROUP_E, TF_E), BF16),
            ],
        ),
        out_shape=jax.ShapeDtypeStruct((n_slots * PACK_ROWS, LANES), U32),
        compiler_params=_cparams(("arbitrary", "arbitrary")),
        name="experts",
    )(group_e, group_nsub, n_used, xs, w_gu, w_dn, b_gu, b_dn)


def _combine_body(dest_ref, dnext_ref, ys_ref, w_ref, x1_ref, g_ref, o_ref, buf_ref, sems):
    i = pl.program_id(0)
    n = pl.num_programs(0)
    tm = x1_ref.shape[0]
    buf_rows = tm * TOP_K
    slot = i % 2

    def issue(d_ref, b):
        def body(t, c):
            for k in range(TOP_K):
                src = _row_slice(ys_ref, d_ref[t * TOP_K + k])
                pltpu.make_async_copy(src, _row_slice(buf_ref, b * buf_rows + k * tm + t), sems.at[b]).start()
            return c
        lax.fori_loop(0, tm, body, 0)

    @pl.when(i == 0)
    def _first_tile():
        issue(dest_ref, 0)

    @pl.when(i + 1 < n)
    def _next_tile():
        issue(dnext_ref, 1 - slot)

    base = slot * buf_rows
    pltpu.make_async_copy(_row_slice(ys_ref, 0, buf_rows), _row_slice(buf_ref, base, buf_rows), sems.at[slot]).wait()

    w = w_ref[...]
    wk = [jnp.broadcast_to(w[:, k:k + 1], (tm, LANES)) for k in range(TOP_K)]
    ssq = jnp.zeros((tm, 1), F32)
    for r in range(PACK_ROWS):
        acc_lo = acc_hi = None
        for k in range(TOP_K):
            word = buf_ref[pl.ds((base + k * tm) * PACK_ROWS + r, tm, stride=PACK_ROWS), :]
            lo = pltpu.bitcast(lax.shift_left(word, U32(16)), F32) * wk[k]
            hi = pltpu.bitcast(word & U32(0xFFFF0000), F32) * wk[k]
            acc_lo = lo if acc_lo is None else acc_lo + lo
            acc_hi = hi if acc_hi is None else acc_hi + hi
        for c, moe in ((2 * r, acc_lo), (2 * r + 1, acc_hi)):
            cols = slice(c * LANES, (c + 1) * LANES)
            v = x1_ref[:, cols] + moe
            o_ref[:, cols] = v
            ssq = ssq + jnp.sum(v * v, axis=-1, keepdims=True)
    inv = lax.rsqrt(ssq * np.float32(1.0 / D_MODEL) + RMS_EPS)
    o_ref[...] = o_ref[...] * inv * g_ref[...]


def _combine(dest_flat, ys, top_w, x1, g):
    t = x1.shape[0]
    tm = TM_COMB
    nt = t // tm
    return pl.pallas_call(
        _combine_body,
        grid=(nt,),
        in_specs=[
            pl.BlockSpec((tm * TOP_K,), lambda i: (i,), memory_space=pltpu.SMEM),
            pl.BlockSpec((tm * TOP_K,), lambda i: (jnp.minimum(i + 1, nt - 1),), memory_space=pltpu.SMEM),
            pl.BlockSpec(memory_space=pl.ANY),
            pl.BlockSpec((tm, TOP_K), lambda i: (i, 0)),
            pl.BlockSpec((tm, D_MODEL), lambda i: (i, 0)),
            pl.BlockSpec((1, D_MODEL), lambda i: (0, 0)),
        ],
        out_specs=pl.BlockSpec((tm, D_MODEL), lambda i: (i, 0)),
        out_shape=jax.ShapeDtypeStruct((t, D_MODEL), F32),
        scratch_shapes=[pltpu.VMEM((2 * tm * TOP_K * PACK_ROWS, LANES), U32), pltpu.SemaphoreType.DMA((2,))],
        compiler_params=_cparams(("arbitrary",)),
        name="combine",
    )(dest_flat, dest_flat, ys, top_w, x1, g)


def _routing_tables(counts, top_e, top_pos, n_slots):
    i32 = jnp.int32
    ngrp = n_slots // GROUP_E
    experts = jnp.arange(N_EXPERTS, dtype=i32)
    padded = (counts + GROUP_E - 1) // GROUP_E * GROUP_E
    ends = jnp.cumsum(padded)
    starts = ends - padded
    total = ends[-1]
    dest = jnp.sum(jnp.where(top_e[..., None] == experts, starts, 0), axis=-1) + top_pos
    gstart = jnp.arange(ngrp, dtype=i32) * GROUP_E
    used = gstart < total
    of_group = gstart[:, None] >= ends[None, :]
    last_e = jnp.max(jnp.where(padded > 0, experts, 0))
    group_e = jnp.where(used, jnp.minimum(jnp.sum(of_group, axis=1), N_EXPERTS - 1), last_e)
    valid_end = jnp.sum(jnp.where(group_e[:, None] == experts, starts + counts, 0), axis=-1)
    rows = jnp.clip(valid_end - gstart, 0, GROUP_E)
    group_nsub = jnp.where(used, (rows + SUB_E - 1) // SUB_E, 0)
    n_used = (total // GROUP_E).reshape(1)
    sub = jnp.arange(NSUB_E, dtype=i32) * SUB_E
    tail = (starts + counts // SUB_E * SUB_E)[:, None] + sub
    tail = jnp.where(tail < ends[:, None], tail, -1)
    spare = (n_used[0] + experts)[:, None] * GROUP_E + sub
    spare = jnp.where(spare < n_slots, spare, -1)
    zstart = jnp.concatenate([tail.reshape(-1), spare.reshape(-1)])
    return (dest.reshape(-1).astype(i32), group_e.astype(i32), group_nsub.astype(i32), n_used.astype(i32),
            zstart.astype(i32))


def kernel(x, mem, rel_bias_table, norm1_g, w_in, attn_sink, w_mem_kv, w_fourier_out, w_attn_out, w_mem_out,
           w_o, norm2_g, w_router, b_router, w_gate_up, b_gate_up, w_down, b_down, final_g):
    batch, seq, _ = x.shape
    t = batch * seq
    depth = norm1_g.shape[0]
    assert depth == 1, "the combine kernel applies the final RMSNorm, so a single layer is supported"
    x2d = x.reshape(t, D_MODEL)
    mem2d = mem.reshape(batch * MEM_LEN, D_MODEL)
    bias = _band_bias(rel_bias_table)
    n_slots = t * TOP_K + N_EXPERTS * GROUP_E

    for l in range(depth):
        wi = w_in[l]
        wi = jnp.concatenate([wi[:, _OFF_GATE:], wi[:, :_OFF_Q], wi[:, _OFF_MQ:_OFF_GATE],
                              wi[:, _OFF_Q:_OFF_K], wi[:, _OFF_K:_OFF_V], wi[:, _OFF_V:_OFF_MQ]], axis=1).astype(BF16)
        proj = _inproj(x2d, norm1_g[l].reshape(1, D_MODEL), wi)
        y_f = _fourier(proj, w_fourier_out[l].astype(BF16), batch, seq)
        att = _swa(proj, attn_sink[l], bias, batch, seq)
        kv = _memkv(mem2d, w_mem_kv[l].astype(BF16))
        mo = _memattn(proj, kv, batch, seq)
        mixed = _branches(y_f, att, mo, proj, w_attn_out[l].astype(BF16), w_mem_out[l].astype(BF16))
        x1, h2s, top_e, top_pos, top_w, counts = _merge(
            mixed, x2d, w_o[l].astype(BF16),
            norm2_g[l].reshape(1, D_MODEL), w_router[l], b_router[l].reshape(1, N_EXPERTS))
        dest, group_e, group_nsub, n_used, zstart = _routing_tables(counts[0], top_e, top_pos, n_slots)
        xs = _dispatch(zstart, dest, h2s, n_slots)
        ys = _experts(group_e, group_nsub, n_used, xs, w_gate_up, w_down,
                      b_gate_up.reshape(depth, N_EXPERTS, 1, 2 * D_FF), b_down.reshape(depth, N_EXPERTS, 1, D_MODEL),
                      l, n_slots)
        x2d = _combine(dest, ys, top_w, x1, final_g.reshape(1, D_MODEL))
    return x2d.reshape(batch, seq, D_MODEL)
```

```python
import functools

import numpy as np
import jax
import jax.numpy as jnp
from jax import lax
from jax.experimental import pallas as pl
from jax.experimental.pallas import tpu as pltpu

D_MODEL = 2048
MEM_LEN = 256
FOURIER_GROUPS = 4
FOURIER_GROUP_DIM = 256
FOURIER_WIDTH = 1024
N_HEADS = 32
N_KV_HEADS = 4
HEAD_DIM = 64
ATTN_WIDTH = 2048
KV_WIDTH = 256
WINDOW = 128
ATT_BLOCK = 128
MEM_HEADS = 4
MEM_HEAD_DIM = 256
MEM_WIDTH = 1024
N_BRANCHES = 3
NUM_BUCKETS = 32
MAX_DISTANCE = 128
N_EXPERTS = 32
TOP_K = 4
D_FF = 2048
SWIGLU_LIMIT = 7.0
SWIGLU_ALPHA = 1.702
RMS_EPS = 1e-5
NEG_INF = -1e30

GATE_WIDTH = N_BRANCHES * D_MODEL
COL_GATE = 0
COL_F = GATE_WIDTH
COL_MQ = COL_F + FOURIER_WIDTH
COL_Q = COL_MQ + MEM_WIDTH
COL_K = COL_Q + ATTN_WIDTH
COL_V = COL_K + KV_WIDTH
IN_WIDTH = COL_V + KV_WIDTH

_OFF_Q = FOURIER_WIDTH
_OFF_K = _OFF_Q + ATTN_WIDTH
_OFF_V = _OFF_K + KV_WIDTH
_OFF_MQ = _OFF_V + KV_WIDTH
_OFF_GATE = _OFF_MQ + MEM_WIDTH

LANES = 128
PACK_ROWS = D_MODEL // (2 * LANES)
VMEM_LIMIT = 56 * 1024 * 1024

TM_IN, TN_IN = 512, 1536
GATE_SLICES = 3
TS_F = 512
TQ_MEM = 512
TM_MERGE = 512
TM_DISP = 256
GROUP_E, SUB_E, TF_E = 1024, 128, 256
NSUB_E = GROUP_E // SUB_E
NF_E = D_FF // TF_E
DOWN_E = 512
TM_COMB = 256

BF16 = jnp.bfloat16
F32 = jnp.float32
U32 = jnp.uint32


def _cparams(sem):
    return pltpu.CompilerParams(dimension_semantics=sem, vmem_limit_bytes=VMEM_LIMIT)


def _bf16_bits(v):
    return pltpu.bitcast(v.astype(BF16).astype(F32), U32)


def _pack_rows(v, out_ref, first_row, rows):
    for r in range(PACK_ROWS):
        lo = _bf16_bits(v[:, (2 * r) * LANES:(2 * r + 1) * LANES])
        hi = _bf16_bits(v[:, (2 * r + 1) * LANES:(2 * r + 2) * LANES])
        out_ref[pl.ds(first_row * PACK_ROWS + r, rows, stride=PACK_ROWS), :] = lax.shift_right_logical(lo, U32(16)) | hi


def _unpack_rows(ref, first_row, rows):
    chunks = []
    for r in range(PACK_ROWS):
        w = ref[pl.ds(first_row * PACK_ROWS + r, rows, stride=PACK_ROWS), :]
        chunks.append(pltpu.bitcast(lax.shift_left(w, U32(16)), F32))
        chunks.append(pltpu.bitcast(w & U32(0xFFFF0000), F32))
    return chunks


def _inproj_body(x_ref, g_ref, w_ref, o_ref, h_ref, *, n_gate_tiles):
    j = pl.program_id(1)

    @pl.when(j == 0)
    def _norm():
        x = x_ref[...]
        inv = lax.rsqrt(jnp.mean(x * x, axis=-1, keepdims=True) + RMS_EPS)
        h_ref[...] = (x * inv * g_ref[...]).astype(BF16)

    @pl.when(j < n_gate_tiles)
    def _gate():
        width = w_ref.shape[1] // GATE_SLICES
        for c in range(GATE_SLICES):
            cols = slice(c * width, (c + 1) * width)
            acc = jnp.dot(h_ref[...], w_ref[:, cols], preferred_element_type=F32)
            o_ref[:, cols] = (1.0 / (1.0 + jnp.exp(-acc))).astype(o_ref.dtype)

    @pl.when(j >= n_gate_tiles)
    def _plain():
        o_ref[...] = jnp.dot(h_ref[...], w_ref[...], preferred_element_type=F32).astype(o_ref.dtype)


def _inproj(x2d, g, w):
    t = x2d.shape[0]
    return pl.pallas_call(
        functools.partial(_inproj_body, n_gate_tiles=GATE_WIDTH // TN_IN),
        grid=(t // TM_IN, IN_WIDTH // TN_IN),
        in_specs=[
            pl.BlockSpec((TM_IN, D_MODEL), lambda i, j: (i, 0)),
            pl.BlockSpec((1, D_MODEL), lambda i, j: (0, 0)),
            pl.BlockSpec((D_MODEL, TN_IN), lambda i, j: (0, j)),
        ],
        out_specs=pl.BlockSpec((TM_IN, TN_IN), lambda i, j: (i, j)),
        out_shape=jax.ShapeDtypeStruct((t, IN_WIDTH), BF16),
        scratch_shapes=[pltpu.VMEM((TM_IN, D_MODEL), BF16)],
        compiler_params=_cparams(("parallel", "arbitrary")),
        name="inproj",
    )(x2d, g, w)


def _fourier_body(f_ref, cc_ref, sc_ref, cs_ref, wfo_ref, o_ref, z_ref, *, seq):
    st = pl.program_id(1)

    @pl.when(st == 0)
    def _channel_dft():
        for g in range(FOURIER_GROUPS):
            cols = slice(g * FOURIER_GROUP_DIM, (g + 1) * FOURIER_GROUP_DIM)
            xg = f_ref[:, cols]
            z_ref[0:seq, cols] = jnp.dot(xg, cc_ref[...], preferred_element_type=F32).astype(BF16)
            z_ref[seq:2 * seq, cols] = jnp.dot(xg, sc_ref[...], preferred_element_type=F32).astype(BF16)

    y = jnp.dot(cs_ref[...], z_ref[...], preferred_element_type=F32)
    o_ref[...] = jnp.dot(y.astype(BF16), wfo_ref[...], preferred_element_type=F32).astype(o_ref.dtype)


def _dft_tables(seq):
    c = FOURIER_GROUP_DIM
    kc = (np.arange(c)[:, None] * np.arange(c)[None, :]) % c
    ac = 2.0 * np.pi * kc / c
    ks = (np.arange(seq)[:, None] * np.arange(seq)[None, :]) % seq
    a_s = 2.0 * np.pi * ks / seq
    cc = (np.cos(ac) / np.sqrt(c)).astype(np.float32)
    sc = (np.sin(ac) / np.sqrt(c)).astype(np.float32)
    cs = np.concatenate([np.cos(a_s), -np.sin(a_s)], axis=1) / np.sqrt(seq)
    return cc, sc, cs.astype(np.float32)


def _fourier(proj, w_fo, batch, seq):
    cc, sc, cs = _dft_tables(seq)
    cc = jnp.asarray(cc).astype(BF16)
    sc = jnp.asarray(sc).astype(BF16)
    cs = jnp.asarray(cs).astype(BF16)
    t = batch * seq
    return pl.pallas_call(
        functools.partial(_fourier_body, seq=seq),
        grid=(batch, seq // TS_F),
        in_specs=[
            pl.BlockSpec((seq, FOURIER_WIDTH), lambda b, s: (b, COL_F // FOURIER_WIDTH)),
            pl.BlockSpec((FOURIER_GROUP_DIM, FOURIER_GROUP_DIM), lambda b, s: (0, 0)),
            pl.BlockSpec((FOURIER_GROUP_DIM, FOURIER_GROUP_DIM), lambda b, s: (0, 0)),
            pl.BlockSpec((TS_F, 2 * seq), lambda b, s: (s, 0)),
            pl.BlockSpec((FOURIER_WIDTH, D_MODEL), lambda b, s: (0, 0)),
        ],
        out_specs=pl.BlockSpec((TS_F, D_MODEL), lambda b, s: (b * (seq // TS_F) + s, 0)),
        out_shape=jax.ShapeDtypeStruct((t, D_MODEL), BF16),
        scratch_shapes=[pltpu.VMEM((2 * seq, FOURIER_WIDTH), BF16)],
        compiler_params=_cparams(("parallel", "arbitrary")),
        name="fourier",
    )(proj, cc, sc, cs, w_fo)


def _swa_body(sink_ref, q_ref, k_ref, v_ref, bias_ref, o_ref):
    n = pl.program_id(1)
    nb = pl.num_programs(1)
    prev = pl.multiple_of(jnp.maximum(n - 1, 0) * ATT_BLOCK, ATT_BLOCK)
    own = pl.multiple_of(n * ATT_BLOCK, ATT_BLOCK)
    nxt = pl.multiple_of(jnp.minimum(n + 1, nb - 1) * ATT_BLOCK, ATT_BLOCK)

    def band(ref):
        return jnp.concatenate(
            [ref[pl.ds(prev, ATT_BLOCK), :], ref[pl.ds(own, ATT_BLOCK), :], ref[pl.ds(nxt, ATT_BLOCK), :]], axis=0)

    kb = band(k_ref).astype(F32)
    vb = band(v_ref).astype(F32)
    key = lax.broadcasted_iota(jnp.int32, (3 * ATT_BLOCK, 1), 0)
    valid = ((key >= ATT_BLOCK) | (n > 0)) & ((key < 2 * ATT_BLOCK) | (n < nb - 1))
    low = lax.broadcasted_iota(jnp.int32, (1, LANES), 1) < HEAD_DIM
    low_row = lax.broadcasted_iota(jnp.int32, (LANES, 1), 0) < HEAD_DIM

    def halves(x, c, kv_is_low):
        a = x[:, c * LANES:(c + 1) * LANES]
        r = pltpu.roll(a, HEAD_DIM, 1)
        in_low, in_high = (a, r) if kv_is_low else (r, a)
        return jnp.where(low, in_low, 0.0), jnp.where(low, 0.0, in_high)

    pairs = N_HEADS // N_KV_HEADS // 2
    for g in range(N_KV_HEADS):
        k_par = [h.astype(BF16) for h in halves(kb, g // 2, g % 2 == 0)]
        vt_par = [h.T.astype(BF16) for h in halves(vb, g // 2, g % 2 == 0)]
        base = g * pairs * LANES
        qp = jnp.concatenate([q_ref[:, base + p * LANES:base + (p + 1) * LANES] for p in range(pairs)], axis=0)
        qp = qp * jnp.asarray(HEAD_DIM ** -0.5, BF16)
        probs, inv_den = [], []
        for e in range(2):
            s = lax.dot_general(k_par[e], qp, (((1,), (1,)), ((), ())), preferred_element_type=F32)
            s = jnp.where(valid, s + bias_ref[g, e], NEG_INF)
            sink = jnp.concatenate(
                [jnp.full((1, ATT_BLOCK), sink_ref[(g * pairs + p) * 2 + e], F32) for p in range(pairs)], axis=1)
            m = jnp.maximum(jnp.max(s, axis=0, keepdims=True), sink)
            p_un = jnp.exp(s - m)
            den = jnp.sum(p_un, axis=0, keepdims=True) + jnp.exp(sink - m)
            probs.append(p_un.astype(BF16))
            inv_den.append(1.0 / den)
        o_t = (jnp.dot(vt_par[0], probs[0], preferred_element_type=F32)
               + jnp.dot(vt_par[1], probs[1], preferred_element_type=F32))
        o = (o_t * jnp.where(low_row, inv_den[0], inv_den[1])).T
        for p in range(pairs):
            o_ref[:, base + p * LANES:base + (p + 1) * LANES] = o[p * ATT_BLOCK:(p + 1) * ATT_BLOCK].astype(o_ref.dtype)


def _t5_bucket(rel):
    nb = NUM_BUCKETS // 2
    max_exact = nb // 2
    ret = (rel > 0).astype(jnp.int32) * nb
    n = jnp.abs(rel)
    nf = jnp.maximum(n, 1).astype(jnp.float32)
    large = max_exact + (jnp.log(nf / max_exact) / np.float32(np.log(MAX_DISTANCE / max_exact))
                         * (nb - max_exact)).astype(jnp.int32)
    large = jnp.minimum(large, nb - 1)
    return ret + jnp.where(n < max_exact, n, large)


def _band_bias(rel_table):
    q_off = jnp.arange(ATT_BLOCK)[:, None]
    k_off = jnp.arange(3 * ATT_BLOCK)[None, :]
    rel = k_off - ATT_BLOCK - q_off
    hot = (_t5_bucket(rel)[..., None] == jnp.arange(NUM_BUCKETS)).astype(F32)
    bias = jnp.einsum("qkb,bh->hqk", hot, rel_table.astype(F32), precision=lax.Precision.HIGHEST)
    bias = jnp.where(jnp.abs(rel) <= WINDOW, bias, NEG_INF)
    pairs = N_HEADS // N_KV_HEADS // 2
    bias = bias.reshape(N_KV_HEADS, pairs, 2, ATT_BLOCK, 3 * ATT_BLOCK).transpose(0, 2, 4, 1, 3)
    return bias.reshape(N_KV_HEADS, 2, 3 * ATT_BLOCK, pairs * ATT_BLOCK)


def _swa(proj, sink, bias, batch, seq):
    nb = seq // ATT_BLOCK
    t = batch * seq
    return pl.pallas_call(
        _swa_body,
        grid_spec=pltpu.PrefetchScalarGridSpec(
            num_scalar_prefetch=0,
            grid=(batch, nb),
            in_specs=[
                pl.BlockSpec(memory_space=pltpu.SMEM),
                pl.BlockSpec((ATT_BLOCK, ATTN_WIDTH), lambda b, n: (b * nb + n, COL_Q // ATTN_WIDTH)),
                pl.BlockSpec((seq, KV_WIDTH), lambda b, n: (b, COL_K // KV_WIDTH)),
                pl.BlockSpec((seq, KV_WIDTH), lambda b, n: (b, COL_V // KV_WIDTH)),
                pl.BlockSpec(bias.shape, lambda b, n: (0, 0, 0, 0)),
            ],
            out_specs=pl.BlockSpec((ATT_BLOCK, ATTN_WIDTH), lambda b, n: (b * nb + n, 0)),
        ),
        out_shape=jax.ShapeDtypeStruct((t, ATTN_WIDTH), BF16),
        compiler_params=_cparams(("parallel", "arbitrary")),
        name="swa",
    )(sink, proj, proj, proj, bias)


def _memkv_body(m_ref, w_ref, o_ref):
    o_ref[...] = jnp.dot(m_ref[...].astype(BF16), w_ref[...], preferred_element_type=F32).astype(o_ref.dtype)


def _memkv(mem2d, w):
    rows = mem2d.shape[0]
    tm, tn = 512, 1024
    return pl.pallas_call(
        _memkv_body,
        grid=(rows // tm, (2 * MEM_WIDTH) // tn),
        in_specs=[
            pl.BlockSpec((tm, D_MODEL), lambda i, j: (i, 0)),
            pl.BlockSpec((D_MODEL, tn), lambda i, j: (0, j)),
        ],
        out_specs=pl.BlockSpec((tm, tn), lambda i, j: (i, j)),
        out_shape=jax.ShapeDtypeStruct((rows, 2 * MEM_WIDTH), BF16),
        compiler_params=_cparams(("parallel", "arbitrary")),
        name="memkv",
    )(mem2d, w)


def _memattn_body(q_ref, k_ref, v_ref, o_ref):
    outs = []
    for h in range(MEM_HEADS):
        cols = slice(h * MEM_HEAD_DIM, (h + 1) * MEM_HEAD_DIM)
        s = lax.dot_general(q_ref[:, cols], k_ref[:, cols], (((1,), (1,)), ((), ())), preferred_element_type=F32)
        s = s * np.float32(MEM_HEAD_DIM ** -0.5)
        m = jnp.max(s, axis=-1, keepdims=True)
        p = jnp.exp(s - m)
        den = jnp.sum(p, axis=-1, keepdims=True)
        o = jnp.dot(p.astype(BF16), v_ref[:, cols], preferred_element_type=F32)
        outs.append(o / den)
    o_ref[...] = jnp.concatenate(outs, axis=1).astype(o_ref.dtype)


def _memattn(proj, kv, batch, seq):
    t = batch * seq
    nq = seq // TQ_MEM
    return pl.pallas_call(
        _memattn_body,
        grid=(batch, nq),
        in_specs=[
            pl.BlockSpec((TQ_MEM, MEM_WIDTH), lambda b, i: (b * nq + i, COL_MQ // MEM_WIDTH)),
            pl.BlockSpec((MEM_LEN, MEM_WIDTH), lambda b, i: (b, 0)),
            pl.BlockSpec((MEM_LEN, MEM_WIDTH), lambda b, i: (b, 1)),
        ],
        out_specs=pl.BlockSpec((TQ_MEM, MEM_WIDTH), lambda b, i: (b * nq + i, 0)),
        out_shape=jax.ShapeDtypeStruct((t, MEM_WIDTH), BF16),
        compiler_params=_cparams(("parallel", "arbitrary")),
        name="memattn",
    )(proj, kv, kv)


def _branches_body(yf_ref, att_ref, mo_ref, g0_ref, g1_ref, g2_ref, wao_ref, wmo_ref, o_ref):
    y_a = jnp.dot(att_ref[...], wao_ref[...], preferred_element_type=F32)
    y_m = jnp.dot(mo_ref[...], wmo_ref[...], preferred_element_type=F32)
    mixed = (g0_ref[...].astype(F32) * yf_ref[...].astype(F32)
             + g1_ref[...].astype(F32) * y_a + g2_ref[...].astype(F32) * y_m)
    o_ref[...] = mixed.astype(o_ref.dtype)


def _branches(y_f, att, mo, proj, w_ao, w_mo):
    t = y_f.shape[0]
    tm = TM_MERGE
    row = lambda i: (i, 0)
    const = lambda i: (0, 0)
    resident = dict(pipeline_mode=pl.Buffered(1))
    return pl.pallas_call(
        _branches_body,
        grid=(t // tm,),
        in_specs=[
            pl.BlockSpec((tm, D_MODEL), row),
            pl.BlockSpec((tm, ATTN_WIDTH), row),
            pl.BlockSpec((tm, MEM_WIDTH), row),
            pl.BlockSpec((tm, D_MODEL), lambda i: (i, 0)),
            pl.BlockSpec((tm, D_MODEL), lambda i: (i, 1)),
            pl.BlockSpec((tm, D_MODEL), lambda i: (i, 2)),
            pl.BlockSpec((ATTN_WIDTH, D_MODEL), const, **resident),
            pl.BlockSpec((MEM_WIDTH, D_MODEL), const, **resident),
        ],
        out_specs=pl.BlockSpec((tm, D_MODEL), row),
        out_shape=jax.ShapeDtypeStruct((t, D_MODEL), BF16),
        compiler_params=_cparams(("parallel",)),
        name="branches",
    )(y_f, att, mo, proj, proj, proj, w_ao, w_mo)


def _merge_body(mixed_ref, x_ref, wo_ref, n2_ref, wr_ref, br_ref,
                x1_ref, h2s_ref, te_ref, tp_ref, tw_ref, cnt_ref, carry_ref):
    i = pl.program_id(0)
    tm = x_ref.shape[0]

    @pl.when(i == 0)
    def _init():
        carry_ref[...] = jnp.zeros_like(carry_ref)

    x1 = x_ref[...] + jnp.dot(mixed_ref[...], wo_ref[...], preferred_element_type=F32)
    x1_ref[...] = x1

    inv = lax.rsqrt(jnp.mean(x1 * x1, axis=-1, keepdims=True) + RMS_EPS)
    h2 = x1 * inv * n2_ref[...]
    _pack_rows(h2, h2s_ref, 0, tm)

    h_hi = h2.astype(BF16)
    h_lo = (h2 - h_hi.astype(F32)).astype(BF16)
    w_r = wr_ref[...]
    w_hi = w_r.astype(BF16)
    w_lo = (w_r - w_hi.astype(F32)).astype(BF16)
    part = jnp.dot(h_hi, jnp.concatenate([w_hi, w_lo], axis=1), preferred_element_type=F32)
    logits = (part[:, :N_EXPERTS] + part[:, N_EXPERTS:]
              + jnp.dot(h_lo, w_hi, preferred_element_type=F32) + br_ref[...])
    lane = lax.broadcasted_iota(jnp.int32, (tm, N_EXPERTS), 1)
    vals, idxs, hots = [], [], []
    rest = logits
    for _ in range(TOP_K):
        mk = jnp.max(rest, axis=-1, keepdims=True)
        ik = jnp.min(jnp.where(rest == mk, lane, N_EXPERTS), axis=-1, keepdims=True)
        hot = lane == ik
        vals.append(mk)
        idxs.append(ik)
        hots.append(hot)
        rest = jnp.where(hot, -jnp.inf, rest)
    exps = [jnp.exp(v - vals[0]) for v in vals]
    den = exps[0] + exps[1] + exps[2] + exps[3]

    sel = (hots[0] | hots[1] | hots[2] | hots[3]).astype(F32)
    r_i = lax.broadcasted_iota(jnp.int32, (tm, tm), 0)
    c_i = lax.broadcasted_iota(jnp.int32, (tm, tm), 1)
    tri = (c_i < r_i).astype(BF16)
    pos = jnp.dot(tri, sel.astype(BF16), preferred_element_type=F32) + carry_ref[...]
    poss = [jnp.sum(jnp.where(hot, pos, 0.0), axis=-1, keepdims=True) for hot in hots]
    carry = carry_ref[...] + jnp.sum(sel, axis=0, keepdims=True)
    carry_ref[...] = carry

    te_ref[...] = jnp.concatenate(idxs, axis=1)
    tp_ref[...] = jnp.concatenate(poss, axis=1).astype(jnp.int32)
    tw_ref[...] = jnp.concatenate([e / den for e in exps], axis=1)
    cnt_ref[...] = carry.astype(jnp.int32)


def _merge(mixed, x2d, w_o, n2, w_r, b_r):
    t = x2d.shape[0]
    tm = TM_MERGE
    row = lambda i: (i, 0)
    const = lambda i: (0, 0)
    resident = dict(pipeline_mode=pl.Buffered(1))
    return pl.pallas_call(
        _merge_body,
        grid=(t // tm,),
        in_specs=[
            pl.BlockSpec((tm, D_MODEL), row),
            pl.BlockSpec((tm, D_MODEL), row),
            pl.BlockSpec((D_MODEL, D_MODEL), const, **resident),
            pl.BlockSpec((1, D_MODEL), const),
            pl.BlockSpec((D_MODEL, N_EXPERTS), const),
            pl.BlockSpec((1, N_EXPERTS), const),
        ],
        out_specs=[
            pl.BlockSpec((tm, D_MODEL), row),
            pl.BlockSpec((tm * PACK_ROWS, LANES), row),
            pl.BlockSpec((tm, TOP_K), row),
            pl.BlockSpec((tm, TOP_K), row),
            pl.BlockSpec((tm, TOP_K), row),
            pl.BlockSpec((1, N_EXPERTS), const),
        ],
        out_shape=[
            jax.ShapeDtypeStruct((t, D_MODEL), F32),
            jax.ShapeDtypeStruct((t * PACK_ROWS, LANES), jnp.uint32),
            jax.ShapeDtypeStruct((t, TOP_K), jnp.int32),
            jax.ShapeDtypeStruct((t, TOP_K), jnp.int32),
            jax.ShapeDtypeStruct((t, TOP_K), F32),
            jax.ShapeDtypeStruct((1, N_EXPERTS), jnp.int32),
        ],
        scratch_shapes=[pltpu.VMEM((1, N_EXPERTS), F32)],
        compiler_params=_cparams(("arbitrary",)),
        name="merge",
    )(mixed, x2d, w_o, n2, w_r, b_r)


def _row_slice(ref, row, n_rows=1):
    start = pl.multiple_of(row * PACK_ROWS, PACK_ROWS)
    return ref.at[pl.ds(start, n_rows * PACK_ROWS), :]


def _dispatch_body(zstart_ref, dest_ref, h_ref, xs_ref, zero_ref, zsem, sem):
    i = pl.program_id(0)
    tm = h_ref.shape[0] // PACK_ROWS

    @pl.when(i == 0)
    def _zero_tails():
        zero_ref[...] = jnp.zeros_like(zero_ref)

        def tail_copy(e):
            return pltpu.make_async_copy(zero_ref, _row_slice(xs_ref, zstart_ref[e], SUB_E), zsem)

        def start(e, c):
            @pl.when(zstart_ref[e] >= 0)
            def _():
                tail_copy(e).start()
            return c

        def wait(e, c):
            @pl.when(zstart_ref[e] >= 0)
            def _():
                tail_copy(e).wait()
            return c

        lax.fori_loop(0, zstart_ref.shape[0], start, 0)
        lax.fori_loop(0, zstart_ref.shape[0], wait, 0)

    def scatter(t, c):
        for k in range(TOP_K):
            d = dest_ref[t * TOP_K + k]
            pltpu.make_async_copy(_row_slice(h_ref, t), _row_slice(xs_ref, d), sem).start(priority=k % 2)
        return c

    lax.fori_loop(0, tm, scatter, 0)
    for _ in range(TOP_K):
        pltpu.make_async_copy(h_ref, _row_slice(xs_ref, 0, tm), sem).wait()


def _dispatch(zstart, dest_flat, h2s, n_slots):
    t = h2s.shape[0] // PACK_ROWS
    tm = TM_DISP
    return pl.pallas_call(
        _dispatch_body,
        grid_spec=pltpu.PrefetchScalarGridSpec(
            num_scalar_prefetch=1,
            grid=(t // tm,),
            in_specs=[
                pl.BlockSpec((tm * TOP_K,), lambda i, z: (i,), memory_space=pltpu.SMEM),
                pl.BlockSpec((tm * PACK_ROWS, LANES), lambda i, z: (i, 0)),
            ],
            out_specs=pl.BlockSpec(memory_space=pl.ANY),
            scratch_shapes=[
                pltpu.VMEM((SUB_E * PACK_ROWS, LANES), U32),
                pltpu.SemaphoreType.DMA(()),
                pltpu.SemaphoreType.DMA(()),
            ],
        ),
        out_shape=jax.ShapeDtypeStruct((n_slots * PACK_ROWS, LANES), U32),
        compiler_params=_cparams(("arbitrary",)),
        name="dispatch",
    )(zstart, dest_flat, h2s)


def _experts_body(ge_ref, gn_ref, nu_ref, xs_ref, wgu_ref, wd_ref, bgu_ref, bd_ref, o_ref,
                  xb_ref, wgu_bf_ref, wd_bf_ref, gu_even_ref, gu_odd_ref, act_ref):
    g = pl.program_id(0)
    f = pl.program_id(1)
    nsub = gn_ref[g]
    half = TF_E // 2
    gu_bufs = (gu_even_ref, gu_odd_ref)

    @pl.when((nsub == 0) & (f == 0))
    def _unused_group():
        o_ref[...] = jnp.zeros_like(o_ref)

    def load_rows():
        for r in range(PACK_ROWS):
            w = xs_ref[pl.ds(r, GROUP_E, stride=PACK_ROWS), :]
            lo = pltpu.bitcast(lax.shift_left(w, U32(16)), F32)
            hi = pltpu.bitcast(w & U32(0xFFFF0000), F32)
            xb_ref[:, (2 * r) * LANES:(2 * r + 1) * LANES] = lo.astype(BF16)
            xb_ref[:, (2 * r + 1) * LANES:(2 * r + 2) * LANES] = hi.astype(BF16)

    def convert_weights():
        wgu_bf_ref[...] = wgu_ref[...].astype(BF16)
        lo = _bf16_bits(wd_ref[0:half, :])
        hi = _bf16_bits(wd_ref[half:TF_E, :])
        chunk_rows = pl.ds(pl.multiple_of(f * TF_E, TF_E), TF_E)
        wd_bf_ref[chunk_rows, :] = pltpu.bitcast(lax.shift_right_logical(lo, U32(16)) | hi, BF16)

    def up_matmul(rows, parity):
        gu_bufs[parity][rows, :] = (jnp.dot(xb_ref[rows, :], wgu_bf_ref[...], preferred_element_type=F32)
                                    + bgu_ref[...])

    def activation(rows, parity):
        gu = gu_bufs[1 - parity][rows, :]
        gate = jnp.minimum(gu, SWIGLU_LIMIT)
        glu = gate * (1.0 / (1.0 + jnp.exp(-gate * SWIGLU_ALPHA)))
        up1 = jnp.clip(gu, -SWIGLU_LIMIT, SWIGLU_LIMIT) + 1.0
        prod = glu * pltpu.roll(up1, 2 * TF_E - 1, 1)
        even = (lax.broadcasted_iota(jnp.int32, (1, TF_E), 1) & 1) == 0
        act_ref[f - 1, rows, :] = jnp.where(even, prod[:, :TF_E], pltpu.roll(prod[:, TF_E:], 1, 1)).astype(BF16)

    def down_and_store(first_row, n_rows):
        rows = pl.ds(first_row, n_rows)
        act = jnp.concatenate([act_ref[c, rows, :] for c in range(NF_E)], axis=1)
        y = jnp.dot(act, wd_bf_ref[...], preferred_element_type=F32) + bd_ref[...]
        _pack_rows(y, o_ref, first_row, n_rows)

    def over_rows(parity, first, last):
        @pl.when(nsub > NSUB_E // 2)
        def _mostly_full():
            rows = pl.ds(0, GROUP_E)
            if first:
                load_rows()
            if not last:
                convert_weights()
                up_matmul(rows, parity)
            if not first:
                activation(rows, parity)
            if last:
                for s in range(GROUP_E // DOWN_E):
                    down_and_store(s * DOWN_E, DOWN_E)

        @pl.when(nsub <= NSUB_E // 2)
        def _few_rows():
            if first:
                load_rows()
            if not last:
                convert_weights()
            if last:
                o_ref[...] = jnp.zeros_like(o_ref)

            def body(s, c):
                first_row = pl.multiple_of(s * SUB_E, SUB_E)
                rows = pl.ds(first_row, SUB_E)
                if not last:
                    up_matmul(rows, parity)
                if not first:
                    activation(rows, parity)
                if last:
                    down_and_store(first_row, SUB_E)
                return c
            lax.fori_loop(0, nsub, body, 0)

    @pl.when(nsub > 0)
    def _group():
        @pl.when(f == 0)
        def _first_chunk():
            over_rows(0, True, False)

        for parity in range(2):
            @pl.when((f > 0) & (f < NF_E) & (f % 2 == parity))
            def _middle_chunk():
                over_rows(parity, False, False)

        @pl.when(f == NF_E)
        def _drain():
            over_rows(NF_E % 2, False, True)


def _experts(group_e, group_nsub, n_used, xs, w_gu, w_dn, b_gu, b_dn, layer, n_slots):
    ngrp = n_slots // GROUP_E
    nf = NF_E

    def fchunk(g, f, nu):
        return jnp.where(g < nu[0], jnp.clip(f, 0, nf - 1), nf - 1)

    return pl.pallas_call(
        _experts_body,
        grid_spec=pltpu.PrefetchScalarGridSpec(
            num_scalar_prefetch=3,
            grid=(ngrp, nf + 1),
            in_specs=[
                pl.BlockSpec((GROUP_E * PACK_ROWS, LANES), lambda g, f, ge, gn, nu: (jnp.minimum(g, nu[0] - 1), 0)),
                pl.BlockSpec((None, None, D_MODEL, 2 * TF_E),
                             lambda g, f, ge, gn, nu: (layer, ge[g], 0, fchunk(g, f, nu))),
                pl.BlockSpec((None, None, TF_E, D_MODEL),
                             lambda g, f, ge, gn, nu: (layer, ge[g], fchunk(g, f, nu), 0)),
                pl.BlockSpec((None, None, 1, 2 * TF_E),
                             lambda g, f, ge, gn, nu: (layer, ge[g], 0, fchunk(g, f, nu))),
                pl.BlockSpec((None, None, 1, D_MODEL), lambda g, f, ge, gn, nu: (layer, ge[g], 0, 0)),
            ],
            out_specs=pl.BlockSpec((GROUP_E * PACK_ROWS, LANES), lambda g, f, ge, gn, nu: (g, 0)),
            scratch_shapes=[
                pltpu.VMEM((GROUP_E, D_MODEL), BF16),
                pltpu.VMEM((D_MODEL, 2 * TF_E), BF16),
                pltpu.VMEM((D_FF, D_MODEL), BF16),
                pltpu.VMEM((GROUP_E, 2 * TF_E), F32),
                pltpu.VMEM((GROUP_E, 2 * TF_E), F32),
                pltpu.VMEM((NF_E, GROUP_E, TF_E), BF16),
            ],
        ),
        out_shape=jax.ShapeDtypeStruct((n_slots * PACK_ROWS, LANES), U32),
        compiler_params=_cparams(("arbitrary", "arbitrary")),
        name="experts",
    )(group_e, group_nsub, n_used, xs, w_gu, w_dn, b_gu, b_dn)


def _combine_body(dest_ref, dnext_ref, ys_ref, w_ref, x1_ref, g_ref, o_ref, buf_ref, sems):
    i = pl.program_id(0)
    n = pl.num_programs(0)
    tm = x1_ref.shape[0]
    buf_rows = tm * TOP_K
    slot = i % 2

    def issue(d_ref, b):
        def body(t, c):
            for k in range(TOP_K):
                src = _row_slice(ys_ref, d_ref[t * TOP_K + k])
                dst = _row_slice(buf_ref, b * buf_rows + k * tm + t)
                pltpu.make_async_copy(src, dst, sems.at[b]).start(priority=k % 2)
            return c
        lax.fori_loop(0, tm, body, 0)

    @pl.when(i == 0)
    def _first_tile():
        issue(dest_ref, 0)

    @pl.when(i + 1 < n)
    def _next_tile():
        issue(dnext_ref, 1 - slot)

    base = slot * buf_rows
    pltpu.make_async_copy(_row_slice(ys_ref, 0, buf_rows), _row_slice(buf_ref, base, buf_rows), sems.at[slot]).wait()

    w = w_ref[...]
    wk = [jnp.broadcast_to(w[:, k:k + 1], (tm, LANES)) for k in range(TOP_K)]
    ssq = jnp.zeros((tm, 1), F32)
    for r in range(PACK_ROWS):
        acc_lo = acc_hi = None
        for k in range(TOP_K):
            word = buf_ref[pl.ds((base + k * tm) * PACK_ROWS + r, tm, stride=PACK_ROWS), :]
            lo = pltpu.bitcast(lax.shift_left(word, U32(16)), F32) * wk[k]
            hi = pltpu.bitcast(word & U32(0xFFFF0000), F32) * wk[k]
            acc_lo = lo if acc_lo is None else acc_lo + lo
            acc_hi = hi if acc_hi is None else acc_hi + hi
        for c, moe in ((2 * r, acc_lo), (2 * r + 1, acc_hi)):
            cols = slice(c * LANES, (c + 1) * LANES)
            v = x1_ref[:, cols] + moe
            o_ref[:, cols] = v
            ssq = ssq + jnp.sum(v * v, axis=-1, keepdims=True)
    inv = lax.rsqrt(ssq * np.float32(1.0 / D_MODEL) + RMS_EPS)
    o_ref[...] = o_ref[...] * inv * g_ref[...]


def _combine(dest_flat, ys, top_w, x1, g):
    t = x1.shape[0]
    tm = TM_COMB
    nt = t // tm
    return pl.pallas_call(
        _combine_body,
        grid=(nt,),
        in_specs=[
            pl.BlockSpec((tm * TOP_K,), lambda i: (i,), memory_space=pltpu.SMEM),
            pl.BlockSpec((tm * TOP_K,), lambda i: (jnp.minimum(i + 1, nt - 1),), memory_space=pltpu.SMEM),
            pl.BlockSpec(memory_space=pl.ANY),
            pl.BlockSpec((tm, TOP_K), lambda i: (i, 0)),
            pl.BlockSpec((tm, D_MODEL), lambda i: (i, 0)),
            pl.BlockSpec((1, D_MODEL), lambda i: (0, 0)),
        ],
        out_specs=pl.BlockSpec((tm, D_MODEL), lambda i: (i, 0)),
        out_shape=jax.ShapeDtypeStruct((t, D_MODEL), F32),
        scratch_shapes=[pltpu.VMEM((2 * tm * TOP_K * PACK_ROWS, LANES), U32), pltpu.SemaphoreType.DMA((2,))],
        compiler_params=_cparams(("arbitrary",)),
        name="combine",
    )(dest_flat, dest_flat, ys, top_w, x1, g)


def _routing_tables(counts, top_e, top_pos, n_slots):
    i32 = jnp.int32
    ngrp = n_slots // GROUP_E
    experts = jnp.arange(N_EXPERTS, dtype=i32)
    padded = (counts + GROUP_E - 1) // GROUP_E * GROUP_E
    ends = jnp.cumsum(padded)
    starts = ends - padded
    total = ends[-1]
    dest = jnp.sum(jnp.where(top_e[..., None] == experts, starts, 0), axis=-1) + top_pos
    gstart = jnp.arange(ngrp, dtype=i32) * GROUP_E
    used = gstart < total
    of_group = gstart[:, None] >= ends[None, :]
    last_e = jnp.max(jnp.where(padded > 0, experts, 0))
    group_e = jnp.where(used, jnp.minimum(jnp.sum(of_group, axis=1), N_EXPERTS - 1), last_e)
    valid_end = jnp.sum(jnp.where(group_e[:, None] == experts, starts + counts, 0), axis=-1)
    rows = jnp.clip(valid_end - gstart, 0, GROUP_E)
    group_nsub = jnp.where(used, (rows + SUB_E - 1) // SUB_E, 0)
    n_used = (total // GROUP_E).reshape(1)
    sub = jnp.arange(NSUB_E, dtype=i32) * SUB_E
    tail = (starts + counts // SUB_E * SUB_E)[:, None] + sub
    tail = jnp.where(tail < ends[:, None], tail, -1)
    spare = (n_used[0] + experts)[:, None] * GROUP_E + sub
    spare = jnp.where(spare < n_slots, spare, -1)
    zstart = jnp.concatenate([tail.reshape(-1), spare.reshape(-1)])
    return (dest.reshape(-1).astype(i32), group_e.astype(i32), group_nsub.astype(i32), n_used.astype(i32),
            zstart.astype(i32))


def kernel(x, mem, rel_bias_table, norm1_g, w_in, attn_sink, w_mem_kv, w_fourier_out, w_attn_out, w_mem_out,
           w_o, norm2_g, w_router, b_router, w_gate_up, b_gate_up, w_down, b_down, final_g):
    batch, seq, _ = x.shape
    t = batch * seq
    depth = norm1_g.shape[0]
    assert depth == 1, "the combine kernel applies the final RMSNorm, so a single layer is supported"
    x2d = x.reshape(t, D_MODEL)
    mem2d = mem.reshape(batch * MEM_LEN, D_MODEL)
    bias = _band_bias(rel_bias_table)
    n_slots = t * TOP_K + N_EXPERTS * GROUP_E

    for l in range(depth):
        wi = w_in[l]
        wi = jnp.concatenate([wi[:, _OFF_GATE:], wi[:, :_OFF_Q], wi[:, _OFF_MQ:_OFF_GATE],
                              wi[:, _OFF_Q:_OFF_K], wi[:, _OFF_K:_OFF_V], wi[:, _OFF_V:_OFF_MQ]], axis=1).astype(BF16)
        proj = _inproj(x2d, norm1_g[l].reshape(1, D_MODEL), wi)
        y_f = _fourier(proj, w_fourier_out[l].astype(BF16), batch, seq)
        att = _swa(proj, attn_sink[l], bias, batch, seq)
        kv = _memkv(mem2d, w_mem_kv[l].astype(BF16))
        mo = _memattn(proj, kv, batch, seq)
        mixed = _branches(y_f, att, mo, proj, w_attn_out[l].astype(BF16), w_mem_out[l].astype(BF16))
        x1, h2s, top_e, top_pos, top_w, counts = _merge(
            mixed, x2d, w_o[l].astype(BF16),
            norm2_g[l].reshape(1, D_MODEL), w_router[l], b_router[l].reshape(1, N_EXPERTS))
        dest, group_e, group_nsub, n_used, zstart = _routing_tables(counts[0], top_e, top_pos, n_slots)
        xs = _dispatch(zstart, dest, h2s, n_slots)
        ys = _experts(group_e, group_nsub, n_used, xs, w_gate_up, w_down,
                      b_gate_up.reshape(depth, N_EXPERTS, 1, 2 * D_FF), b_down.reshape(depth, N_EXPERTS, 1, D_MODEL),
                      l, n_slots)
        x2d = _combine(dest, ys, top_w, x1, final_g.reshape(1, D_MODEL))
    return x2d.reshape(batch, seq, D_MODEL)
```

```python
import functools

import numpy as np
import jax
import jax.numpy as jnp
from jax import lax
from jax.experimental import pallas as pl
from jax.experimental.pallas import tpu as pltpu

D_MODEL = 2048
MEM_LEN = 256
FOURIER_GROUPS = 4
FOURIER_GROUP_DIM = 256
FOURIER_WIDTH = 1024
N_HEADS = 32
N_KV_HEADS = 4
HEAD_DIM = 64
ATTN_WIDTH = 2048
KV_WIDTH = 256
WINDOW = 128
ATT_BLOCK = 128
MEM_HEADS = 4
MEM_HEAD_DIM = 256
MEM_WIDTH = 1024
N_BRANCHES = 3
NUM_BUCKETS = 32
MAX_DISTANCE = 128
N_EXPERTS = 32
TOP_K = 4
D_FF = 2048
SWIGLU_LIMIT = 7.0
SWIGLU_ALPHA = 1.702
RMS_EPS = 1e-5
NEG_INF = -1e30

GATE_WIDTH = N_BRANCHES * D_MODEL
COL_GATE = 0
COL_F = GATE_WIDTH
COL_MQ = COL_F + FOURIER_WIDTH
COL_Q = COL_MQ + MEM_WIDTH
COL_K = COL_Q + ATTN_WIDTH
COL_V = COL_K + KV_WIDTH
IN_WIDTH = COL_V + KV_WIDTH

_OFF_Q = FOURIER_WIDTH
_OFF_K = _OFF_Q + ATTN_WIDTH
_OFF_V = _OFF_K + KV_WIDTH
_OFF_MQ = _OFF_V + KV_WIDTH
_OFF_GATE = _OFF_MQ + MEM_WIDTH

LANES = 128
PACK_ROWS = D_MODEL // (2 * LANES)
VMEM_LIMIT = 56 * 1024 * 1024

TM_IN, TN_IN = 512, 1536
GATE_SLICES = 3
TS_F = 512
TQ_MEM = 512
TM_MERGE = 512
TM_DISP = 256
GROUP_E, SUB_E, TF_E = 1024, 128, 256
NSUB_E = GROUP_E // SUB_E
NF_E = D_FF // TF_E
DOWN_E = 512
TM_COMB = 256

BF16 = jnp.bfloat16
F32 = jnp.float32
U32 = jnp.uint32


def _cparams(sem):
    return pltpu.CompilerParams(dimension_semantics=sem, vmem_limit_bytes=VMEM_LIMIT)


def _bf16_bits(v):
    return pltpu.bitcast(v.astype(BF16).astype(F32), U32)


def _pack_rows(v, out_ref, first_row, rows):
    for r in range(PACK_ROWS):
        lo = _bf16_bits(v[:, (2 * r) * LANES:(2 * r + 1) * LANES])
        hi = _bf16_bits(v[:, (2 * r + 1) * LANES:(2 * r + 2) * LANES])
        out_ref[pl.ds(first_row * PACK_ROWS + r, rows, stride=PACK_ROWS), :] = lax.shift_right_logical(lo, U32(16)) | hi


def _unpack_rows(ref, first_row, rows):
    chunks = []
    for r in range(PACK_ROWS):
        w = ref[pl.ds(first_row * PACK_ROWS + r, rows, stride=PACK_ROWS), :]
        chunks.append(pltpu.bitcast(lax.shift_left(w, U32(16)), F32))
        chunks.append(pltpu.bitcast(w & U32(0xFFFF0000), F32))
    return chunks


def _inproj_body(x_ref, g_ref, w_ref, o_ref, h_ref, *, n_gate_tiles):
    j = pl.program_id(1)

    @pl.when(j == 0)
    def _norm():
        x = x_ref[...]
        inv = lax.rsqrt(jnp.mean(x * x, axis=-1, keepdims=True) + RMS_EPS)
        h_ref[...] = (x * inv * g_ref[...]).astype(BF16)

    @pl.when(j < n_gate_tiles)
    def _gate():
        width = w_ref.shape[1] // GATE_SLICES
        for c in range(GATE_SLICES):
            cols = slice(c * width, (c + 1) * width)
            acc = jnp.dot(h_ref[...], w_ref[:, cols], preferred_element_type=F32)
            o_ref[:, cols] = (1.0 / (1.0 + jnp.exp(-acc))).astype(o_ref.dtype)

    @pl.when(j >= n_gate_tiles)
    def _plain():
        o_ref[...] = jnp.dot(h_ref[...], w_ref[...], preferred_element_type=F32).astype(o_ref.dtype)


def _inproj(x2d, g, w):
    t = x2d.shape[0]
    return pl.pallas_call(
        functools.partial(_inproj_body, n_gate_tiles=GATE_WIDTH // TN_IN),
        grid=(t // TM_IN, IN_WIDTH // TN_IN),
        in_specs=[
            pl.BlockSpec((TM_IN, D_MODEL), lambda i, j: (i, 0)),
            pl.BlockSpec((1, D_MODEL), lambda i, j: (0, 0)),
            pl.BlockSpec((D_MODEL, TN_IN), lambda i, j: (0, j)),
        ],
        out_specs=pl.BlockSpec((TM_IN, TN_IN), lambda i, j: (i, j)),
        out_shape=jax.ShapeDtypeStruct((t, IN_WIDTH), BF16),
        scratch_shapes=[pltpu.VMEM((TM_IN, D_MODEL), BF16)],
        compiler_params=_cparams(("parallel", "arbitrary")),
        name="inproj",
    )(x2d, g, w)


def _fourier_body(f_ref, cc_ref, sc_ref, cs_ref, wfo_ref, o_ref, z_ref, *, seq):
    st = pl.program_id(1)

    @pl.when(st == 0)
    def _channel_dft():
        for g in range(FOURIER_GROUPS):
            cols = slice(g * FOURIER_GROUP_DIM, (g + 1) * FOURIER_GROUP_DIM)
            xg = f_ref[:, cols]
            z_ref[0:seq, cols] = jnp.dot(xg, cc_ref[...], preferred_element_type=F32).astype(BF16)
            z_ref[seq:2 * seq, cols] = jnp.dot(xg, sc_ref[...], preferred_element_type=F32).astype(BF16)

    y = jnp.dot(cs_ref[...], z_ref[...], preferred_element_type=F32)
    o_ref[...] = jnp.dot(y.astype(BF16), wfo_ref[...], preferred_element_type=F32).astype(o_ref.dtype)


def _dft_tables(seq):
    c = FOURIER_GROUP_DIM
    kc = (np.arange(c)[:, None] * np.arange(c)[None, :]) % c
    ac = 2.0 * np.pi * kc / c
    ks = (np.arange(seq)[:, None] * np.arange(seq)[None, :]) % seq
    a_s = 2.0 * np.pi * ks / seq
    cc = (np.cos(ac) / np.sqrt(c)).astype(np.float32)
    sc = (np.sin(ac) / np.sqrt(c)).astype(np.float32)
    cs = np.concatenate([np.cos(a_s), -np.sin(a_s)], axis=1) / np.sqrt(seq)
    return cc, sc, cs.astype(np.float32)


def _fourier(proj, w_fo, batch, seq):
    cc, sc, cs = _dft_tables(seq)
    cc = jnp.asarray(cc).astype(BF16)
    sc = jnp.asarray(sc).astype(BF16)
    cs = jnp.asarray(cs).astype(BF16)
    t = batch * seq
    return pl.pallas_call(
        functools.partial(_fourier_body, seq=seq),
        grid=(batch, seq // TS_F),
        in_specs=[
            pl.BlockSpec((seq, FOURIER_WIDTH), lambda b, s: (b, COL_F // FOURIER_WIDTH)),
            pl.BlockSpec((FOURIER_GROUP_DIM, FOURIER_GROUP_DIM), lambda b, s: (0, 0)),
            pl.BlockSpec((FOURIER_GROUP_DIM, FOURIER_GROUP_DIM), lambda b, s: (0, 0)),
            pl.BlockSpec((TS_F, 2 * seq), lambda b, s: (s, 0)),
            pl.BlockSpec((FOURIER_WIDTH, D_MODEL), lambda b, s: (0, 0)),
        ],
        out_specs=pl.BlockSpec((TS_F, D_MODEL), lambda b, s: (b * (seq // TS_F) + s, 0)),
        out_shape=jax.ShapeDtypeStruct((t, D_MODEL), BF16),
        scratch_shapes=[pltpu.VMEM((2 * seq, FOURIER_WIDTH), BF16)],
        compiler_params=_cparams(("parallel", "arbitrary")),
        name="fourier",
    )(proj, cc, sc, cs, w_fo)


def _swa_body(sink_ref, q_ref, k_ref, v_ref, bias_ref, o_ref):
    n = pl.program_id(1)
    nb = pl.num_programs(1)
    prev = pl.multiple_of(jnp.maximum(n - 1, 0) * ATT_BLOCK, ATT_BLOCK)
    own = pl.multiple_of(n * ATT_BLOCK, ATT_BLOCK)
    nxt = pl.multiple_of(jnp.minimum(n + 1, nb - 1) * ATT_BLOCK, ATT_BLOCK)

    def band(ref):
        return jnp.concatenate(
            [ref[pl.ds(prev, ATT_BLOCK), :], ref[pl.ds(own, ATT_BLOCK), :], ref[pl.ds(nxt, ATT_BLOCK), :]], axis=0)

    kb = band(k_ref).astype(F32)
    vb = band(v_ref).astype(F32)
    key = lax.broadcasted_iota(jnp.int32, (3 * ATT_BLOCK, 1), 0)
    valid = ((key >= ATT_BLOCK) | (n > 0)) & ((key < 2 * ATT_BLOCK) | (n < nb - 1))
    low = lax.broadcasted_iota(jnp.int32, (1, LANES), 1) < HEAD_DIM
    low_row = lax.broadcasted_iota(jnp.int32, (LANES, 1), 0) < HEAD_DIM

    def halves(x, c, kv_is_low):
        a = x[:, c * LANES:(c + 1) * LANES]
        r = pltpu.roll(a, HEAD_DIM, 1)
        in_low, in_high = (a, r) if kv_is_low else (r, a)
        return jnp.where(low, in_low, 0.0), jnp.where(low, 0.0, in_high)

    pairs = N_HEADS // N_KV_HEADS // 2
    for g in range(N_KV_HEADS):
        k_par = [h.astype(BF16) for h in halves(kb, g // 2, g % 2 == 0)]
        vt_par = [h.T.astype(BF16) for h in halves(vb, g // 2, g % 2 == 0)]
        base = g * pairs * LANES
        qp = jnp.concatenate([q_ref[:, base + p * LANES:base + (p + 1) * LANES] for p in range(pairs)], axis=0)
        qp = qp * jnp.asarray(HEAD_DIM ** -0.5, BF16)
        probs, inv_den = [], []
        for e in range(2):
            s = lax.dot_general(k_par[e], qp, (((1,), (1,)), ((), ())), preferred_element_type=F32)
            s = jnp.where(valid, s + bias_ref[g, e], NEG_INF)
            sink = jnp.concatenate(
                [jnp.full((1, ATT_BLOCK), sink_ref[(g * pairs + p) * 2 + e], F32) for p in range(pairs)], axis=1)
            m = jnp.maximum(jnp.max(s, axis=0, keepdims=True), sink)
            p_un = jnp.exp(s - m)
            den = jnp.sum(p_un, axis=0, keepdims=True) + jnp.exp(sink - m)
            probs.append(p_un.astype(BF16))
            inv_den.append(1.0 / den)
        o_t = (jnp.dot(vt_par[0], probs[0], preferred_element_type=F32)
               + jnp.dot(vt_par[1], probs[1], preferred_element_type=F32))
        o = (o_t * jnp.where(low_row, inv_den[0], inv_den[1])).T
        for p in range(pairs):
            o_ref[:, base + p * LANES:base + (p + 1) * LANES] = o[p * ATT_BLOCK:(p + 1) * ATT_BLOCK].astype(o_ref.dtype)


def _t5_bucket(rel):
    nb = NUM_BUCKETS // 2
    max_exact = nb // 2
    ret = (rel > 0).astype(jnp.int32) * nb
    n = jnp.abs(rel)
    nf = jnp.maximum(n, 1).astype(jnp.float32)
    large = max_exact + (jnp.log(nf / max_exact) / np.float32(np.log(MAX_DISTANCE / max_exact))
                         * (nb - max_exact)).astype(jnp.int32)
    large = jnp.minimum(large, nb - 1)
    return ret + jnp.where(n < max_exact, n, large)


def _band_bias(rel_table):
    q_off = jnp.arange(ATT_BLOCK)[:, None]
    k_off = jnp.arange(3 * ATT_BLOCK)[None, :]
    rel = k_off - ATT_BLOCK - q_off
    hot = (_t5_bucket(rel)[..., None] == jnp.arange(NUM_BUCKETS)).astype(F32)
    bias = jnp.einsum("qkb,bh->hqk", hot, rel_table.astype(F32), precision=lax.Precision.HIGHEST)
    bias = jnp.where(jnp.abs(rel) <= WINDOW, bias, NEG_INF)
    pairs = N_HEADS // N_KV_HEADS // 2
    bias = bias.reshape(N_KV_HEADS, pairs, 2, ATT_BLOCK, 3 * ATT_BLOCK).transpose(0, 2, 4, 1, 3)
    return bias.reshape(N_KV_HEADS, 2, 3 * ATT_BLOCK, pairs * ATT_BLOCK)


def _swa(proj, sink, bias, batch, seq):
    nb = seq // ATT_BLOCK
    t = batch * seq
    return pl.pallas_call(
        _swa_body,
        grid_spec=pltpu.PrefetchScalarGridSpec(
            num_scalar_prefetch=0,
            grid=(batch, nb),
            in_specs=[
                pl.BlockSpec(memory_space=pltpu.SMEM),
                pl.BlockSpec((ATT_BLOCK, ATTN_WIDTH), lambda b, n: (b * nb + n, COL_Q // ATTN_WIDTH)),
                pl.BlockSpec((seq, KV_WIDTH), lambda b, n: (b, COL_K // KV_WIDTH)),
                pl.BlockSpec((seq, KV_WIDTH), lambda b, n: (b, COL_V // KV_WIDTH)),
                pl.BlockSpec(bias.shape, lambda b, n: (0, 0, 0, 0)),
            ],
            out_specs=pl.BlockSpec((ATT_BLOCK, ATTN_WIDTH), lambda b, n: (b * nb + n, 0)),
        ),
        out_shape=jax.ShapeDtypeStruct((t, ATTN_WIDTH), BF16),
        compiler_params=_cparams(("parallel", "arbitrary")),
        name="swa",
    )(sink, proj, proj, proj, bias)


def _memkv_body(m_ref, w_ref, o_ref):
    o_ref[...] = jnp.dot(m_ref[...].astype(BF16), w_ref[...], preferred_element_type=F32).astype(o_ref.dtype)


def _memkv(mem2d, w):
    rows = mem2d.shape[0]
    tm, tn = 512, 1024
    return pl.pallas_call(
        _memkv_body,
        grid=(rows // tm, (2 * MEM_WIDTH) // tn),
        in_specs=[
            pl.BlockSpec((tm, D_MODEL), lambda i, j: (i, 0)),
            pl.BlockSpec((D_MODEL, tn), lambda i, j: (0, j)),
        ],
        out_specs=pl.BlockSpec((tm, tn), lambda i, j: (i, j)),
        out_shape=jax.ShapeDtypeStruct((rows, 2 * MEM_WIDTH), BF16),
        compiler_params=_cparams(("parallel", "arbitrary")),
        name="memkv",
    )(mem2d, w)


def _memattn_body(q_ref, k_ref, v_ref, o_ref):
    outs = []
    for h in range(MEM_HEADS):
        cols = slice(h * MEM_HEAD_DIM, (h + 1) * MEM_HEAD_DIM)
        s = lax.dot_general(q_ref[:, cols], k_ref[:, cols], (((1,), (1,)), ((), ())), preferred_element_type=F32)
        s = s * np.float32(MEM_HEAD_DIM ** -0.5)
        m = jnp.max(s, axis=-1, keepdims=True)
        p = jnp.exp(s - m)
        den = jnp.sum(p, axis=-1, keepdims=True)
        o = jnp.dot(p.astype(BF16), v_ref[:, cols], preferred_element_type=F32)
        outs.append(o / den)
    o_ref[...] = jnp.concatenate(outs, axis=1).astype(o_ref.dtype)


def _memattn(proj, kv, batch, seq):
    t = batch * seq
    nq = seq // TQ_MEM
    return pl.pallas_call(
        _memattn_body,
        grid=(batch, nq),
        in_specs=[
            pl.BlockSpec((TQ_MEM, MEM_WIDTH), lambda b, i: (b * nq + i, COL_MQ // MEM_WIDTH)),
            pl.BlockSpec((MEM_LEN, MEM_WIDTH), lambda b, i: (b, 0)),
            pl.BlockSpec((MEM_LEN, MEM_WIDTH), lambda b, i: (b, 1)),
        ],
        out_specs=pl.BlockSpec((TQ_MEM, MEM_WIDTH), lambda b, i: (b * nq + i, 0)),
        out_shape=jax.ShapeDtypeStruct((t, MEM_WIDTH), BF16),
        compiler_params=_cparams(("parallel", "arbitrary")),
        name="memattn",
    )(proj, kv, kv)


def _branches_body(yf_ref, att_ref, mo_ref, g0_ref, g1_ref, g2_ref, wao_ref, wmo_ref, o_ref):
    y_a = jnp.dot(att_ref[...], wao_ref[...], preferred_element_type=F32)
    y_m = jnp.dot(mo_ref[...], wmo_ref[...], preferred_element_type=F32)
    mixed = (g0_ref[...].astype(F32) * yf_ref[...].astype(F32)
             + g1_ref[...].astype(F32) * y_a + g2_ref[...].astype(F32) * y_m)
    o_ref[...] = mixed.astype(o_ref.dtype)


def _branches(y_f, att, mo, proj, w_ao, w_mo):
    t = y_f.shape[0]
    tm = TM_MERGE
    row = lambda i: (i, 0)
    const = lambda i: (0, 0)
    resident = dict(pipeline_mode=pl.Buffered(1))
    return pl.pallas_call(
        _branches_body,
        grid=(t // tm,),
        in_specs=[
            pl.BlockSpec((tm, D_MODEL), row),
            pl.BlockSpec((tm, ATTN_WIDTH), row),
            pl.BlockSpec((tm, MEM_WIDTH), row),
            pl.BlockSpec((tm, D_MODEL), lambda i: (i, 0)),
            pl.BlockSpec((tm, D_MODEL), lambda i: (i, 1)),
            pl.BlockSpec((tm, D_MODEL), lambda i: (i, 2)),
            pl.BlockSpec((ATTN_WIDTH, D_MODEL), const, **resident),
            pl.BlockSpec((MEM_WIDTH, D_MODEL), const, **resident),
        ],
        out_specs=pl.BlockSpec((tm, D_MODEL), row),
        out_shape=jax.ShapeDtypeStruct((t, D_MODEL), BF16),
        compiler_params=_cparams(("parallel",)),
        name="branches",
    )(y_f, att, mo, proj, proj, proj, w_ao, w_mo)


def _merge_body(mixed_ref, x_ref, wo_ref, n2_ref, wr_ref, br_ref,
                x1_ref, h2s_ref, te_ref, tp_ref, tw_ref, cnt_ref, carry_ref):
    i = pl.program_id(0)
    tm = x_ref.shape[0]

    @pl.when(i == 0)
    def _init():
        carry_ref[...] = jnp.zeros_like(carry_ref)

    x1 = x_ref[...] + jnp.dot(mixed_ref[...], wo_ref[...], preferred_element_type=F32)
    x1_ref[...] = x1

    inv = lax.rsqrt(jnp.mean(x1 * x1, axis=-1, keepdims=True) + RMS_EPS)
    h2 = x1 * inv * n2_ref[...]
    _pack_rows(h2, h2s_ref, 0, tm)

    h_hi = h2.astype(BF16)
    h_lo = (h2 - h_hi.astype(F32)).astype(BF16)
    w_r = wr_ref[...]
    w_hi = w_r.astype(BF16)
    w_lo = (w_r - w_hi.astype(F32)).astype(BF16)
    part = jnp.dot(h_hi, jnp.concatenate([w_hi, w_lo], axis=1), preferred_element_type=F32)
    logits = (part[:, :N_EXPERTS] + part[:, N_EXPERTS:]
              + jnp.dot(h_lo, w_hi, preferred_element_type=F32) + br_ref[...])
    lane = lax.broadcasted_iota(jnp.int32, (tm, N_EXPERTS), 1)
    vals, idxs, hots = [], [], []
    rest = logits
    for _ in range(TOP_K):
        mk = jnp.max(rest, axis=-1, keepdims=True)
        ik = jnp.min(jnp.where(rest == mk, lane, N_EXPERTS), axis=-1, keepdims=True)
        hot = lane == ik
        vals.append(mk)
        idxs.append(ik)
        hots.append(hot)
        rest = jnp.where(hot, -jnp.inf, rest)
    exps = [jnp.exp(v - vals[0]) for v in vals]
    den = exps[0] + exps[1] + exps[2] + exps[3]

    sel = (hots[0] | hots[1] | hots[2] | hots[3]).astype(F32)
    r_i = lax.broadcasted_iota(jnp.int32, (tm, tm), 0)
    c_i = lax.broadcasted_iota(jnp.int32, (tm, tm), 1)
    tri = (c_i < r_i).astype(BF16)
    pos = jnp.dot(tri, sel.astype(BF16), preferred_element_type=F32) + carry_ref[...]
    poss = [jnp.sum(jnp.where(hot, pos, 0.0), axis=-1, keepdims=True) for hot in hots]
    carry = carry_ref[...] + jnp.sum(sel, axis=0, keepdims=True)
    carry_ref[...] = carry

    te_ref[...] = jnp.concatenate(idxs, axis=1)
    tp_ref[...] = jnp.concatenate(poss, axis=1).astype(jnp.int32)
    tw_ref[...] = jnp.concatenate([e / den for e in exps], axis=1)
    cnt_ref[...] = carry.astype(jnp.int32)


def _merge(mixed, x2d, w_o, n2, w_r, b_r):
    t = x2d.shape[0]
    tm = TM_MERGE
    row = lambda i: (i, 0)
    const = lambda i: (0, 0)
    resident = dict(pipeline_mode=pl.Buffered(1))
    return pl.pallas_call(
        _merge_body,
        grid=(t // tm,),
        in_specs=[
            pl.BlockSpec((tm, D_MODEL), row),
            pl.BlockSpec((tm, D_MODEL), row),
            pl.BlockSpec((D_MODEL, D_MODEL), const, **resident),
            pl.BlockSpec((1, D_MODEL), const),
            pl.BlockSpec((D_MODEL, N_EXPERTS), const),
            pl.BlockSpec((1, N_EXPERTS), const),
        ],
        out_specs=[
            pl.BlockSpec((tm, D_MODEL), row),
            pl.BlockSpec((tm * PACK_ROWS, LANES), row),
            pl.BlockSpec((tm, TOP_K), row),
            pl.BlockSpec((tm, TOP_K), row),
            pl.BlockSpec((tm, TOP_K), row),
            pl.BlockSpec((1, N_EXPERTS), const),
        ],
        out_shape=[
            jax.ShapeDtypeStruct((t, D_MODEL), F32),
            jax.ShapeDtypeStruct((t * PACK_ROWS, LANES), jnp.uint32),
            jax.ShapeDtypeStruct((t, TOP_K), jnp.int32),
            jax.ShapeDtypeStruct((t, TOP_K), jnp.int32),
            jax.ShapeDtypeStruct((t, TOP_K), F32),
            jax.ShapeDtypeStruct((1, N_EXPERTS), jnp.int32),
        ],
        scratch_shapes=[pltpu.VMEM((1, N_EXPERTS), F32)],
        compiler_params=_cparams(("arbitrary",)),
        name="merge",
    )(mixed, x2d, w_o, n2, w_r, b_r)


def _row_slice(ref, row, n_rows=1):
    start = pl.multiple_of(row * PACK_ROWS, PACK_ROWS)
    return ref.at[pl.ds(start, n_rows * PACK_ROWS), :]


def _dispatch_body(zstart_ref, dest_ref, h_ref, xs_ref, zero_ref, zsem, sem):
    i = pl.program_id(0)
    tm = h_ref.shape[0] // PACK_ROWS

    @pl.when(i == 0)
    def _zero_tails():
        zero_ref[...] = jnp.zeros_like(zero_ref)

        def tail_copy(e):
            return pltpu.make_async_copy(zero_ref, _row_slice(xs_ref, zstart_ref[e], SUB_E), zsem)

        def start(e, c):
            @pl.when(zstart_ref[e] >= 0)
            def _():
                tail_copy(e).start()
            return c

        def wait(e, c):
            @pl.when(zstart_ref[e] >= 0)
            def _():
                tail_copy(e).wait()
            return c

        lax.fori_loop(0, zstart_ref.shape[0], start, 0)
        lax.fori_loop(0, zstart_ref.shape[0], wait, 0)

    def scatter(t, c):
        for k in range(TOP_K):
            d = dest_ref[t * TOP_K + k]
            pltpu.make_async_copy(_row_slice(h_ref, t), _row_slice(xs_ref, d), sem).start(priority=k % 2)
        return c

    lax.fori_loop(0, tm, scatter, 0)
    for _ in range(TOP_K):
        pltpu.make_async_copy(h_ref, _row_slice(xs_ref, 0, tm), sem).wait()


def _dispatch(zstart, dest_flat, h2s, n_slots):
    t = h2s.shape[0] // PACK_ROWS
    tm = TM_DISP
    return pl.pallas_call(
        _dispatch_body,
        grid_spec=pltpu.PrefetchScalarGridSpec(
            num_scalar_prefetch=1,
            grid=(t // tm,),
            in_specs=[
                pl.BlockSpec((tm * TOP_K,), lambda i, z: (i,), memory_space=pltpu.SMEM),
                pl.BlockSpec((tm * PACK_ROWS, LANES), lambda i, z: (i, 0)),
            ],
            out_specs=pl.BlockSpec(memory_space=pl.ANY),
            scratch_shapes=[
                pltpu.VMEM((SUB_E * PACK_ROWS, LANES), U32),
                pltpu.SemaphoreType.DMA(()),
                pltpu.SemaphoreType.DMA(()),
            ],
        ),
        out_shape=jax.ShapeDtypeStruct((n_slots * PACK_ROWS, LANES), U32),
        compiler_params=_cparams(("arbitrary",)),
        name="dispatch",
    )(zstart, dest_flat, h2s)


def _experts_body(ge_ref, gn_ref, nu_ref, xs_ref, wgu_ref, wd_ref, bgu_ref, bd_ref, o_ref,
                  xb_ref, wgu_bf_ref, wd_bf_ref, gu_even_ref, gu_odd_ref, act_ref):
    g = pl.program_id(0)
    f = pl.program_id(1)
    nsub = gn_ref[g]
    half = TF_E // 2
    gu_bufs = (gu_even_ref, gu_odd_ref)

    @pl.when((nsub == 0) & (f == 0))
    def _unused_group():
        o_ref[...] = jnp.zeros_like(o_ref)

    def load_rows():
        for r in range(PACK_ROWS):
            w = xs_ref[pl.ds(r, GROUP_E, stride=PACK_ROWS), :]
            lo = pltpu.bitcast(lax.shift_left(w, U32(16)), F32)
            hi = pltpu.bitcast(w & U32(0xFFFF0000), F32)
            xb_ref[:, (2 * r) * LANES:(2 * r + 1) * LANES] = lo.astype(BF16)
            xb_ref[:, (2 * r + 1) * LANES:(2 * r + 2) * LANES] = hi.astype(BF16)

    def convert_weights():
        wgu_bf_ref[...] = wgu_ref[...].astype(BF16)
        lo = _bf16_bits(wd_ref[0:half, :])
        hi = _bf16_bits(wd_ref[half:TF_E, :])
        chunk_rows = pl.ds(pl.multiple_of(f * TF_E, TF_E), TF_E)
        wd_bf_ref[chunk_rows, :] = pltpu.bitcast(lax.shift_right_logical(lo, U32(16)) | hi, BF16)

    def up_matmul(rows, parity):
        gu_bufs[parity][rows, :] = (jnp.dot(xb_ref[rows, :], wgu_bf_ref[...], preferred_element_type=F32)
                                    + bgu_ref[...])

    def activation(rows, parity):
        gu = gu_bufs[1 - parity][rows, :]
        gate = jnp.minimum(gu, SWIGLU_LIMIT)
        glu = gate * (1.0 / (1.0 + jnp.exp(-gate * SWIGLU_ALPHA)))
        up1 = jnp.clip(gu, -SWIGLU_LIMIT, SWIGLU_LIMIT) + 1.0
        prod = glu * pltpu.roll(up1, 2 * TF_E - 1, 1)
        even = (lax.broadcasted_iota(jnp.int32, (1, TF_E), 1) & 1) == 0
        act_ref[f - 1, rows, :] = jnp.where(even, prod[:, :TF_E], pltpu.roll(prod[:, TF_E:], 1, 1)).astype(BF16)

    def down_and_store(first_row, n_rows):
        rows = pl.ds(first_row, n_rows)
        act = jnp.concatenate([act_ref[c, rows, :] for c in range(NF_E)], axis=1)
        y = jnp.dot(act, wd_bf_ref[...], preferred_element_type=F32) + bd_ref[...]
        _pack_rows(y, o_ref, first_row, n_rows)

    def over_rows(parity, first, last):
        @pl.when(nsub > NSUB_E // 2)
        def _mostly_full():
            rows = pl.ds(0, GROUP_E)
            if first:
                load_rows()
            if not last:
                convert_weights()
                up_matmul(rows, parity)
            if not first:
                activation(rows, parity)
            if last:
                for s in range(GROUP_E // DOWN_E):
                    down_and_store(s * DOWN_E, DOWN_E)

        @pl.when(nsub <= NSUB_E // 2)
        def _few_rows():
            if first:
                load_rows()
            if not last:
                convert_weights()
            if last:
                o_ref[...] = jnp.zeros_like(o_ref)

            def body(s, c):
                first_row = pl.multiple_of(s * SUB_E, SUB_E)
                rows = pl.ds(first_row, SUB_E)
                if not last:
                    up_matmul(rows, parity)
                if not first:
                    activation(rows, parity)
                if last:
                    down_and_store(first_row, SUB_E)
                return c
            lax.fori_loop(0, nsub, body, 0)

    @pl.when(nsub > 0)
    def _group():
        @pl.when(f == 0)
        def _first_chunk():
            over_rows(0, True, False)

        for parity in range(2):
            @pl.when((f > 0) & (f < NF_E) & (f % 2 == parity))
            def _middle_chunk():
                over_rows(parity, False, False)

        @pl.when(f == NF_E)
        def _drain():
            over_rows(NF_E % 2, False, True)


def _experts(group_e, group_nsub, n_used, xs, w_gu, w_dn, b_gu, b_dn, layer, n_slots):
    ngrp = n_slots // GROUP_E
    nf = NF_E

    def fchunk(g, f, nu):
        return jnp.where(g < nu[0], jnp.clip(f, 0, nf - 1), nf - 1)

    return pl.pallas_call(
        _experts_body,
        grid_spec=pltpu.PrefetchScalarGridSpec(
            num_scalar_prefetch=3,
            grid=(ngrp, nf + 1),
            in_specs=[
                pl.BlockSpec((GROUP_E * PACK_ROWS, LANES), lambda g, f, ge, gn, nu: (jnp.minimum(g, nu[0] - 1), 0)),
                pl.BlockSpec((None, None, D_MODEL, 2 * TF_E),
                             lambda g, f, ge, gn, nu: (layer, ge[g], 0, fchunk(g, f, nu))),
                pl.BlockSpec((None, None, TF_E, D_MODEL),
                             lambda g, f, ge, gn, nu: (layer, ge[g], fchunk(g, f, nu), 0)),
                pl.BlockSpec((None, None, 1, 2 * TF_E),
                             lambda g, f, ge, gn, nu: (layer, ge[g], 0, fchunk(g, f, nu))),
                pl.BlockSpec((None, None, 1, D_MODEL), lambda g, f, ge, gn, nu: (layer, ge[g], 0, 0)),
            ],
            out_specs=pl.BlockSpec((GROUP_E * PACK_ROWS, LANES), lambda g, f, ge, gn, nu: (g, 0)),
            scratch_shapes=[
                pltpu.VMEM((GROUP_E, D_MODEL), BF16),
                pltpu.VMEM((D_MODEL, 2 * TF_E), BF16),
                pltpu.VMEM((D_FF, D_MODEL), BF16),
                pltpu.VMEM((GROUP_E, 2 * TF_E), F32),
                pltpu.VMEM((GROUP_E, 2 * TF_E), F32),
                pltpu.VMEM((NF_E, GROUP_E, TF_E), BF16),
            ],
        ),
        out_shape=jax.ShapeDtypeStruct((n_slots * PACK_ROWS, LANES), U32),
        compiler_params=_cparams(("arbitrary", "arbitrary")),
        name="experts",
    )(group_e, group_nsub, n_used, xs, w_gu, w_dn, b_gu, b_dn)


def _combine_body(dest_ref, dnext_ref, ys_ref, w_ref, x1_ref, g_ref, o_ref, buf_ref, sems):
    i = pl.program_id(0)
    n = pl.num_programs(0)
    tm = x1_ref.shape[0]
    buf_rows = tm * TOP_K
    slot = i % 2

    def issue_token(d_ref, b, t):
        for k in range(TOP_K):
            src = _row_slice(ys_ref, d_ref[t * TOP_K + k])
            dst = _row_slice(buf_ref, b * buf_rows + k * tm + t)
            pltpu.make_async_copy(src, dst, sems.at[b]).start(priority=k % 2)

    def wait_buffer(b):
        dst = _row_slice(buf_ref, b * buf_rows, buf_rows)
        pltpu.make_async_copy(_row_slice(ys_ref, 0, buf_rows), dst, sems.at[b]).wait()

    @pl.when(i == 0)
    def _first_tile():
        def body(t, c):
            issue_token(dest_ref, 0, t)
            return c
        lax.fori_loop(0, tm, body, 0)

    base = slot * buf_rows
    wait_buffer(slot)

    w = w_ref[...]
    wk = [jnp.broadcast_to(w[:, k:k + 1], (tm, LANES)) for k in range(TOP_K)]
    ssq = jnp.zeros((tm, 1), F32)
    ahead = tm // PACK_ROWS
    for r in range(PACK_ROWS):
        for t in range(r * ahead, (r + 1) * ahead):
            issue_token(dnext_ref, 1 - slot, t)
        acc_lo = acc_hi = None
        for k in range(TOP_K):
            word = buf_ref[pl.ds((base + k * tm) * PACK_ROWS + r, tm, stride=PACK_ROWS), :]
            lo = pltpu.bitcast(lax.shift_left(word, U32(16)), F32) * wk[k]
            hi = pltpu.bitcast(word & U32(0xFFFF0000), F32) * wk[k]
            acc_lo = lo if acc_lo is None else acc_lo + lo
            acc_hi = hi if acc_hi is None else acc_hi + hi
        for c, moe in ((2 * r, acc_lo), (2 * r + 1, acc_hi)):
            cols = slice(c * LANES, (c + 1) * LANES)
            v = x1_ref[:, cols] + moe
            o_ref[:, cols] = v
            ssq = ssq + jnp.sum(v * v, axis=-1, keepdims=True)
    inv = lax.rsqrt(ssq * np.float32(1.0 / D_MODEL) + RMS_EPS)
    o_ref[...] = o_ref[...] * inv * g_ref[...]

    @pl.when(i == n - 1)
    def _drain_extra_gather():
        wait_buffer(1 - slot)


def _combine(dest_flat, ys, top_w, x1, g):
    t = x1.shape[0]
    tm = TM_COMB
    nt = t // tm
    return pl.pallas_call(
        _combine_body,
        grid=(nt,),
        in_specs=[
            pl.BlockSpec((tm * TOP_K,), lambda i: (i,), memory_space=pltpu.SMEM),
            pl.BlockSpec((tm * TOP_K,), lambda i: (jnp.minimum(i + 1, nt - 1),), memory_space=pltpu.SMEM),
            pl.BlockSpec(memory_space=pl.ANY),
            pl.BlockSpec((tm, TOP_K), lambda i: (i, 0)),
            pl.BlockSpec((tm, D_MODEL), lambda i: (i, 0)),
            pl.BlockSpec((1, D_MODEL), lambda i: (0, 0)),
        ],
        out_specs=pl.BlockSpec((tm, D_MODEL), lambda i: (i, 0)),
        out_shape=jax.ShapeDtypeStruct((t, D_MODEL), F32),
        scratch_shapes=[pltpu.VMEM((2 * tm * TOP_K * PACK_ROWS, LANES), U32), pltpu.SemaphoreType.DMA((2,))],
        compiler_params=_cparams(("arbitrary",)),
        name="combine",
    )(dest_flat, dest_flat, ys, top_w, x1, g)


def _routing_tables(counts, top_e, top_pos, n_slots):
    i32 = jnp.int32
    ngrp = n_slots // GROUP_E
    experts = jnp.arange(N_EXPERTS, dtype=i32)
    padded = (counts + GROUP_E - 1) // GROUP_E * GROUP_E
    ends = jnp.cumsum(padded)
    starts = ends - padded
    total = ends[-1]
    dest = jnp.sum(jnp.where(top_e[..., None] == experts, starts, 0), axis=-1) + top_pos
    gstart = jnp.arange(ngrp, dtype=i32) * GROUP_E
    used = gstart < total
    of_group = gstart[:, None] >= ends[None, :]
    last_e = jnp.max(jnp.where(padded > 0, experts, 0))
    group_e = jnp.where(used, jnp.minimum(jnp.sum(of_group, axis=1), N_EXPERTS - 1), last_e)
    valid_end = jnp.sum(jnp.where(group_e[:, None] == experts, starts + counts, 0), axis=-1)
    rows = jnp.clip(valid_end - gstart, 0, GROUP_E)
    group_nsub = jnp.where(used, (rows + SUB_E - 1) // SUB_E, 0)
    n_used = (total // GROUP_E).reshape(1)
    sub = jnp.arange(NSUB_E, dtype=i32) * SUB_E
    tail = (starts + counts // SUB_E * SUB_E)[:, None] + sub
    tail = jnp.where(tail < ends[:, None], tail, -1)
    spare = (n_used[0] + experts)[:, None] * GROUP_E + sub
    spare = jnp.where(spare < n_slots, spare, -1)
    zstart = jnp.concatenate([tail.reshape(-1), spare.reshape(-1)])
    return (dest.reshape(-1).astype(i32), group_e.astype(i32), group_nsub.astype(i32), n_used.astype(i32),
            zstart.astype(i32))


def kernel(x, mem, rel_bias_table, norm1_g, w_in, attn_sink, w_mem_kv, w_fourier_out, w_attn_out, w_mem_out,
           w_o, norm2_g, w_router, b_router, w_gate_up, b_gate_up, w_down, b_down, final_g):
    batch, seq, _ = x.shape
    t = batch * seq
    depth = norm1_g.shape[0]
    assert depth == 1, "the combine kernel applies the final RMSNorm, so a single layer is supported"
    x2d = x.reshape(t, D_MODEL)
    mem2d = mem.reshape(batch * MEM_LEN, D_MODEL)
    bias = _band_bias(rel_bias_table)
    n_slots = t * TOP_K + N_EXPERTS * GROUP_E

    for l in range(depth):
        wi = w_in[l].astype(BF16)
        wi = jnp.concatenate([wi[:, _OFF_GATE:], wi[:, :_OFF_Q], wi[:, _OFF_MQ:_OFF_GATE],
                              wi[:, _OFF_Q:_OFF_K], wi[:, _OFF_K:_OFF_V], wi[:, _OFF_V:_OFF_MQ]], axis=1)
        proj = _inproj(x2d, norm1_g[l].reshape(1, D_MODEL), wi)
        y_f = _fourier(proj, w_fourier_out[l].astype(BF16), batch, seq)
        att = _swa(proj, attn_sink[l], bias, batch, seq)
        kv = _memkv(mem2d, w_mem_kv[l].astype(BF16))
        mo = _memattn(proj, kv, batch, seq)
        mixed = _branches(y_f, att, mo, proj, w_attn_out[l].astype(BF16), w_mem_out[l].astype(BF16))
        x1, h2s, top_e, top_pos, top_w, counts = _merge(
            mixed, x2d, w_o[l].astype(BF16),
            norm2_g[l].reshape(1, D_MODEL), w_router[l], b_router[l].reshape(1, N_EXPERTS))
        dest, group_e, group_nsub, n_used, zstart = _routing_tables(counts[0], top_e, top_pos, n_slots)
        xs = _dispatch(zstart, dest, h2s, n_slots)
        ys = _experts(group_e, group_nsub, n_used, xs, w_gate_up, w_down,
                      b_gate_up.reshape(depth, N_EXPERTS, 1, 2 * D_FF), b_down.reshape(depth, N_EXPERTS, 1, D_MODEL),
                      l, n_slots)
        x2d = _combine(dest, ys, top_w, x1, final_g.reshape(1, D_MODEL))
    return x2d.reshape(batch, seq, D_MODEL)
```

```python
import functools

import numpy as np
import jax
import jax.numpy as jnp
from jax import lax
from jax.experimental import pallas as pl
from jax.experimental.pallas import tpu as pltpu

D_MODEL = 2048
MEM_LEN = 256
FOURIER_GROUPS = 4
FOURIER_GROUP_DIM = 256
FOURIER_WIDTH = 1024
N_HEADS = 32
N_KV_HEADS = 4
HEAD_DIM = 64
ATTN_WIDTH = 2048
KV_WIDTH = 256
WINDOW = 128
ATT_BLOCK = 128
MEM_HEADS = 4
MEM_HEAD_DIM = 256
MEM_WIDTH = 1024
N_BRANCHES = 3
NUM_BUCKETS = 32
MAX_DISTANCE = 128
N_EXPERTS = 32
TOP_K = 4
D_FF = 2048
SWIGLU_LIMIT = 7.0
SWIGLU_ALPHA = 1.702
RMS_EPS = 1e-5
NEG_INF = -1e30

GATE_WIDTH = N_BRANCHES * D_MODEL
COL_GATE = 0
COL_F = GATE_WIDTH
COL_MQ = COL_F + FOURIER_WIDTH
COL_Q = COL_MQ + MEM_WIDTH
COL_K = COL_Q + ATTN_WIDTH
COL_V = COL_K + KV_WIDTH
IN_WIDTH = COL_V + KV_WIDTH

_OFF_Q = FOURIER_WIDTH
_OFF_K = _OFF_Q + ATTN_WIDTH
_OFF_V = _OFF_K + KV_WIDTH
_OFF_MQ = _OFF_V + KV_WIDTH
_OFF_GATE = _OFF_MQ + MEM_WIDTH

LANES = 128
PACK_ROWS = D_MODEL // (2 * LANES)
VMEM_CAPACITY_V7X = 64 * 1024 * 1024
VMEM_LIMIT = VMEM_CAPACITY_V7X // 8 * 7

TM_IN, TN_IN = 512, 1536
GATE_SLICES = 3
TS_F = 512
TQ_MEM = 512
TM_MERGE = 512
TM_DISP = 256
DISP_UNROLL = 8
GROUP_E, SUB_E, TF_E = 1024, 128, 256
NSUB_E = GROUP_E // SUB_E
NF_E = D_FF // TF_E
DOWN_E = 512
TM_COMB = 256

BF16 = jnp.bfloat16
F32 = jnp.float32
U32 = jnp.uint32


def _cparams(sem):
    return pltpu.CompilerParams(dimension_semantics=sem, vmem_limit_bytes=VMEM_LIMIT)


def _bf16_bits(v):
    return pltpu.bitcast(v.astype(BF16).astype(F32), U32)


def _pack_rows(v, out_ref, first_row, rows):
    for r in range(PACK_ROWS):
        lo = _bf16_bits(v[:, (2 * r) * LANES:(2 * r + 1) * LANES])
        hi = _bf16_bits(v[:, (2 * r + 1) * LANES:(2 * r + 2) * LANES])
        out_ref[pl.ds(first_row * PACK_ROWS + r, rows, stride=PACK_ROWS), :] = lax.shift_right_logical(lo, U32(16)) | hi


def _unpack_rows(ref, first_row, rows):
    chunks = []
    for r in range(PACK_ROWS):
        w = ref[pl.ds(first_row * PACK_ROWS + r, rows, stride=PACK_ROWS), :]
        chunks.append(pltpu.bitcast(lax.shift_left(w, U32(16)), F32))
        chunks.append(pltpu.bitcast(w & U32(0xFFFF0000), F32))
    return chunks


def _inproj_body(x_ref, g_ref, w_ref, o_ref, h_ref, *, n_gate_tiles):
    j = pl.program_id(1)

    @pl.when(j == 0)
    def _norm():
        x = x_ref[...]
        inv = lax.rsqrt(jnp.mean(x * x, axis=-1, keepdims=True) + RMS_EPS)
        h_ref[...] = (x * inv * g_ref[...]).astype(BF16)

    @pl.when(j < n_gate_tiles)
    def _gate():
        width = w_ref.shape[1] // GATE_SLICES
        for c in range(GATE_SLICES):
            cols = slice(c * width, (c + 1) * width)
            acc = jnp.dot(h_ref[...], w_ref[:, cols], preferred_element_type=F32)
            o_ref[:, cols] = (1.0 / (1.0 + jnp.exp(-acc))).astype(o_ref.dtype)

    @pl.when(j >= n_gate_tiles)
    def _plain():
        o_ref[...] = jnp.dot(h_ref[...], w_ref[...], preferred_element_type=F32).astype(o_ref.dtype)


def _inproj(x2d, g, w):
    t = x2d.shape[0]
    return pl.pallas_call(
        functools.partial(_inproj_body, n_gate_tiles=GATE_WIDTH // TN_IN),
        grid=(t // TM_IN, IN_WIDTH // TN_IN),
        in_specs=[
            pl.BlockSpec((TM_IN, D_MODEL), lambda i, j: (i, 0)),
            pl.BlockSpec((1, D_MODEL), lambda i, j: (0, 0)),
            pl.BlockSpec((D_MODEL, TN_IN), lambda i, j: (0, j)),
        ],
        out_specs=pl.BlockSpec((TM_IN, TN_IN), lambda i, j: (i, j)),
        out_shape=jax.ShapeDtypeStruct((t, IN_WIDTH), BF16),
        scratch_shapes=[pltpu.VMEM((TM_IN, D_MODEL), BF16)],
        compiler_params=_cparams(("parallel", "arbitrary")),
        name="inproj",
    )(x2d, g, w)


def _fourier_body(f_ref, cc_ref, sc_ref, cs_ref, wfo_ref, o_ref, z_ref, *, seq):
    st = pl.program_id(1)

    @pl.when(st == 0)
    def _channel_dft():
        for g in range(FOURIER_GROUPS):
            cols = slice(g * FOURIER_GROUP_DIM, (g + 1) * FOURIER_GROUP_DIM)
            xg = f_ref[:, cols]
            z_ref[0:seq, cols] = jnp.dot(xg, cc_ref[...], preferred_element_type=F32).astype(BF16)
            z_ref[seq:2 * seq, cols] = jnp.dot(xg, sc_ref[...], preferred_element_type=F32).astype(BF16)

    y = jnp.dot(cs_ref[...], z_ref[...], preferred_element_type=F32)
    o_ref[...] = jnp.dot(y.astype(BF16), wfo_ref[...], preferred_element_type=F32).astype(o_ref.dtype)


def _dft_tables(seq):
    c = FOURIER_GROUP_DIM
    kc = (np.arange(c)[:, None] * np.arange(c)[None, :]) % c
    ac = 2.0 * np.pi * kc / c
    ks = (np.arange(seq)[:, None] * np.arange(seq)[None, :]) % seq
    a_s = 2.0 * np.pi * ks / seq
    cc = (np.cos(ac) / np.sqrt(c)).astype(np.float32)
    sc = (np.sin(ac) / np.sqrt(c)).astype(np.float32)
    cs = np.concatenate([np.cos(a_s), -np.sin(a_s)], axis=1) / np.sqrt(seq)
    return cc, sc, cs.astype(np.float32)


def _fourier(proj, w_fo, batch, seq):
    cc, sc, cs = _dft_tables(seq)
    cc = jnp.asarray(cc).astype(BF16)
    sc = jnp.asarray(sc).astype(BF16)
    cs = jnp.asarray(cs).astype(BF16)
    t = batch * seq
    return pl.pallas_call(
        functools.partial(_fourier_body, seq=seq),
        grid=(batch, seq // TS_F),
        in_specs=[
            pl.BlockSpec((seq, FOURIER_WIDTH), lambda b, s: (b, COL_F // FOURIER_WIDTH)),
            pl.BlockSpec((FOURIER_GROUP_DIM, FOURIER_GROUP_DIM), lambda b, s: (0, 0)),
            pl.BlockSpec((FOURIER_GROUP_DIM, FOURIER_GROUP_DIM), lambda b, s: (0, 0)),
            pl.BlockSpec((TS_F, 2 * seq), lambda b, s: (s, 0)),
            pl.BlockSpec((FOURIER_WIDTH, D_MODEL), lambda b, s: (0, 0)),
        ],
        out_specs=pl.BlockSpec((TS_F, D_MODEL), lambda b, s: (b * (seq // TS_F) + s, 0)),
        out_shape=jax.ShapeDtypeStruct((t, D_MODEL), BF16),
        scratch_shapes=[pltpu.VMEM((2 * seq, FOURIER_WIDTH), BF16)],
        compiler_params=_cparams(("parallel", "arbitrary")),
        name="fourier",
    )(proj, cc, sc, cs, w_fo)


def _swa_body(sink_ref, q_ref, k_ref, v_ref, bias_ref, o_ref):
    n = pl.program_id(1)
    nb = pl.num_programs(1)
    prev = pl.multiple_of(jnp.maximum(n - 1, 0) * ATT_BLOCK, ATT_BLOCK)
    own = pl.multiple_of(n * ATT_BLOCK, ATT_BLOCK)
    nxt = pl.multiple_of(jnp.minimum(n + 1, nb - 1) * ATT_BLOCK, ATT_BLOCK)

    def band(ref):
        return jnp.concatenate(
            [ref[pl.ds(prev, ATT_BLOCK), :], ref[pl.ds(own, ATT_BLOCK), :], ref[pl.ds(nxt, ATT_BLOCK), :]], axis=0)

    kb = band(k_ref).astype(F32)
    vb = band(v_ref).astype(F32)
    key = lax.broadcasted_iota(jnp.int32, (3 * ATT_BLOCK, 1), 0)
    valid = ((key >= ATT_BLOCK) | (n > 0)) & ((key < 2 * ATT_BLOCK) | (n < nb - 1))
    low = lax.broadcasted_iota(jnp.int32, (1, LANES), 1) < HEAD_DIM
    low_row = lax.broadcasted_iota(jnp.int32, (LANES, 1), 0) < HEAD_DIM

    def halves(x, c, kv_is_low):
        a = x[:, c * LANES:(c + 1) * LANES]
        r = pltpu.roll(a, HEAD_DIM, 1)
        in_low, in_high = (a, r) if kv_is_low else (r, a)
        return jnp.where(low, in_low, 0.0), jnp.where(low, 0.0, in_high)

    pairs = N_HEADS // N_KV_HEADS // 2
    for g in range(N_KV_HEADS):
        k_par = [h.astype(BF16) for h in halves(kb, g // 2, g % 2 == 0)]
        vt_par = [h.T.astype(BF16) for h in halves(vb, g // 2, g % 2 == 0)]
        base = g * pairs * LANES
        qp = jnp.concatenate([q_ref[:, base + p * LANES:base + (p + 1) * LANES] for p in range(pairs)], axis=0)
        qp = qp * jnp.asarray(HEAD_DIM ** -0.5, BF16)
        probs, inv_den = [], []
        for e in range(2):
            s = lax.dot_general(k_par[e], qp, (((1,), (1,)), ((), ())), preferred_element_type=F32)
            s = jnp.where(valid, s + bias_ref[g, e], NEG_INF)
            sink = jnp.concatenate(
                [jnp.full((1, ATT_BLOCK), sink_ref[(g * pairs + p) * 2 + e], F32) for p in range(pairs)], axis=1)
            m = jnp.maximum(jnp.max(s, axis=0, keepdims=True), sink)
            p_un = jnp.exp(s - m)
            den = jnp.sum(p_un, axis=0, keepdims=True) + jnp.exp(sink - m)
            probs.append(p_un.astype(BF16))
            inv_den.append(1.0 / den)
        o_t = (jnp.dot(vt_par[0], probs[0], preferred_element_type=F32)
               + jnp.dot(vt_par[1], probs[1], preferred_element_type=F32))
        o = (o_t * jnp.where(low_row, inv_den[0], inv_den[1])).T
        for p in range(pairs):
            o_ref[:, base + p * LANES:base + (p + 1) * LANES] = o[p * ATT_BLOCK:(p + 1) * ATT_BLOCK].astype(o_ref.dtype)


def _t5_bucket(rel):
    nb = NUM_BUCKETS // 2
    max_exact = nb // 2
    ret = (rel > 0).astype(jnp.int32) * nb
    n = jnp.abs(rel)
    nf = jnp.maximum(n, 1).astype(jnp.float32)
    large = max_exact + (jnp.log(nf / max_exact) / np.float32(np.log(MAX_DISTANCE / max_exact))
                         * (nb - max_exact)).astype(jnp.int32)
    large = jnp.minimum(large, nb - 1)
    return ret + jnp.where(n < max_exact, n, large)


def _band_bias(rel_table):
    q_off = jnp.arange(ATT_BLOCK)[:, None]
    k_off = jnp.arange(3 * ATT_BLOCK)[None, :]
    rel = k_off - ATT_BLOCK - q_off
    hot = (_t5_bucket(rel)[..., None] == jnp.arange(NUM_BUCKETS)).astype(F32)
    bias = jnp.einsum("qkb,bh->hqk", hot, rel_table.astype(F32), precision=lax.Precision.HIGHEST)
    bias = jnp.where(jnp.abs(rel) <= WINDOW, bias, NEG_INF)
    pairs = N_HEADS // N_KV_HEADS // 2
    bias = bias.reshape(N_KV_HEADS, pairs, 2, ATT_BLOCK, 3 * ATT_BLOCK).transpose(0, 2, 4, 1, 3)
    return bias.reshape(N_KV_HEADS, 2, 3 * ATT_BLOCK, pairs * ATT_BLOCK)


def _swa(proj, sink, bias, batch, seq):
    nb = seq // ATT_BLOCK
    t = batch * seq
    return pl.pallas_call(
        _swa_body,
        grid_spec=pltpu.PrefetchScalarGridSpec(
            num_scalar_prefetch=0,
            grid=(batch, nb),
            in_specs=[
                pl.BlockSpec(memory_space=pltpu.SMEM),
                pl.BlockSpec((ATT_BLOCK, ATTN_WIDTH), lambda b, n: (b * nb + n, COL_Q // ATTN_WIDTH)),
                pl.BlockSpec((seq, KV_WIDTH), lambda b, n: (b, COL_K // KV_WIDTH)),
                pl.BlockSpec((seq, KV_WIDTH), lambda b, n: (b, COL_V // KV_WIDTH)),
                pl.BlockSpec(bias.shape, lambda b, n: (0, 0, 0, 0)),
            ],
            out_specs=pl.BlockSpec((ATT_BLOCK, ATTN_WIDTH), lambda b, n: (b * nb + n, 0)),
        ),
        out_shape=jax.ShapeDtypeStruct((t, ATTN_WIDTH), BF16),
        compiler_params=_cparams(("parallel", "arbitrary")),
        name="swa",
    )(sink, proj, proj, proj, bias)


def _memkv_body(m_ref, w_ref, o_ref):
    o_ref[...] = jnp.dot(m_ref[...].astype(BF16), w_ref[...], preferred_element_type=F32).astype(o_ref.dtype)


def _memkv(mem2d, w):
    rows = mem2d.shape[0]
    tm, tn = 512, 1024
    return pl.pallas_call(
        _memkv_body,
        grid=(rows // tm, (2 * MEM_WIDTH) // tn),
        in_specs=[
            pl.BlockSpec((tm, D_MODEL), lambda i, j: (i, 0)),
            pl.BlockSpec((D_MODEL, tn), lambda i, j: (0, j)),
        ],
        out_specs=pl.BlockSpec((tm, tn), lambda i, j: (i, j)),
        out_shape=jax.ShapeDtypeStruct((rows, 2 * MEM_WIDTH), BF16),
        compiler_params=_cparams(("parallel", "arbitrary")),
        name="memkv",
    )(mem2d, w)


def _memattn_body(q_ref, k_ref, v_ref, o_ref):
    outs = []
    for h in range(MEM_HEADS):
        cols = slice(h * MEM_HEAD_DIM, (h + 1) * MEM_HEAD_DIM)
        s = lax.dot_general(q_ref[:, cols], k_ref[:, cols], (((1,), (1,)), ((), ())), preferred_element_type=F32)
        s = s * np.float32(MEM_HEAD_DIM ** -0.5)
        m = jnp.max(s, axis=-1, keepdims=True)
        p = jnp.exp(s - m)
        den = jnp.sum(p, axis=-1, keepdims=True)
        o = jnp.dot(p.astype(BF16), v_ref[:, cols], preferred_element_type=F32)
        outs.append(o / den)
    o_ref[...] = jnp.concatenate(outs, axis=1).astype(o_ref.dtype)


def _memattn(proj, kv, batch, seq):
    t = batch * seq
    nq = seq // TQ_MEM
    return pl.pallas_call(
        _memattn_body,
        grid=(batch, nq),
        in_specs=[
            pl.BlockSpec((TQ_MEM, MEM_WIDTH), lambda b, i: (b * nq + i, COL_MQ // MEM_WIDTH)),
            pl.BlockSpec((MEM_LEN, MEM_WIDTH), lambda b, i: (b, 0)),
            pl.BlockSpec((MEM_LEN, MEM_WIDTH), lambda b, i: (b, 1)),
        ],
        out_specs=pl.BlockSpec((TQ_MEM, MEM_WIDTH), lambda b, i: (b * nq + i, 0)),
        out_shape=jax.ShapeDtypeStruct((t, MEM_WIDTH), BF16),
        compiler_params=_cparams(("parallel", "arbitrary")),
        name="memattn",
    )(proj, kv, kv)


def _branches_body(yf_ref, att_ref, mo_ref, g0_ref, g1_ref, g2_ref, wao_ref, wmo_ref, o_ref):
    y_a = jnp.dot(att_ref[...], wao_ref[...], preferred_element_type=F32)
    y_m = jnp.dot(mo_ref[...], wmo_ref[...], preferred_element_type=F32)
    mixed = (g0_ref[...].astype(F32) * yf_ref[...].astype(F32)
             + g1_ref[...].astype(F32) * y_a + g2_ref[...].astype(F32) * y_m)
    o_ref[...] = mixed.astype(o_ref.dtype)


def _branches(y_f, att, mo, proj, w_ao, w_mo):
    t = y_f.shape[0]
    tm = TM_MERGE
    row = lambda i: (i, 0)
    const = lambda i: (0, 0)
    resident = dict(pipeline_mode=pl.Buffered(1))
    return pl.pallas_call(
        _branches_body,
        grid=(t // tm,),
        in_specs=[
            pl.BlockSpec((tm, D_MODEL), row),
            pl.BlockSpec((tm, ATTN_WIDTH), row),
            pl.BlockSpec((tm, MEM_WIDTH), row),
            pl.BlockSpec((tm, D_MODEL), lambda i: (i, 0)),
            pl.BlockSpec((tm, D_MODEL), lambda i: (i, 1)),
            pl.BlockSpec((tm, D_MODEL), lambda i: (i, 2)),
            pl.BlockSpec((ATTN_WIDTH, D_MODEL), const, **resident),
            pl.BlockSpec((MEM_WIDTH, D_MODEL), const, **resident),
        ],
        out_specs=pl.BlockSpec((tm, D_MODEL), row),
        out_shape=jax.ShapeDtypeStruct((t, D_MODEL), BF16),
        compiler_params=_cparams(("parallel",)),
        name="branches",
    )(y_f, att, mo, proj, proj, proj, w_ao, w_mo)


def _merge_body(mixed_ref, x_ref, wo_ref, n2_ref, wr_ref, br_ref,
                x1_ref, h2s_ref, te_ref, tp_ref, tw_ref, cnt_ref, carry_ref):
    i = pl.program_id(0)
    tm = x_ref.shape[0]

    @pl.when(i == 0)
    def _init():
        carry_ref[...] = jnp.zeros_like(carry_ref)

    x1 = x_ref[...] + jnp.dot(mixed_ref[...], wo_ref[...], preferred_element_type=F32)
    x1_ref[...] = x1

    inv = lax.rsqrt(jnp.mean(x1 * x1, axis=-1, keepdims=True) + RMS_EPS)
    h2 = x1 * inv * n2_ref[...]
    _pack_rows(h2, h2s_ref, 0, tm)

    h_hi = h2.astype(BF16)
    h_lo = (h2 - h_hi.astype(F32)).astype(BF16)
    w_r = wr_ref[...]
    w_hi = w_r.astype(BF16)
    w_lo = (w_r - w_hi.astype(F32)).astype(BF16)
    part = jnp.dot(h_hi, jnp.concatenate([w_hi, w_lo], axis=1), preferred_element_type=F32)
    logits = (part[:, :N_EXPERTS] + part[:, N_EXPERTS:]
              + jnp.dot(h_lo, w_hi, preferred_element_type=F32) + br_ref[...])
    lane = lax.broadcasted_iota(jnp.int32, (tm, N_EXPERTS), 1)
    vals, idxs, hots = [], [], []
    rest = logits
    for _ in range(TOP_K):
        mk = jnp.max(rest, axis=-1, keepdims=True)
        ik = jnp.min(jnp.where(rest == mk, lane, N_EXPERTS), axis=-1, keepdims=True)
        hot = lane == ik
        vals.append(mk)
        idxs.append(ik)
        hots.append(hot)
        rest = jnp.where(hot, -jnp.inf, rest)
    exps = [jnp.exp(v - vals[0]) for v in vals]
    den = exps[0] + exps[1] + exps[2] + exps[3]

    sel = (hots[0] | hots[1] | hots[2] | hots[3]).astype(F32)
    r_i = lax.broadcasted_iota(jnp.int32, (tm, tm), 0)
    c_i = lax.broadcasted_iota(jnp.int32, (tm, tm), 1)
    tri = (c_i < r_i).astype(BF16)
    pos = jnp.dot(tri, sel.astype(BF16), preferred_element_type=F32) + carry_ref[...]
    poss = [jnp.sum(jnp.where(hot, pos, 0.0), axis=-1, keepdims=True) for hot in hots]
    carry = carry_ref[...] + jnp.sum(sel, axis=0, keepdims=True)
    carry_ref[...] = carry

    te_ref[...] = jnp.concatenate(idxs, axis=1)
    tp_ref[...] = jnp.concatenate(poss, axis=1).astype(jnp.int32)
    tw_ref[...] = jnp.concatenate([e / den for e in exps], axis=1)
    cnt_ref[...] = carry.astype(jnp.int32)


def _merge(mixed, x2d, w_o, n2, w_r, b_r):
    t = x2d.shape[0]
    tm = TM_MERGE
    row = lambda i: (i, 0)
    const = lambda i: (0, 0)
    resident = dict(pipeline_mode=pl.Buffered(1))
    return pl.pallas_call(
        _merge_body,
        grid=(t // tm,),
        in_specs=[
            pl.BlockSpec((tm, D_MODEL), row),
            pl.BlockSpec((tm, D_MODEL), row),
            pl.BlockSpec((D_MODEL, D_MODEL), const, **resident),
            pl.BlockSpec((1, D_MODEL), const),
            pl.BlockSpec((D_MODEL, N_EXPERTS), const),
            pl.BlockSpec((1, N_EXPERTS), const),
        ],
        out_specs=[
            pl.BlockSpec((tm, D_MODEL), row),
            pl.BlockSpec((tm * PACK_ROWS, LANES), row),
            pl.BlockSpec((tm, TOP_K), row),
            pl.BlockSpec((tm, TOP_K), row),
            pl.BlockSpec((tm, TOP_K), row),
            pl.BlockSpec((1, N_EXPERTS), const),
        ],
        out_shape=[
            jax.ShapeDtypeStruct((t, D_MODEL), F32),
            jax.ShapeDtypeStruct((t * PACK_ROWS, LANES), jnp.uint32),
            jax.ShapeDtypeStruct((t, TOP_K), jnp.int32),
            jax.ShapeDtypeStruct((t, TOP_K), jnp.int32),
            jax.ShapeDtypeStruct((t, TOP_K), F32),
            jax.ShapeDtypeStruct((1, N_EXPERTS), jnp.int32),
        ],
        scratch_shapes=[pltpu.VMEM((1, N_EXPERTS), F32)],
        compiler_params=_cparams(("arbitrary",)),
        name="merge",
    )(mixed, x2d, w_o, n2, w_r, b_r)


def _row_slice(ref, row, n_rows=1):
    start = pl.multiple_of(row * PACK_ROWS, PACK_ROWS)
    return ref.at[pl.ds(start, n_rows * PACK_ROWS), :]


def _dispatch_body(zstart_ref, dest_ref, h_ref, xs_ref, zero_ref, zsem, sem):
    i = pl.program_id(0)
    tm = h_ref.shape[0] // PACK_ROWS

    @pl.when(i == 0)
    def _zero_tails():
        zero_ref[...] = jnp.zeros_like(zero_ref)

        def tail_copy(e):
            return pltpu.make_async_copy(zero_ref, _row_slice(xs_ref, zstart_ref[e], SUB_E), zsem)

        def start(e, c):
            @pl.when(zstart_ref[e] >= 0)
            def _():
                tail_copy(e).start()
            return c

        def wait(e, c):
            @pl.when(zstart_ref[e] >= 0)
            def _():
                tail_copy(e).wait()
            return c

        lax.fori_loop(0, zstart_ref.shape[0], start, 0)
        lax.fori_loop(0, zstart_ref.shape[0], wait, 0)

    def scatter(block, c):
        for u in range(DISP_UNROLL):
            t = block * DISP_UNROLL + u
            for k in range(TOP_K):
                d = dest_ref[t * TOP_K + k]
                pltpu.make_async_copy(_row_slice(h_ref, t), _row_slice(xs_ref, d), sem).start(priority=k % 2)
        return c

    lax.fori_loop(0, tm // DISP_UNROLL, scatter, 0)
    for _ in range(TOP_K):
        pltpu.make_async_copy(h_ref, _row_slice(xs_ref, 0, tm), sem).wait()


def _dispatch(zstart, dest_flat, h2s, n_slots):
    t = h2s.shape[0] // PACK_ROWS
    tm = TM_DISP
    return pl.pallas_call(
        _dispatch_body,
        grid_spec=pltpu.PrefetchScalarGridSpec(
            num_scalar_prefetch=1,
            grid=(t // tm,),
            in_specs=[
                pl.BlockSpec((tm * TOP_K,), lambda i, z: (i,), memory_space=pltpu.SMEM),
                pl.BlockSpec((tm * PACK_ROWS, LANES), lambda i, z: (i, 0)),
            ],
            out_specs=pl.BlockSpec(memory_space=pl.ANY),
            scratch_shapes=[
                pltpu.VMEM((SUB_E * PACK_ROWS, LANES), U32),
                pltpu.SemaphoreType.DMA(()),
                pltpu.SemaphoreType.DMA(()),
            ],
        ),
        out_shape=jax.ShapeDtypeStruct((n_slots * PACK_ROWS, LANES), U32),
        compiler_params=_cparams(("arbitrary",)),
        name="dispatch",
    )(zstart, dest_flat, h2s)


def _experts_body(ge_ref, gn_ref, nu_ref, xs_ref, wgu_ref, wd_ref, bgu_ref, bd_ref, o_ref,
                  xb_ref, wgu_bf_ref, wd_bf_ref, gu_even_ref, gu_odd_ref, act_ref):
    g = pl.program_id(0)
    f = pl.program_id(1)
    nsub = gn_ref[g]
    half = TF_E // 2
    gu_bufs = (gu_even_ref, gu_odd_ref)

    @pl.when((nsub == 0) & (f == 0))
    def _unused_group():
        o_ref[...] = jnp.zeros_like(o_ref)

    def load_rows():
        for r in range(PACK_ROWS):
            w = xs_ref[pl.ds(r, GROUP_E, stride=PACK_ROWS), :]
            lo = pltpu.bitcast(lax.shift_left(w, U32(16)), F32)
            hi = pltpu.bitcast(w & U32(0xFFFF0000), F32)
            xb_ref[:, (2 * r) * LANES:(2 * r + 1) * LANES] = lo.astype(BF16)
            xb_ref[:, (2 * r + 1) * LANES:(2 * r + 2) * LANES] = hi.astype(BF16)

    def convert_weights():
        wgu_bf_ref[...] = wgu_ref[...].astype(BF16)
        lo = _bf16_bits(wd_ref[0:half, :])
        hi = _bf16_bits(wd_ref[half:TF_E, :])
        chunk_rows = pl.ds(pl.multiple_of(f * TF_E, TF_E), TF_E)
        wd_bf_ref[chunk_rows, :] = pltpu.bitcast(lax.shift_right_logical(lo, U32(16)) | hi, BF16)

    def up_matmul(rows, parity):
        gu_bufs[parity][rows, :] = (jnp.dot(xb_ref[rows, :], wgu_bf_ref[...], preferred_element_type=F32)
                                    + bgu_ref[...])

    def activation(rows, parity):
        gu = gu_bufs[1 - parity][rows, :]
        gate = jnp.minimum(gu, SWIGLU_LIMIT)
        glu = gate * (1.0 / (1.0 + jnp.exp(-gate * SWIGLU_ALPHA)))
        up1 = jnp.clip(gu, -SWIGLU_LIMIT, SWIGLU_LIMIT) + 1.0
        prod = glu * pltpu.roll(up1, 2 * TF_E - 1, 1)
        even = (lax.broadcasted_iota(jnp.int32, (1, TF_E), 1) & 1) == 0
        act_ref[f - 1, rows, :] = jnp.where(even, prod[:, :TF_E], pltpu.roll(prod[:, TF_E:], 1, 1)).astype(BF16)

    def down_and_store(first_row, n_rows):
        rows = pl.ds(first_row, n_rows)
        act = jnp.concatenate([act_ref[c, rows, :] for c in range(NF_E)], axis=1)
        y = jnp.dot(act, wd_bf_ref[...], preferred_element_type=F32) + bd_ref[...]
        _pack_rows(y, o_ref, first_row, n_rows)

    def over_rows(parity, first, last):
        @pl.when(nsub > NSUB_E // 2)
        def _mostly_full():
            rows = pl.ds(0, GROUP_E)
            if first:
                load_rows()
            if not last:
                convert_weights()
                up_matmul(rows, parity)
            if not first:
                activation(rows, parity)
            if last:
                for s in range(GROUP_E // DOWN_E):
                    down_and_store(s * DOWN_E, DOWN_E)

        @pl.when(nsub <= NSUB_E // 2)
        def _few_rows():
            if first:
                load_rows()
            if not last:
                convert_weights()
            if last:
                o_ref[...] = jnp.zeros_like(o_ref)

            def body(s, c):
                first_row = pl.multiple_of(s * SUB_E, SUB_E)
                rows = pl.ds(first_row, SUB_E)
                if not last:
                    up_matmul(rows, parity)
                if not first:
                    activation(rows, parity)
                if last:
                    down_and_store(first_row, SUB_E)
                return c
            lax.fori_loop(0, nsub, body, 0)

    @pl.when(nsub > 0)
    def _group():
        @pl.when(f == 0)
        def _first_chunk():
            over_rows(0, True, False)

        for parity in range(2):
            @pl.when((f > 0) & (f < NF_E) & (f % 2 == parity))
            def _middle_chunk():
                over_rows(parity, False, False)

        @pl.when(f == NF_E)
        def _drain():
            over_rows(NF_E % 2, False, True)


def _experts(group_e, group_nsub, n_used, xs, w_gu, w_dn, b_gu, b_dn, layer, n_slots):
    ngrp = n_slots // GROUP_E
    nf = NF_E

    def fchunk(g, f, nu):
        return jnp.where(g < nu[0], jnp.clip(f, 0, nf - 1), nf - 1)

    return pl.pallas_call(
        _experts_body,
        grid_spec=pltpu.PrefetchScalarGridSpec(
            num_scalar_prefetch=3,
            grid=(ngrp, nf + 1),
            in_specs=[
                pl.BlockSpec((GROUP_E * PACK_ROWS, LANES), lambda g, f, ge, gn, nu: (jnp.minimum(g, nu[0] - 1), 0)),
                pl.BlockSpec((None, None, D_MODEL, 2 * TF_E),
                             lambda g, f, ge, gn, nu: (layer, ge[g], 0, fchunk(g, f, nu))),
                pl.BlockSpec((None, None, TF_E, D_MODEL),
                             lambda g, f, ge, gn, nu: (layer, ge[g], fchunk(g, f, nu), 0)),
                pl.BlockSpec((None, None, 1, 2 * TF_E),
                             lambda g, f, ge, gn, nu: (layer, ge[g], 0, fchunk(g, f, nu))),
                pl.BlockSpec((None, None, 1, D_MODEL), lambda g, f, ge, gn, nu: (layer, ge[g], 0, 0)),
            ],
            out_specs=pl.BlockSpec((GROUP_E * PACK_ROWS, LANES), lambda g, f, ge, gn, nu: (g, 0)),
            scratch_shapes=[
                pltpu.VMEM((GROUP_E, D_MODEL), BF16),
                pltpu.VMEM((D_MODEL, 2 * TF_E), BF16),
                pltpu.VMEM((D_FF, D_MODEL), BF16),
                pltpu.VMEM((GROUP_E, 2 * TF_E), F32),
                pltpu.VMEM((GROUP_E, 2 * TF_E), F32),
                pltpu.VMEM((NF_E, GROUP_E, TF_E), BF16),
            ],
        ),
        out_shape=jax.ShapeDtypeStruct((n_slots * PACK_ROWS, LANES), U32),
        compiler_params=_cparams(("arbitrary", "arbitrary")),
        name="experts",
    )(group_e, group_nsub, n_used, xs, w_gu, w_dn, b_gu, b_dn)


def _combine_body(dest_ref, dnext_ref, ys_ref, w_ref, x1_ref, g_ref, o_ref, buf_ref, sems):
    i = pl.program_id(0)
    n = pl.num_programs(0)
    tm = x1_ref.shape[0]
    buf_rows = tm * TOP_K
    slot = i % 2

    def issue_token(d_ref, b, t):
        for k in range(TOP_K):
            src = _row_slice(ys_ref, d_ref[t * TOP_K + k])
            dst = _row_slice(buf_ref, b * buf_rows + k * tm + t)
            pltpu.make_async_copy(src, dst, sems.at[b]).start(priority=k % 2)

    def wait_buffer(b):
        dst = _row_slice(buf_ref, b * buf_rows, buf_rows)
        pltpu.make_async_copy(_row_slice(ys_ref, 0, buf_rows), dst, sems.at[b]).wait()

    @pl.when(i == 0)
    def _first_tile():
        def body(t, c):
            issue_token(dest_ref, 0, t)
            return c
        lax.fori_loop(0, tm, body, 0)

    base = slot * buf_rows
    wait_buffer(slot)

    w = w_ref[...]
    wk = [jnp.broadcast_to(w[:, k:k + 1], (tm, LANES)) for k in range(TOP_K)]
    ssq = jnp.zeros((tm, 1), F32)
    ahead = tm // PACK_ROWS
    for r in range(PACK_ROWS):
        for t in range(r * ahead, (r + 1) * ahead):
            issue_token(dnext_ref, 1 - slot, t)
        acc_lo = acc_hi = None
        for k in range(TOP_K):
            word = buf_ref[pl.ds((base + k * tm) * PACK_ROWS + r, tm, stride=PACK_ROWS), :]
            lo = pltpu.bitcast(lax.shift_left(word, U32(16)), F32) * wk[k]
            hi = pltpu.bitcast(word & U32(0xFFFF0000), F32) * wk[k]
            acc_lo = lo if acc_lo is None else acc_lo + lo
            acc_hi = hi if acc_hi is None else acc_hi + hi
        for c, moe in ((2 * r, acc_lo), (2 * r + 1, acc_hi)):
            cols = slice(c * LANES, (c + 1) * LANES)
            v = x1_ref[:, cols] + moe
            o_ref[:, cols] = v
            ssq = ssq + jnp.sum(v * v, axis=-1, keepdims=True)
    inv = lax.rsqrt(ssq * np.float32(1.0 / D_MODEL) + RMS_EPS)
    o_ref[...] = o_ref[...] * inv * g_ref[...]

    @pl.when(i == n - 1)
    def _drain_extra_gather():
        wait_buffer(1 - slot)


def _combine(dest_flat, ys, top_w, x1, g):
    t = x1.shape[0]
    tm = TM_COMB
    nt = t // tm
    return pl.pallas_call(
        _combine_body,
        grid=(nt,),
        in_specs=[
            pl.BlockSpec((tm * TOP_K,), lambda i: (i,), memory_space=pltpu.SMEM),
            pl.BlockSpec((tm * TOP_K,), lambda i: (jnp.minimum(i + 1, nt - 1),), memory_space=pltpu.SMEM),
            pl.BlockSpec(memory_space=pl.ANY),
            pl.BlockSpec((tm, TOP_K), lambda i: (i, 0)),
            pl.BlockSpec((tm, D_MODEL), lambda i: (i, 0)),
            pl.BlockSpec((1, D_MODEL), lambda i: (0, 0)),
        ],
        out_specs=pl.BlockSpec((tm, D_MODEL), lambda i: (i, 0)),
        out_shape=jax.ShapeDtypeStruct((t, D_MODEL), F32),
        scratch_shapes=[pltpu.VMEM((2 * tm * TOP_K * PACK_ROWS, LANES), U32), pltpu.SemaphoreType.DMA((2,))],
        compiler_params=_cparams(("arbitrary",)),
        name="combine",
    )(dest_flat, dest_flat, ys, top_w, x1, g)


def _routing_tables(counts, top_e, top_pos, n_slots):
    i32 = jnp.int32
    ngrp = n_slots // GROUP_E
    experts = jnp.arange(N_EXPERTS, dtype=i32)
    padded = (counts + GROUP_E - 1) // GROUP_E * GROUP_E
    ends = jnp.cumsum(padded)
    starts = ends - padded
    total = ends[-1]
    dest = jnp.sum(jnp.where(top_e[..., None] == experts, starts, 0), axis=-1) + top_pos
    gstart = jnp.arange(ngrp, dtype=i32) * GROUP_E
    used = gstart < total
    of_group = gstart[:, None] >= ends[None, :]
    last_e = jnp.max(jnp.where(padded > 0, experts, 0))
    group_e = jnp.where(used, jnp.minimum(jnp.sum(of_group, axis=1), N_EXPERTS - 1), last_e)
    valid_end = jnp.sum(jnp.where(group_e[:, None] == experts, starts + counts, 0), axis=-1)
    rows = jnp.clip(valid_end - gstart, 0, GROUP_E)
    group_nsub = jnp.where(used, (rows + SUB_E - 1) // SUB_E, 0)
    n_used = (total // GROUP_E).reshape(1)
    sub = jnp.arange(NSUB_E, dtype=i32) * SUB_E
    tail = (starts + counts // SUB_E * SUB_E)[:, None] + sub
    tail = jnp.where(tail < ends[:, None], tail, -1)
    spare = (n_used[0] + experts)[:, None] * GROUP_E + sub
    spare = jnp.where(spare < n_slots, spare, -1)
    zstart = jnp.concatenate([tail.reshape(-1), spare.reshape(-1)])
    return (dest.reshape(-1).astype(i32), group_e.astype(i32), group_nsub.astype(i32), n_used.astype(i32),
            zstart.astype(i32))


def kernel(x, mem, rel_bias_table, norm1_g, w_in, attn_sink, w_mem_kv, w_fourier_out, w_attn_out, w_mem_out,
           w_o, norm2_g, w_router, b_router, w_gate_up, b_gate_up, w_down, b_down, final_g):
    batch, seq, _ = x.shape
    t = batch * seq
    depth = norm1_g.shape[0]
    assert depth == 1, "the combine kernel applies the final RMSNorm, so a single layer is supported"
    x2d = x.reshape(t, D_MODEL)
    mem2d = mem.reshape(batch * MEM_LEN, D_MODEL)
    bias = _band_bias(rel_bias_table)
    n_slots = t * TOP_K + N_EXPERTS * GROUP_E

    for l in range(depth):
        wi = w_in[l].astype(BF16)
        wi = jnp.concatenate([wi[:, _OFF_GATE:], wi[:, :_OFF_Q], wi[:, _OFF_MQ:_OFF_GATE],
                              wi[:, _OFF_Q:_OFF_K], wi[:, _OFF_K:_OFF_V], wi[:, _OFF_V:_OFF_MQ]], axis=1)
        proj = _inproj(x2d, norm1_g[l].reshape(1, D_MODEL), wi)
        y_f = _fourier(proj, w_fourier_out[l].astype(BF16), batch, seq)
        att = _swa(proj, attn_sink[l], bias, batch, seq)
        kv = _memkv(mem2d, w_mem_kv[l].astype(BF16))
        mo = _memattn(proj, kv, batch, seq)
        mixed = _branches(y_f, att, mo, proj, w_attn_out[l].astype(BF16), w_mem_out[l].astype(BF16))
        x1, h2s, top_e, top_pos, top_w, counts = _merge(
            mixed, x2d, w_o[l].astype(BF16),
            norm2_g[l].reshape(1, D_MODEL), w_router[l], b_router[l].reshape(1, N_EXPERTS))
        dest, group_e, group_nsub, n_used, zstart = _routing_tables(counts[0], top_e, top_pos, n_slots)
        xs = _dispatch(zstart, dest, h2s, n_slots)
        ys = _experts(group_e, group_nsub, n_used, xs, w_gate_up, w_down,
                      b_gate_up.reshape(depth, N_EXPERTS, 1, 2 * D_FF), b_down.reshape(depth, N_EXPERTS, 1, D_MODEL),
                      l, n_slots)
        x2d = _combine(dest, ys, top_w, x1, final_g.reshape(1, D_MODEL))
    return x2d.reshape(batch, seq, D_MODEL)
```

```python
import functools

import numpy as np
import jax
import jax.numpy as jnp
from jax import lax
from jax.experimental import pallas as pl
from jax.experimental.pallas import tpu as pltpu

D_MODEL = 2048
MEM_LEN = 256
FOURIER_GROUPS = 4
FOURIER_GROUP_DIM = 256
FOURIER_WIDTH = 1024
N_HEADS = 32
N_KV_HEADS = 4
HEAD_DIM = 64
ATTN_WIDTH = 2048
KV_WIDTH = 256
WINDOW = 128
ATT_BLOCK = 128
MEM_HEADS = 4
MEM_HEAD_DIM = 256
MEM_WIDTH = 1024
N_BRANCHES = 3
NUM_BUCKETS = 32
MAX_DISTANCE = 128
N_EXPERTS = 32
TOP_K = 4
D_FF = 2048
SWIGLU_LIMIT = 7.0
SWIGLU_ALPHA = 1.702
RMS_EPS = 1e-5
NEG_INF = -1e30

GATE_WIDTH = N_BRANCHES * D_MODEL
COL_GATE = 0
COL_F = GATE_WIDTH
COL_MQ = COL_F + FOURIER_WIDTH
COL_Q = COL_MQ + MEM_WIDTH
COL_K = COL_Q + ATTN_WIDTH
COL_V = COL_K + KV_WIDTH
IN_WIDTH = COL_V + KV_WIDTH

_OFF_Q = FOURIER_WIDTH
_OFF_K = _OFF_Q + ATTN_WIDTH
_OFF_V = _OFF_K + KV_WIDTH
_OFF_MQ = _OFF_V + KV_WIDTH
_OFF_GATE = _OFF_MQ + MEM_WIDTH

LANES = 128
PACK_ROWS = D_MODEL // (2 * LANES)
VMEM_CAPACITY_V7X = 64 * 1024 * 1024
VMEM_LIMIT = VMEM_CAPACITY_V7X // 8 * 7
EXPERTS_VMEM_LIMIT = 58 * 1024 * 1024

TM_IN, TN_IN = 512, 1536
GATE_SLICES = 3
TS_F = 512
TQ_MEM = 512
TM_MERGE = 512
TM_DISP = 256
DISP_UNROLL = 8
MAIN_E, SUB_E, TF_E = 1024, 128, 256
OVER_E = SUB_E
GROUP_E = MAIN_E + OVER_E
NSUB_E = GROUP_E // SUB_E
NF_E = D_FF // TF_E
DOWN_E = 512
TM_COMB = 256

BF16 = jnp.bfloat16
F32 = jnp.float32
U32 = jnp.uint32


def _cparams(sem, vmem_limit=VMEM_LIMIT):
    return pltpu.CompilerParams(dimension_semantics=sem, vmem_limit_bytes=vmem_limit)


def _bf16_bits(v):
    return pltpu.bitcast(v.astype(BF16).astype(F32), U32)


def _pack_rows(v, out_ref, first_row, rows):
    for r in range(PACK_ROWS):
        lo = _bf16_bits(v[:, (2 * r) * LANES:(2 * r + 1) * LANES])
        hi = _bf16_bits(v[:, (2 * r + 1) * LANES:(2 * r + 2) * LANES])
        out_ref[pl.ds(first_row * PACK_ROWS + r, rows, stride=PACK_ROWS), :] = lax.shift_right_logical(lo, U32(16)) | hi


def _unpack_rows(ref, first_row, rows):
    chunks = []
    for r in range(PACK_ROWS):
        w = ref[pl.ds(first_row * PACK_ROWS + r, rows, stride=PACK_ROWS), :]
        chunks.append(pltpu.bitcast(lax.shift_left(w, U32(16)), F32))
        chunks.append(pltpu.bitcast(w & U32(0xFFFF0000), F32))
    return chunks


def _inproj_body(x_ref, g_ref, w_ref, o_ref, h_ref, *, n_gate_tiles):
    j = pl.program_id(1)

    @pl.when(j == 0)
    def _norm():
        x = x_ref[...]
        inv = lax.rsqrt(jnp.mean(x * x, axis=-1, keepdims=True) + RMS_EPS)
        h_ref[...] = (x * inv * g_ref[...]).astype(BF16)

    @pl.when(j < n_gate_tiles)
    def _gate():
        width = w_ref.shape[1] // GATE_SLICES
        for c in range(GATE_SLICES):
            cols = slice(c * width, (c + 1) * width)
            acc = jnp.dot(h_ref[...], w_ref[:, cols], preferred_element_type=F32)
            o_ref[:, cols] = (1.0 / (1.0 + jnp.exp(-acc))).astype(o_ref.dtype)

    @pl.when(j >= n_gate_tiles)
    def _plain():
        o_ref[...] = jnp.dot(h_ref[...], w_ref[...], preferred_element_type=F32).astype(o_ref.dtype)


def _inproj(x2d, g, w):
    t = x2d.shape[0]
    return pl.pallas_call(
        functools.partial(_inproj_body, n_gate_tiles=GATE_WIDTH // TN_IN),
        grid=(t // TM_IN, IN_WIDTH // TN_IN),
        in_specs=[
            pl.BlockSpec((TM_IN, D_MODEL), lambda i, j: (i, 0)),
            pl.BlockSpec((1, D_MODEL), lambda i, j: (0, 0)),
            pl.BlockSpec((D_MODEL, TN_IN), lambda i, j: (0, j)),
        ],
        out_specs=pl.BlockSpec((TM_IN, TN_IN), lambda i, j: (i, j)),
        out_shape=jax.ShapeDtypeStruct((t, IN_WIDTH), BF16),
        scratch_shapes=[pltpu.VMEM((TM_IN, D_MODEL), BF16)],
        compiler_params=_cparams(("parallel", "arbitrary")),
        name="inproj",
    )(x2d, g, w)


def _fourier_body(f_ref, cc_ref, sc_ref, cs_ref, wfo_ref, o_ref, z_ref, *, seq):
    st = pl.program_id(1)

    @pl.when(st == 0)
    def _channel_dft():
        for g in range(FOURIER_GROUPS):
            cols = slice(g * FOURIER_GROUP_DIM, (g + 1) * FOURIER_GROUP_DIM)
            xg = f_ref[:, cols]
            z_ref[0:seq, cols] = jnp.dot(xg, cc_ref[...], preferred_element_type=F32).astype(BF16)
            z_ref[seq:2 * seq, cols] = jnp.dot(xg, sc_ref[...], preferred_element_type=F32).astype(BF16)

    y = jnp.dot(cs_ref[...], z_ref[...], preferred_element_type=F32)
    o_ref[...] = jnp.dot(y.astype(BF16), wfo_ref[...], preferred_element_type=F32).astype(o_ref.dtype)


def _dft_tables(seq):
    c = FOURIER_GROUP_DIM
    kc = (np.arange(c)[:, None] * np.arange(c)[None, :]) % c
    ac = 2.0 * np.pi * kc / c
    ks = (np.arange(seq)[:, None] * np.arange(seq)[None, :]) % seq
    a_s = 2.0 * np.pi * ks / seq
    cc = (np.cos(ac) / np.sqrt(c)).astype(np.float32)
    sc = (np.sin(ac) / np.sqrt(c)).astype(np.float32)
    cs = np.concatenate([np.cos(a_s), -np.sin(a_s)], axis=1) / np.sqrt(seq)
    return cc, sc, cs.astype(np.float32)


def _fourier(proj, w_fo, batch, seq):
    cc, sc, cs = _dft_tables(seq)
    cc = jnp.asarray(cc).astype(BF16)
    sc = jnp.asarray(sc).astype(BF16)
    cs = jnp.asarray(cs).astype(BF16)
    t = batch * seq
    return pl.pallas_call(
        functools.partial(_fourier_body, seq=seq),
        grid=(batch, seq // TS_F),
        in_specs=[
            pl.BlockSpec((seq, FOURIER_WIDTH), lambda b, s: (b, COL_F // FOURIER_WIDTH)),
            pl.BlockSpec((FOURIER_GROUP_DIM, FOURIER_GROUP_DIM), lambda b, s: (0, 0)),
            pl.BlockSpec((FOURIER_GROUP_DIM, FOURIER_GROUP_DIM), lambda b, s: (0, 0)),
            pl.BlockSpec((TS_F, 2 * seq), lambda b, s: (s, 0)),
            pl.BlockSpec((FOURIER_WIDTH, D_MODEL), lambda b, s: (0, 0)),
        ],
        out_specs=pl.BlockSpec((TS_F, D_MODEL), lambda b, s: (b * (seq // TS_F) + s, 0)),
        out_shape=jax.ShapeDtypeStruct((t, D_MODEL), BF16),
        scratch_shapes=[pltpu.VMEM((2 * seq, FOURIER_WIDTH), BF16)],
        compiler_params=_cparams(("parallel", "arbitrary")),
        name="fourier",
    )(proj, cc, sc, cs, w_fo)


def _swa_body(sink_ref, q_ref, k_ref, v_ref, bias_ref, o_ref):
    n = pl.program_id(1)
    nb = pl.num_programs(1)
    prev = pl.multiple_of(jnp.maximum(n - 1, 0) * ATT_BLOCK, ATT_BLOCK)
    own = pl.multiple_of(n * ATT_BLOCK, ATT_BLOCK)
    nxt = pl.multiple_of(jnp.minimum(n + 1, nb - 1) * ATT_BLOCK, ATT_BLOCK)

    def band(ref):
        return jnp.concatenate(
            [ref[pl.ds(prev, ATT_BLOCK), :], ref[pl.ds(own, ATT_BLOCK), :], ref[pl.ds(nxt, ATT_BLOCK), :]], axis=0)

    kb = band(k_ref).astype(F32)
    vb = band(v_ref).astype(F32)
    key = lax.broadcasted_iota(jnp.int32, (3 * ATT_BLOCK, 1), 0)
    valid = ((key >= ATT_BLOCK) | (n > 0)) & ((key < 2 * ATT_BLOCK) | (n < nb - 1))
    low = lax.broadcasted_iota(jnp.int32, (1, LANES), 1) < HEAD_DIM
    low_row = lax.broadcasted_iota(jnp.int32, (LANES, 1), 0) < HEAD_DIM

    def halves(x, c, kv_is_low):
        a = x[:, c * LANES:(c + 1) * LANES]
        r = pltpu.roll(a, HEAD_DIM, 1)
        in_low, in_high = (a, r) if kv_is_low else (r, a)
        return jnp.where(low, in_low, 0.0), jnp.where(low, 0.0, in_high)

    pairs = N_HEADS // N_KV_HEADS // 2
    for g in range(N_KV_HEADS):
        k_par = [h.astype(BF16) for h in halves(kb, g // 2, g % 2 == 0)]
        vt_par = [h.T.astype(BF16) for h in halves(vb, g // 2, g % 2 == 0)]
        base = g * pairs * LANES
        qp = jnp.concatenate([q_ref[:, base + p * LANES:base + (p + 1) * LANES] for p in range(pairs)], axis=0)
        qp = qp * jnp.asarray(HEAD_DIM ** -0.5, BF16)
        probs, inv_den = [], []
        for e in range(2):
            s = lax.dot_general(k_par[e], qp, (((1,), (1,)), ((), ())), preferred_element_type=F32)
            s = jnp.where(valid, s + bias_ref[g, e], NEG_INF)
            sink = jnp.concatenate(
                [jnp.full((1, ATT_BLOCK), sink_ref[(g * pairs + p) * 2 + e], F32) for p in range(pairs)], axis=1)
            m = jnp.maximum(jnp.max(s, axis=0, keepdims=True), sink)
            p_un = jnp.exp(s - m)
            den = jnp.sum(p_un, axis=0, keepdims=True) + jnp.exp(sink - m)
            probs.append(p_un.astype(BF16))
            inv_den.append(1.0 / den)
        o_t = (jnp.dot(vt_par[0], probs[0], preferred_element_type=F32)
               + jnp.dot(vt_par[1], probs[1], preferred_element_type=F32))
        o = (o_t * jnp.where(low_row, inv_den[0], inv_den[1])).T
        for p in range(pairs):
            o_ref[:, base + p * LANES:base + (p + 1) * LANES] = o[p * ATT_BLOCK:(p + 1) * ATT_BLOCK].astype(o_ref.dtype)


def _t5_bucket(rel):
    nb = NUM_BUCKETS // 2
    max_exact = nb // 2
    ret = (rel > 0).astype(jnp.int32) * nb
    n = jnp.abs(rel)
    nf = jnp.maximum(n, 1).astype(jnp.float32)
    large = max_exact + (jnp.log(nf / max_exact) / np.float32(np.log(MAX_DISTANCE / max_exact))
                         * (nb - max_exact)).astype(jnp.int32)
    large = jnp.minimum(large, nb - 1)
    return ret + jnp.where(n < max_exact, n, large)


def _band_bias(rel_table):
    q_off = jnp.arange(ATT_BLOCK)[:, None]
    k_off = jnp.arange(3 * ATT_BLOCK)[None, :]
    rel = k_off - ATT_BLOCK - q_off
    hot = (_t5_bucket(rel)[..., None] == jnp.arange(NUM_BUCKETS)).astype(F32)
    bias = jnp.einsum("qkb,bh->hqk", hot, rel_table.astype(F32), precision=lax.Precision.HIGHEST)
    bias = jnp.where(jnp.abs(rel) <= WINDOW, bias, NEG_INF)
    pairs = N_HEADS // N_KV_HEADS // 2
    bias = bias.reshape(N_KV_HEADS, pairs, 2, ATT_BLOCK, 3 * ATT_BLOCK).transpose(0, 2, 4, 1, 3)
    return bias.reshape(N_KV_HEADS, 2, 3 * ATT_BLOCK, pairs * ATT_BLOCK)


def _swa(proj, sink, bias, batch, seq):
    nb = seq // ATT_BLOCK
    t = batch * seq
    return pl.pallas_call(
        _swa_body,
        grid_spec=pltpu.PrefetchScalarGridSpec(
            num_scalar_prefetch=0,
            grid=(batch, nb),
            in_specs=[
                pl.BlockSpec(memory_space=pltpu.SMEM),
                pl.BlockSpec((ATT_BLOCK, ATTN_WIDTH), lambda b, n: (b * nb + n, COL_Q // ATTN_WIDTH)),
                pl.BlockSpec((seq, KV_WIDTH), lambda b, n: (b, COL_K // KV_WIDTH)),
                pl.BlockSpec((seq, KV_WIDTH), lambda b, n: (b, COL_V // KV_WIDTH)),
                pl.BlockSpec(bias.shape, lambda b, n: (0, 0, 0, 0)),
            ],
            out_specs=pl.BlockSpec((ATT_BLOCK, ATTN_WIDTH), lambda b, n: (b * nb + n, 0)),
        ),
        out_shape=jax.ShapeDtypeStruct((t, ATTN_WIDTH), BF16),
        compiler_params=_cparams(("parallel", "arbitrary")),
        name="swa",
    )(sink, proj, proj, proj, bias)


def _memkv_body(m_ref, w_ref, o_ref):
    o_ref[...] = jnp.dot(m_ref[...].astype(BF16), w_ref[...], preferred_element_type=F32).astype(o_ref.dtype)


def _memkv(mem2d, w):
    rows = mem2d.shape[0]
    tm, tn = 512, 1024
    return pl.pallas_call(
        _memkv_body,
        grid=(rows // tm, (2 * MEM_WIDTH) // tn),
        in_specs=[
            pl.BlockSpec((tm, D_MODEL), lambda i, j: (i, 0)),
            pl.BlockSpec((D_MODEL, tn), lambda i, j: (0, j)),
        ],
        out_specs=pl.BlockSpec((tm, tn), lambda i, j: (i, j)),
        out_shape=jax.ShapeDtypeStruct((rows, 2 * MEM_WIDTH), BF16),
        compiler_params=_cparams(("parallel", "arbitrary")),
        name="memkv",
    )(mem2d, w)


def _memattn_body(q_ref, k_ref, v_ref, o_ref):
    outs = []
    for h in range(MEM_HEADS):
        cols = slice(h * MEM_HEAD_DIM, (h + 1) * MEM_HEAD_DIM)
        s = lax.dot_general(q_ref[:, cols], k_ref[:, cols], (((1,), (1,)), ((), ())), preferred_element_type=F32)
        s = s * np.float32(MEM_HEAD_DIM ** -0.5)
        m = jnp.max(s, axis=-1, keepdims=True)
        p = jnp.exp(s - m)
        den = jnp.sum(p, axis=-1, keepdims=True)
        o = jnp.dot(p.astype(BF16), v_ref[:, cols], preferred_element_type=F32)
        outs.append(o / den)
    o_ref[...] = jnp.concatenate(outs, axis=1).astype(o_ref.dtype)


def _memattn(proj, kv, batch, seq):
    t = batch * seq
    nq = seq // TQ_MEM
    return pl.pallas_call(
        _memattn_body,
        grid=(batch, nq),
        in_specs=[
            pl.BlockSpec((TQ_MEM, MEM_WIDTH), lambda b, i: (b * nq + i, COL_MQ // MEM_WIDTH)),
            pl.BlockSpec((MEM_LEN, MEM_WIDTH), lambda b, i: (b, 0)),
            pl.BlockSpec((MEM_LEN, MEM_WIDTH), lambda b, i: (b, 1)),
        ],
        out_specs=pl.BlockSpec((TQ_MEM, MEM_WIDTH), lambda b, i: (b * nq + i, 0)),
        out_shape=jax.ShapeDtypeStruct((t, MEM_WIDTH), BF16),
        compiler_params=_cparams(("parallel", "arbitrary")),
        name="memattn",
    )(proj, kv, kv)


def _branches_body(yf_ref, att_ref, mo_ref, g0_ref, g1_ref, g2_ref, wao_ref, wmo_ref, o_ref):
    y_a = jnp.dot(att_ref[...], wao_ref[...], preferred_element_type=F32)
    y_m = jnp.dot(mo_ref[...], wmo_ref[...], preferred_element_type=F32)
    mixed = (g0_ref[...].astype(F32) * yf_ref[...].astype(F32)
             + g1_ref[...].astype(F32) * y_a + g2_ref[...].astype(F32) * y_m)
    o_ref[...] = mixed.astype(o_ref.dtype)


def _branches(y_f, att, mo, proj, w_ao, w_mo):
    t = y_f.shape[0]
    tm = TM_MERGE
    row = lambda i: (i, 0)
    const = lambda i: (0, 0)
    resident = dict(pipeline_mode=pl.Buffered(1))
    return pl.pallas_call(
        _branches_body,
        grid=(t // tm,),
        in_specs=[
            pl.BlockSpec((tm, D_MODEL), row),
            pl.BlockSpec((tm, ATTN_WIDTH), row),
            pl.BlockSpec((tm, MEM_WIDTH), row),
            pl.BlockSpec((tm, D_MODEL), lambda i: (i, 0)),
            pl.BlockSpec((tm, D_MODEL), lambda i: (i, 1)),
            pl.BlockSpec((tm, D_MODEL), lambda i: (i, 2)),
            pl.BlockSpec((ATTN_WIDTH, D_MODEL), const, **resident),
            pl.BlockSpec((MEM_WIDTH, D_MODEL), const, **resident),
        ],
        out_specs=pl.BlockSpec((tm, D_MODEL), row),
        out_shape=jax.ShapeDtypeStruct((t, D_MODEL), BF16),
        compiler_params=_cparams(("parallel",)),
        name="branches",
    )(y_f, att, mo, proj, proj, proj, w_ao, w_mo)


def _merge_body(mixed_ref, x_ref, wo_ref, n2_ref, wr_ref, br_ref,
                x1_ref, h2s_ref, te_ref, tp_ref, tw_ref, cnt_ref, carry_ref):
    i = pl.program_id(0)
    tm = x_ref.shape[0]

    @pl.when(i == 0)
    def _init():
        carry_ref[...] = jnp.zeros_like(carry_ref)

    x1 = x_ref[...] + jnp.dot(mixed_ref[...], wo_ref[...], preferred_element_type=F32)
    x1_ref[...] = x1

    inv = lax.rsqrt(jnp.mean(x1 * x1, axis=-1, keepdims=True) + RMS_EPS)
    h2 = x1 * inv * n2_ref[...]
    _pack_rows(h2, h2s_ref, 0, tm)

    h_hi = h2.astype(BF16)
    h_lo = (h2 - h_hi.astype(F32)).astype(BF16)
    w_r = wr_ref[...]
    w_hi = w_r.astype(BF16)
    w_lo = (w_r - w_hi.astype(F32)).astype(BF16)
    part = jnp.dot(h_hi, jnp.concatenate([w_hi, w_lo], axis=1), preferred_element_type=F32)
    logits = (part[:, :N_EXPERTS] + part[:, N_EXPERTS:]
              + jnp.dot(h_lo, w_hi, preferred_element_type=F32) + br_ref[...])
    lane = lax.broadcasted_iota(jnp.int32, (tm, N_EXPERTS), 1)
    vals, idxs, hots = [], [], []
    rest = logits
    for _ in range(TOP_K):
        mk = jnp.max(rest, axis=-1, keepdims=True)
        ik = jnp.min(jnp.where(rest == mk, lane, N_EXPERTS), axis=-1, keepdims=True)
        hot = lane == ik
        vals.append(mk)
        idxs.append(ik)
        hots.append(hot)
        rest = jnp.where(hot, -jnp.inf, rest)
    exps = [jnp.exp(v - vals[0]) for v in vals]
    den = exps[0] + exps[1] + exps[2] + exps[3]

    sel = (hots[0] | hots[1] | hots[2] | hots[3]).astype(F32)
    r_i = lax.broadcasted_iota(jnp.int32, (tm, tm), 0)
    c_i = lax.broadcasted_iota(jnp.int32, (tm, tm), 1)
    tri = (c_i < r_i).astype(BF16)
    pos = jnp.dot(tri, sel.astype(BF16), preferred_element_type=F32) + carry_ref[...]
    poss = [jnp.sum(jnp.where(hot, pos, 0.0), axis=-1, keepdims=True) for hot in hots]
    carry = carry_ref[...] + jnp.sum(sel, axis=0, keepdims=True)
    carry_ref[...] = carry

    te_ref[...] = jnp.concatenate(idxs, axis=1)
    tp_ref[...] = jnp.concatenate(poss, axis=1).astype(jnp.int32)
    tw_ref[...] = jnp.concatenate([e / den for e in exps], axis=1)
    cnt_ref[...] = carry.astype(jnp.int32)


def _merge(mixed, x2d, w_o, n2, w_r, b_r):
    t = x2d.shape[0]
    tm = TM_MERGE
    row = lambda i: (i, 0)
    const = lambda i: (0, 0)
    resident = dict(pipeline_mode=pl.Buffered(1))
    return pl.pallas_call(
        _merge_body,
        grid=(t // tm,),
        in_specs=[
            pl.BlockSpec((tm, D_MODEL), row),
            pl.BlockSpec((tm, D_MODEL), row),
            pl.BlockSpec((D_MODEL, D_MODEL), const, **resident),
            pl.BlockSpec((1, D_MODEL), const),
            pl.BlockSpec((D_MODEL, N_EXPERTS), const),
            pl.BlockSpec((1, N_EXPERTS), const),
        ],
        out_specs=[
            pl.BlockSpec((tm, D_MODEL), row),
            pl.BlockSpec((tm * PACK_ROWS, LANES), row),
            pl.BlockSpec((tm, TOP_K), row),
            pl.BlockSpec((tm, TOP_K), row),
            pl.BlockSpec((tm, TOP_K), row),
            pl.BlockSpec((1, N_EXPERTS), const),
        ],
        out_shape=[
            jax.ShapeDtypeStruct((t, D_MODEL), F32),
            jax.ShapeDtypeStruct((t * PACK_ROWS, LANES), jnp.uint32),
            jax.ShapeDtypeStruct((t, TOP_K), jnp.int32),
            jax.ShapeDtypeStruct((t, TOP_K), jnp.int32),
            jax.ShapeDtypeStruct((t, TOP_K), F32),
            jax.ShapeDtypeStruct((1, N_EXPERTS), jnp.int32),
        ],
        scratch_shapes=[pltpu.VMEM((1, N_EXPERTS), F32)],
        compiler_params=_cparams(("arbitrary",)),
        name="merge",
    )(mixed, x2d, w_o, n2, w_r, b_r)


def _row_slice(ref, row, n_rows=1):
    start = pl.multiple_of(row * PACK_ROWS, PACK_ROWS)
    return ref.at[pl.ds(start, n_rows * PACK_ROWS), :]


def _dispatch_body(zstart_ref, dest_ref, h_ref, xs_ref, zero_ref, zsem, sem):
    i = pl.program_id(0)
    tm = h_ref.shape[0] // PACK_ROWS

    @pl.when(i == 0)
    def _zero_tails():
        zero_ref[...] = jnp.zeros_like(zero_ref)

        def tail_copy(e):
            return pltpu.make_async_copy(zero_ref, _row_slice(xs_ref, zstart_ref[e], SUB_E), zsem)

        def start(e, c):
            @pl.when(zstart_ref[e] >= 0)
            def _():
                tail_copy(e).start()
            return c

        def wait(e, c):
            @pl.when(zstart_ref[e] >= 0)
            def _():
                tail_copy(e).wait()
            return c

        lax.fori_loop(0, zstart_ref.shape[0], start, 0)
        lax.fori_loop(0, zstart_ref.shape[0], wait, 0)

    def scatter(block, c):
        for u in range(DISP_UNROLL):
            t = block * DISP_UNROLL + u
            for k in range(TOP_K):
                d = dest_ref[t * TOP_K + k]
                pltpu.make_async_copy(_row_slice(h_ref, t), _row_slice(xs_ref, d), sem).start(priority=k % 2)
        return c

    lax.fori_loop(0, tm // DISP_UNROLL, scatter, 0)
    for _ in range(TOP_K):
        pltpu.make_async_copy(h_ref, _row_slice(xs_ref, 0, tm), sem).wait()


def _dispatch(zstart, dest_flat, h2s, n_slots):
    t = h2s.shape[0] // PACK_ROWS
    tm = TM_DISP
    return pl.pallas_call(
        _dispatch_body,
        grid_spec=pltpu.PrefetchScalarGridSpec(
            num_scalar_prefetch=1,
            grid=(t // tm,),
            in_specs=[
                pl.BlockSpec((tm * TOP_K,), lambda i, z: (i,), memory_space=pltpu.SMEM),
                pl.BlockSpec((tm * PACK_ROWS, LANES), lambda i, z: (i, 0)),
            ],
            out_specs=pl.BlockSpec(memory_space=pl.ANY),
            scratch_shapes=[
                pltpu.VMEM((SUB_E * PACK_ROWS, LANES), U32),
                pltpu.SemaphoreType.DMA(()),
                pltpu.SemaphoreType.DMA(()),
            ],
        ),
        out_shape=jax.ShapeDtypeStruct((n_slots * PACK_ROWS, LANES), U32),
        compiler_params=_cparams(("arbitrary",)),
        name="dispatch",
    )(zstart, dest_flat, h2s)


def _experts_body(ge_ref, gn_ref, nu_ref, xs_ref, wgu_ref, wd_ref, bgu_ref, bd_ref, o_ref,
                  xb_ref, wgu_bf_ref, wd_bf_ref, gu_even_ref, gu_odd_ref, act_ref):
    g = pl.program_id(0)
    f = pl.program_id(1)
    nsub = gn_ref[g]
    half = TF_E // 2
    gu_bufs = (gu_even_ref, gu_odd_ref)

    @pl.when((nsub == 0) & (f == 0))
    def _unused_group():
        o_ref[...] = jnp.zeros_like(o_ref)

    def load_rows():
        for r in range(PACK_ROWS):
            w = xs_ref[pl.ds(r, GROUP_E, stride=PACK_ROWS), :]
            lo = pltpu.bitcast(lax.shift_left(w, U32(16)), F32)
            hi = pltpu.bitcast(w & U32(0xFFFF0000), F32)
            xb_ref[:, (2 * r) * LANES:(2 * r + 1) * LANES] = lo.astype(BF16)
            xb_ref[:, (2 * r + 1) * LANES:(2 * r + 2) * LANES] = hi.astype(BF16)

    def convert_weights():
        wgu_bf_ref[...] = wgu_ref[...].astype(BF16)
        lo = _bf16_bits(wd_ref[0:half, :])
        hi = _bf16_bits(wd_ref[half:TF_E, :])
        chunk_rows = pl.ds(pl.multiple_of(f * TF_E, TF_E), TF_E)
        wd_bf_ref[chunk_rows, :] = pltpu.bitcast(lax.shift_right_logical(lo, U32(16)) | hi, BF16)

    def up_matmul(rows, parity):
        gu_bufs[parity][rows, :] = (jnp.dot(xb_ref[rows, :], wgu_bf_ref[...], preferred_element_type=F32)
                                    + bgu_ref[...])

    def activation(rows, parity):
        gu = gu_bufs[1 - parity][rows, :]
        gate = jnp.minimum(gu, SWIGLU_LIMIT)
        glu = gate * (1.0 / (1.0 + jnp.exp(-gate * SWIGLU_ALPHA)))
        up1 = jnp.clip(gu, -SWIGLU_LIMIT, SWIGLU_LIMIT) + 1.0
        prod = glu * pltpu.roll(up1, 2 * TF_E - 1, 1)
        even = (lax.broadcasted_iota(jnp.int32, (1, TF_E), 1) & 1) == 0
        act_ref[f - 1, rows, :] = jnp.where(even, prod[:, :TF_E], pltpu.roll(prod[:, TF_E:], 1, 1)).astype(BF16)

    def down_and_store(first_row, n_rows):
        rows = pl.ds(first_row, n_rows)
        act = jnp.concatenate([act_ref[c, rows, :] for c in range(NF_E)], axis=1)
        y = jnp.dot(act, wd_bf_ref[...], preferred_element_type=F32) + bd_ref[...]
        _pack_rows(y, o_ref, first_row, n_rows)

    def over_rows(parity, first, last):
        @pl.when(nsub > NSUB_E // 2)
        def _mostly_full():
            rows = pl.ds(0, MAIN_E)
            if first:
                load_rows()
            if not last:
                convert_weights()
                up_matmul(rows, parity)
            if not first:
                activation(rows, parity)
            if last:
                for s in range(MAIN_E // DOWN_E):
                    down_and_store(s * DOWN_E, DOWN_E)

            @pl.when(nsub == NSUB_E)
            def _overflow_rows():
                over = pl.ds(MAIN_E, OVER_E)
                if not last:
                    up_matmul(over, parity)
                if not first:
                    activation(over, parity)
                if last:
                    down_and_store(MAIN_E, OVER_E)

            if last:
                @pl.when(nsub < NSUB_E)
                def _no_overflow_rows():
                    o_ref[pl.ds(MAIN_E * PACK_ROWS, OVER_E * PACK_ROWS), :] = jnp.zeros(
                        (OVER_E * PACK_ROWS, LANES), o_ref.dtype)

        @pl.when(nsub <= NSUB_E // 2)
        def _few_rows():
            if first:
                load_rows()
            if not last:
                convert_weights()
            if last:
                o_ref[...] = jnp.zeros_like(o_ref)

            def body(s, c):
                first_row = pl.multiple_of(s * SUB_E, SUB_E)
                rows = pl.ds(first_row, SUB_E)
                if not last:
                    up_matmul(rows, parity)
                if not first:
                    activation(rows, parity)
                if last:
                    down_and_store(first_row, SUB_E)
                return c
            lax.fori_loop(0, nsub, body, 0)

    @pl.when(nsub > 0)
    def _group():
        @pl.when(f == 0)
        def _first_chunk():
            over_rows(0, True, False)

        for parity in range(2):
            @pl.when((f > 0) & (f < NF_E) & (f % 2 == parity))
            def _middle_chunk():
                over_rows(parity, False, False)

        @pl.when(f == NF_E)
        def _drain():
            over_rows(NF_E % 2, False, True)


def _experts(group_e, group_nsub, n_used, xs, w_gu, w_dn, b_gu, b_dn, layer, n_slots):
    ngrp = n_slots // GROUP_E
    nf = NF_E

    def fchunk(g, f, nu):
        return jnp.where(g < nu[0], jnp.clip(f, 0, nf - 1), nf - 1)

    return pl.pallas_call(
        _experts_body,
        grid_spec=pltpu.PrefetchScalarGridSpec(
            num_scalar_prefetch=3,
            grid=(ngrp, nf + 1),
            in_specs=[
                pl.BlockSpec((GROUP_E * PACK_ROWS, LANES), lambda g, f, ge, gn, nu: (jnp.minimum(g, nu[0] - 1), 0)),
                pl.BlockSpec((None, None, D_MODEL, 2 * TF_E),
                             lambda g, f, ge, gn, nu: (layer, ge[g], 0, fchunk(g, f, nu))),
                pl.BlockSpec((None, None, TF_E, D_MODEL),
                             lambda g, f, ge, gn, nu: (layer, ge[g], fchunk(g, f, nu), 0)),
                pl.BlockSpec((None, None, 1, 2 * TF_E),
                             lambda g, f, ge, gn, nu: (layer, ge[g], 0, fchunk(g, f, nu))),
                pl.BlockSpec((None, None, 1, D_MODEL), lambda g, f, ge, gn, nu: (layer, ge[g], 0, 0)),
            ],
            out_specs=pl.BlockSpec((GROUP_E * PACK_ROWS, LANES), lambda g, f, ge, gn, nu: (g, 0)),
            scratch_shapes=[
                pltpu.VMEM((GROUP_E, D_MODEL), BF16),
                pltpu.VMEM((D_MODEL, 2 * TF_E), BF16),
                pltpu.VMEM((D_FF, D_MODEL), BF16),
                pltpu.VMEM((GROUP_E, 2 * TF_E), F32),
                pltpu.VMEM((GROUP_E, 2 * TF_E), F32),
                pltpu.VMEM((NF_E, GROUP_E, TF_E), BF16),
            ],
        ),
        out_shape=jax.ShapeDtypeStruct((n_slots * PACK_ROWS, LANES), U32),
        compiler_params=_cparams(("arbitrary", "arbitrary"), EXPERTS_VMEM_LIMIT),
        name="experts",
    )(group_e, group_nsub, n_used, xs, w_gu, w_dn, b_gu, b_dn)


def _combine_body(dest_ref, dnext_ref, ys_ref, w_ref, x1_ref, g_ref, o_ref, buf_ref, sems):
    i = pl.program_id(0)
    n = pl.num_programs(0)
    tm = x1_ref.shape[0]
    buf_rows = tm * TOP_K
    slot = i % 2

    def issue_token(d_ref, b, t):
        for k in range(TOP_K):
            src = _row_slice(ys_ref, d_ref[t * TOP_K + k])
            dst = _row_slice(buf_ref, b * buf_rows + k * tm + t)
            pltpu.make_async_copy(src, dst, sems.at[b]).start(priority=k % 2)

    def wait_buffer(b):
        dst = _row_slice(buf_ref, b * buf_rows, buf_rows)
        pltpu.make_async_copy(_row_slice(ys_ref, 0, buf_rows), dst, sems.at[b]).wait()

    @pl.when(i == 0)
    def _first_tile():
        def body(t, c):
            issue_token(dest_ref, 0, t)
            return c
        lax.fori_loop(0, tm, body, 0)

    base = slot * buf_rows
    wait_buffer(slot)

    w = w_ref[...]
    wk = [jnp.broadcast_to(w[:, k:k + 1], (tm, LANES)) for k in range(TOP_K)]
    ssq = jnp.zeros((tm, 1), F32)
    ahead = tm // PACK_ROWS
    for r in range(PACK_ROWS):
        for t in range(r * ahead, (r + 1) * ahead):
            issue_token(dnext_ref, 1 - slot, t)
        acc_lo = acc_hi = None
        for k in range(TOP_K):
            word = buf_ref[pl.ds((base + k * tm) * PACK_ROWS + r, tm, stride=PACK_ROWS), :]
            lo = pltpu.bitcast(lax.shift_left(word, U32(16)), F32) * wk[k]
            hi = pltpu.bitcast(word & U32(0xFFFF0000), F32) * wk[k]
            acc_lo = lo if acc_lo is None else acc_lo + lo
            acc_hi = hi if acc_hi is None else acc_hi + hi
        for c, moe in ((2 * r, acc_lo), (2 * r + 1, acc_hi)):
            cols = slice(c * LANES, (c + 1) * LANES)
            v = x1_ref[:, cols] + moe
            o_ref[:, cols] = v
            ssq = ssq + jnp.sum(v * v, axis=-1, keepdims=True)
    inv = lax.rsqrt(ssq * np.float32(1.0 / D_MODEL) + RMS_EPS)
    o_ref[...] = o_ref[...] * inv * g_ref[...]

    @pl.when(i == n - 1)
    def _drain_extra_gather():
        wait_buffer(1 - slot)


def _combine(dest_flat, ys, top_w, x1, g):
    t = x1.shape[0]
    tm = TM_COMB
    nt = t // tm
    return pl.pallas_call(
        _combine_body,
        grid=(nt,),
        in_specs=[
            pl.BlockSpec((tm * TOP_K,), lambda i: (i,), memory_space=pltpu.SMEM),
            pl.BlockSpec((tm * TOP_K,), lambda i: (jnp.minimum(i + 1, nt - 1),), memory_space=pltpu.SMEM),
            pl.BlockSpec(memory_space=pl.ANY),
            pl.BlockSpec((tm, TOP_K), lambda i: (i, 0)),
            pl.BlockSpec((tm, D_MODEL), lambda i: (i, 0)),
            pl.BlockSpec((1, D_MODEL), lambda i: (0, 0)),
        ],
        out_specs=pl.BlockSpec((tm, D_MODEL), lambda i: (i, 0)),
        out_shape=jax.ShapeDtypeStruct((t, D_MODEL), F32),
        scratch_shapes=[pltpu.VMEM((2 * tm * TOP_K * PACK_ROWS, LANES), U32), pltpu.SemaphoreType.DMA((2,))],
        compiler_params=_cparams(("arbitrary",)),
        name="combine",
    )(dest_flat, dest_flat, ys, top_w, x1, g)


def _routing_tables(counts, top_e, top_pos, n_slots):
    i32 = jnp.int32
    ngrp = n_slots // GROUP_E
    experts = jnp.arange(N_EXPERTS, dtype=i32)
    padded = (counts + GROUP_E - 1) // GROUP_E * GROUP_E
    ends = jnp.cumsum(padded)
    starts = ends - padded
    total = ends[-1]
    dest = jnp.sum(jnp.where(top_e[..., None] == experts, starts, 0), axis=-1) + top_pos
    gstart = jnp.arange(ngrp, dtype=i32) * GROUP_E
    used = gstart < total
    of_group = gstart[:, None] >= ends[None, :]
    last_e = jnp.max(jnp.where(padded > 0, experts, 0))
    group_e = jnp.where(used, jnp.minimum(jnp.sum(of_group, axis=1), N_EXPERTS - 1), last_e)
    valid_end = jnp.sum(jnp.where(group_e[:, None] == experts, starts + counts, 0), axis=-1)
    rows = jnp.clip(valid_end - gstart, 0, GROUP_E)
    group_nsub = jnp.where(used, (rows + SUB_E - 1) // SUB_E, 0)
    n_used = (total // GROUP_E).reshape(1)
    sub = jnp.arange(NSUB_E, dtype=i32) * SUB_E
    tail = (starts + counts // SUB_E * SUB_E)[:, None] + sub
    tail = jnp.where(tail < ends[:, None], tail, -1)
    spare = (n_used[0] + experts)[:, None] * GROUP_E + sub
    spare = jnp.where(spare < n_slots, spare, -1)
    zstart = jnp.concatenate([tail.reshape(-1), spare.reshape(-1)])
    return (dest.reshape(-1).astype(i32), group_e.astype(i32), group_nsub.astype(i32), n_used.astype(i32),
            zstart.astype(i32))


def kernel(x, mem, rel_bias_table, norm1_g, w_in, attn_sink, w_mem_kv, w_fourier_out, w_attn_out, w_mem_out,
           w_o, norm2_g, w_router, b_router, w_gate_up, b_gate_up, w_down, b_down, final_g):
    batch, seq, _ = x.shape
    t = batch * seq
    depth = norm1_g.shape[0]
    assert depth == 1, "the combine kernel applies the final RMSNorm, so a single layer is supported"
    x2d = x.reshape(t, D_MODEL)
    mem2d = mem.reshape(batch * MEM_LEN, D_MODEL)
    bias = _band_bias(rel_bias_table)
    n_slots = pl.cdiv(t * TOP_K + N_EXPERTS * (GROUP_E - 1), GROUP_E) * GROUP_E

    for l in range(depth):
        wi = w_in[l].astype(BF16)
        wi = jnp.concatenate([wi[:, _OFF_GATE:], wi[:, :_OFF_Q], wi[:, _OFF_MQ:_OFF_GATE],
                              wi[:, _OFF_Q:_OFF_K], wi[:, _OFF_K:_OFF_V], wi[:, _OFF_V:_OFF_MQ]], axis=1)
        proj = _inproj(x2d, norm1_g[l].reshape(1, D_MODEL), wi)
        y_f = _fourier(proj, w_fourier_out[l].astype(BF16), batch, seq)
        att = _swa(proj, attn_sink[l], bias, batch, seq)
        kv = _memkv(mem2d, w_mem_kv[l].astype(BF16))
        mo = _memattn(proj, kv, batch, seq)
        mixed = _branches(y_f, att, mo, proj, w_attn_out[l].astype(BF16), w_mem_out[l].astype(BF16))
        x1, h2s, top_e, top_pos, top_w, counts = _merge(
            mixed, x2d, w_o[l].astype(BF16),
            norm2_g[l].reshape(1, D_MODEL), w_router[l], b_router[l].reshape(1, N_EXPERTS))
        dest, group_e, group_nsub, n_used, zstart = _routing_tables(counts[0], top_e, top_pos, n_slots)
        xs = _dispatch(zstart, dest, h2s, n_slots)
        ys = _experts(group_e, group_nsub, n_used, xs, w_gate_up, w_down,
                      b_gate_up.reshape(depth, N_EXPERTS, 1, 2 * D_FF), b_down.reshape(depth, N_EXPERTS, 1, D_MODEL),
                      l, n_slots)
        x2d = _combine(dest, ys, top_w, x1, final_g.reshape(1, D_MODEL))
    return x2d.reshape(batch, seq, D_MODEL)
```

```python
import functools

import numpy as np
import jax
import jax.numpy as jnp
from jax import lax
from jax.experimental import pallas as pl
from jax.experimental.pallas import tpu as pltpu

D_MODEL = 2048
MEM_LEN = 256
FOURIER_GROUPS = 4
FOURIER_GROUP_DIM = 256
FOURIER_WIDTH = 1024
N_HEADS = 32
N_KV_HEADS = 4
HEAD_DIM = 64
ATTN_WIDTH = 2048
KV_WIDTH = 256
WINDOW = 128
ATT_BLOCK = 128
MEM_HEADS = 4
MEM_HEAD_DIM = 256
MEM_WIDTH = 1024
N_BRANCHES = 3
NUM_BUCKETS = 32
MAX_DISTANCE = 128
N_EXPERTS = 32
TOP_K = 4
D_FF = 2048
SWIGLU_LIMIT = 7.0
SWIGLU_ALPHA = 1.702
RMS_EPS = 1e-5
NEG_INF = -1e30

GATE_WIDTH = N_BRANCHES * D_MODEL
COL_GATE = 0
COL_F = GATE_WIDTH
COL_MQ = COL_F + FOURIER_WIDTH
COL_Q = COL_MQ + MEM_WIDTH
COL_K = COL_Q + ATTN_WIDTH
COL_V = COL_K + KV_WIDTH
IN_WIDTH = COL_V + KV_WIDTH

_OFF_Q = FOURIER_WIDTH
_OFF_K = _OFF_Q + ATTN_WIDTH
_OFF_V = _OFF_K + KV_WIDTH
_OFF_MQ = _OFF_V + KV_WIDTH
_OFF_GATE = _OFF_MQ + MEM_WIDTH

LANES = 128
PACK_ROWS = D_MODEL // (2 * LANES)
VMEM_CAPACITY_V7X = 64 * 1024 * 1024
VMEM_LIMIT = VMEM_CAPACITY_V7X // 8 * 7

TM_IN, TN_IN = 512, 1536
GATE_SLICES = 3
TS_F = 512
TQ_MEM = 512
TM_MERGE = 512
TM_DISP = 256
MAIN_E, SUB_E, TF_E = 1024, 128, 256
OVER_E = SUB_E
GROUP_E = MAIN_E + OVER_E
NSUB_E = GROUP_E // SUB_E
NF_E = D_FF // TF_E
DOWN_E = 512
_EXPERTS_WINDOWS = 2 * (2 * GROUP_E * D_MODEL * 2 + (D_MODEL * 2 * TF_E + TF_E * D_MODEL) * 4)
_EXPERTS_SCRATCH = (GROUP_E * D_MODEL * 2 + D_MODEL * 2 * TF_E * 2 + D_FF * D_MODEL * 2
                    + 2 * GROUP_E * 2 * TF_E * 4 + NF_E * GROUP_E * TF_E * 2)
EXPERTS_VMEM_LIMIT = _EXPERTS_WINDOWS + _EXPERTS_SCRATCH + 4 * 1024 * 1024
TM_COMB = 256

BF16 = jnp.bfloat16
F32 = jnp.float32
U32 = jnp.uint32


def _cparams(sem, vmem_limit=VMEM_LIMIT):
    return pltpu.CompilerParams(dimension_semantics=sem, vmem_limit_bytes=vmem_limit)


def _bf16_bits(v):
    return pltpu.bitcast(v.astype(BF16).astype(F32), U32)


def _pack_rows(v, out_ref, first_row, rows):
    for r in range(PACK_ROWS):
        lo = _bf16_bits(v[:, (2 * r) * LANES:(2 * r + 1) * LANES])
        hi = _bf16_bits(v[:, (2 * r + 1) * LANES:(2 * r + 2) * LANES])
        out_ref[pl.ds(first_row * PACK_ROWS + r, rows, stride=PACK_ROWS), :] = lax.shift_right_logical(lo, U32(16)) | hi


def _unpack_rows(ref, first_row, rows):
    chunks = []
    for r in range(PACK_ROWS):
        w = ref[pl.ds(first_row * PACK_ROWS + r, rows, stride=PACK_ROWS), :]
        chunks.append(pltpu.bitcast(lax.shift_left(w, U32(16)), F32))
        chunks.append(pltpu.bitcast(w & U32(0xFFFF0000), F32))
    return chunks


def _inproj_body(x_ref, g_ref, w_ref, o_ref, h_ref, *, n_gate_tiles):
    j = pl.program_id(1)

    @pl.when(j == 0)
    def _norm():
        x = x_ref[...]
        inv = lax.rsqrt(jnp.mean(x * x, axis=-1, keepdims=True) + RMS_EPS)
        h_ref[...] = (x * inv * g_ref[...]).astype(BF16)

    @pl.when(j < n_gate_tiles)
    def _gate():
        width = w_ref.shape[1] // GATE_SLICES
        for c in range(GATE_SLICES):
            cols = slice(c * width, (c + 1) * width)
            acc = jnp.dot(h_ref[...], w_ref[:, cols], preferred_element_type=F32)
            o_ref[:, cols] = (1.0 / (1.0 + jnp.exp(-acc))).astype(o_ref.dtype)

    @pl.when(j >= n_gate_tiles)
    def _plain():
        o_ref[...] = jnp.dot(h_ref[...], w_ref[...], preferred_element_type=F32).astype(o_ref.dtype)


def _inproj(x2d, g, w):
    t = x2d.shape[0]
    return pl.pallas_call(
        functools.partial(_inproj_body, n_gate_tiles=GATE_WIDTH // TN_IN),
        grid=(t // TM_IN, IN_WIDTH // TN_IN),
        in_specs=[
            pl.BlockSpec((TM_IN, D_MODEL), lambda i, j: (i, 0)),
            pl.BlockSpec((1, D_MODEL), lambda i, j: (0, 0)),
            pl.BlockSpec((D_MODEL, TN_IN), lambda i, j: (0, j)),
        ],
        out_specs=pl.BlockSpec((TM_IN, TN_IN), lambda i, j: (i, j)),
        out_shape=jax.ShapeDtypeStruct((t, IN_WIDTH), BF16),
        scratch_shapes=[pltpu.VMEM((TM_IN, D_MODEL), BF16)],
        compiler_params=_cparams(("parallel", "arbitrary")),
        name="inproj",
    )(x2d, g, w)


def _fourier_body(f_ref, cc_ref, sc_ref, cs_ref, wfo_ref, o_ref, z_ref, *, seq):
    st = pl.program_id(1)

    @pl.when(st == 0)
    def _channel_dft():
        for g in range(FOURIER_GROUPS):
            cols = slice(g * FOURIER_GROUP_DIM, (g + 1) * FOURIER_GROUP_DIM)
            xg = f_ref[:, cols]
            z_ref[0:seq, cols] = jnp.dot(xg, cc_ref[...], preferred_element_type=F32).astype(BF16)
            z_ref[seq:2 * seq, cols] = jnp.dot(xg, sc_ref[...], preferred_element_type=F32).astype(BF16)

    y = jnp.dot(cs_ref[...], z_ref[...], preferred_element_type=F32)
    o_ref[...] = jnp.dot(y.astype(BF16), wfo_ref[...], preferred_element_type=F32).astype(o_ref.dtype)


def _dft_tables(seq):
    c = FOURIER_GROUP_DIM
    kc = (np.arange(c)[:, None] * np.arange(c)[None, :]) % c
    ac = 2.0 * np.pi * kc / c
    ks = (np.arange(seq)[:, None] * np.arange(seq)[None, :]) % seq
    a_s = 2.0 * np.pi * ks / seq
    cc = (np.cos(ac) / np.sqrt(c)).astype(np.float32)
    sc = (np.sin(ac) / np.sqrt(c)).astype(np.float32)
    cs = np.concatenate([np.cos(a_s), -np.sin(a_s)], axis=1) / np.sqrt(seq)
    return cc, sc, cs.astype(np.float32)


def _fourier(proj, w_fo, batch, seq):
    cc, sc, cs = _dft_tables(seq)
    cc = jnp.asarray(cc).astype(BF16)
    sc = jnp.asarray(sc).astype(BF16)
    cs = jnp.asarray(cs).astype(BF16)
    t = batch * seq
    return pl.pallas_call(
        functools.partial(_fourier_body, seq=seq),
        grid=(batch, seq // TS_F),
        in_specs=[
            pl.BlockSpec((seq, FOURIER_WIDTH), lambda b, s: (b, COL_F // FOURIER_WIDTH)),
            pl.BlockSpec((FOURIER_GROUP_DIM, FOURIER_GROUP_DIM), lambda b, s: (0, 0)),
            pl.BlockSpec((FOURIER_GROUP_DIM, FOURIER_GROUP_DIM), lambda b, s: (0, 0)),
            pl.BlockSpec((TS_F, 2 * seq), lambda b, s: (s, 0)),
            pl.BlockSpec((FOURIER_WIDTH, D_MODEL), lambda b, s: (0, 0)),
        ],
        out_specs=pl.BlockSpec((TS_F, D_MODEL), lambda b, s: (b * (seq // TS_F) + s, 0)),
        out_shape=jax.ShapeDtypeStruct((t, D_MODEL), BF16),
        scratch_shapes=[pltpu.VMEM((2 * seq, FOURIER_WIDTH), BF16)],
        compiler_params=_cparams(("parallel", "arbitrary")),
        name="fourier",
    )(proj, cc, sc, cs, w_fo)


def _swa_body(sink_ref, q_ref, k_ref, v_ref, bias_ref, o_ref):
    n = pl.program_id(1)
    nb = pl.num_programs(1)
    prev = pl.multiple_of(jnp.maximum(n - 1, 0) * ATT_BLOCK, ATT_BLOCK)
    own = pl.multiple_of(n * ATT_BLOCK, ATT_BLOCK)
    nxt = pl.multiple_of(jnp.minimum(n + 1, nb - 1) * ATT_BLOCK, ATT_BLOCK)

    def band(ref):
        return jnp.concatenate(
            [ref[pl.ds(prev, ATT_BLOCK), :], ref[pl.ds(own, ATT_BLOCK), :], ref[pl.ds(nxt, ATT_BLOCK), :]], axis=0)

    kb = band(k_ref).astype(F32)
    vb = band(v_ref).astype(F32)
    key = lax.broadcasted_iota(jnp.int32, (3 * ATT_BLOCK, 1), 0)
    valid = ((key >= ATT_BLOCK) | (n > 0)) & ((key < 2 * ATT_BLOCK) | (n < nb - 1))
    low = lax.broadcasted_iota(jnp.int32, (1, LANES), 1) < HEAD_DIM
    low_row = lax.broadcasted_iota(jnp.int32, (LANES, 1), 0) < HEAD_DIM

    def halves(x, c, kv_is_low):
        a = x[:, c * LANES:(c + 1) * LANES]
        r = pltpu.roll(a, HEAD_DIM, 1)
        in_low, in_high = (a, r) if kv_is_low else (r, a)
        return jnp.where(low, in_low, 0.0), jnp.where(low, 0.0, in_high)

    pairs = N_HEADS // N_KV_HEADS // 2
    for g in range(N_KV_HEADS):
        k_par = [h.astype(BF16) for h in halves(kb, g // 2, g % 2 == 0)]
        vt_par = [h.T.astype(BF16) for h in halves(vb, g // 2, g % 2 == 0)]
        base = g * pairs * LANES
        qp = jnp.concatenate([q_ref[:, base + p * LANES:base + (p + 1) * LANES] for p in range(pairs)], axis=0)
        qp = qp * jnp.asarray(HEAD_DIM ** -0.5, BF16)
        probs, inv_den = [], []
        for e in range(2):
            s = lax.dot_general(k_par[e], qp, (((1,), (1,)), ((), ())), preferred_element_type=F32)
            s = jnp.where(valid, s + bias_ref[g, e], NEG_INF)
            sink = jnp.concatenate(
                [jnp.full((1, ATT_BLOCK), sink_ref[(g * pairs + p) * 2 + e], F32) for p in range(pairs)], axis=1)
            m = jnp.maximum(jnp.max(s, axis=0, keepdims=True), sink)
            p_un = jnp.exp(s - m)
            den = jnp.sum(p_un, axis=0, keepdims=True) + jnp.exp(sink - m)
            probs.append(p_un.astype(BF16))
            inv_den.append(1.0 / den)
        o_t = (jnp.dot(vt_par[0], probs[0], preferred_element_type=F32)
               + jnp.dot(vt_par[1], probs[1], preferred_element_type=F32))
        o = (o_t * jnp.where(low_row, inv_den[0], inv_den[1])).T
        for p in range(pairs):
            cols = slice(base + p * LANES, base + (p + 1) * LANES)
            o_ref[:, cols] = o[p * ATT_BLOCK:(p + 1) * ATT_BLOCK].astype(o_ref.dtype)


def _t5_bucket(rel):
    nb = NUM_BUCKETS // 2
    max_exact = nb // 2
    ret = (rel > 0).astype(jnp.int32) * nb
    n = jnp.abs(rel)
    nf = jnp.maximum(n, 1).astype(jnp.float32)
    large = max_exact + (jnp.log(nf / max_exact) / np.float32(np.log(MAX_DISTANCE / max_exact))
                         * (nb - max_exact)).astype(jnp.int32)
    large = jnp.minimum(large, nb - 1)
    return ret + jnp.where(n < max_exact, n, large)


def _band_bias(rel_table):
    q_off = jnp.arange(ATT_BLOCK)[:, None]
    k_off = jnp.arange(3 * ATT_BLOCK)[None, :]
    rel = k_off - ATT_BLOCK - q_off
    hot = (_t5_bucket(rel)[..., None] == jnp.arange(NUM_BUCKETS)).astype(F32)
    bias = jnp.einsum("qkb,bh->hqk", hot, rel_table.astype(F32), precision=lax.Precision.HIGHEST)
    bias = jnp.where(jnp.abs(rel) <= WINDOW, bias, NEG_INF)
    pairs = N_HEADS // N_KV_HEADS // 2
    bias = bias.reshape(N_KV_HEADS, pairs, 2, ATT_BLOCK, 3 * ATT_BLOCK).transpose(0, 2, 4, 1, 3)
    return bias.reshape(N_KV_HEADS, 2, 3 * ATT_BLOCK, pairs * ATT_BLOCK)


def _swa(proj, sink, bias, batch, seq):
    nb = seq // ATT_BLOCK
    t = batch * seq
    return pl.pallas_call(
        _swa_body,
        grid_spec=pltpu.PrefetchScalarGridSpec(
            num_scalar_prefetch=0,
            grid=(batch, nb),
            in_specs=[
                pl.BlockSpec(memory_space=pltpu.SMEM),
                pl.BlockSpec((ATT_BLOCK, ATTN_WIDTH), lambda b, n: (b * nb + n, COL_Q // ATTN_WIDTH)),
                pl.BlockSpec((seq, KV_WIDTH), lambda b, n: (b, COL_K // KV_WIDTH)),
                pl.BlockSpec((seq, KV_WIDTH), lambda b, n: (b, COL_V // KV_WIDTH)),
                pl.BlockSpec(bias.shape, lambda b, n: (0, 0, 0, 0)),
            ],
            out_specs=pl.BlockSpec((ATT_BLOCK, ATTN_WIDTH), lambda b, n: (b * nb + n, 0)),
        ),
        out_shape=jax.ShapeDtypeStruct((t, ATTN_WIDTH), BF16),
        compiler_params=_cparams(("parallel", "arbitrary")),
        name="swa",
    )(sink, proj, proj, proj, bias)


def _memkv_body(m_ref, w_ref, o_ref):
    o_ref[...] = jnp.dot(m_ref[...].astype(BF16), w_ref[...], preferred_element_type=F32).astype(o_ref.dtype)


def _memkv(mem2d, w):
    rows = mem2d.shape[0]
    tm, tn = 512, 1024
    return pl.pallas_call(
        _memkv_body,
        grid=(rows // tm, (2 * MEM_WIDTH) // tn),
        in_specs=[
            pl.BlockSpec((tm, D_MODEL), lambda i, j: (i, 0)),
            pl.BlockSpec((D_MODEL, tn), lambda i, j: (0, j)),
        ],
        out_specs=pl.BlockSpec((tm, tn), lambda i, j: (i, j)),
        out_shape=jax.ShapeDtypeStruct((rows, 2 * MEM_WIDTH), BF16),
        compiler_params=_cparams(("parallel", "arbitrary")),
        name="memkv",
    )(mem2d, w)


def _memattn_body(q_ref, k_ref, v_ref, o_ref):
    outs = []
    for h in range(MEM_HEADS):
        cols = slice(h * MEM_HEAD_DIM, (h + 1) * MEM_HEAD_DIM)
        s = lax.dot_general(q_ref[:, cols], k_ref[:, cols], (((1,), (1,)), ((), ())), preferred_element_type=F32)
        s = s * np.float32(MEM_HEAD_DIM ** -0.5)
        m = jnp.max(s, axis=-1, keepdims=True)
        p = jnp.exp(s - m)
        den = jnp.sum(p, axis=-1, keepdims=True)
        o = jnp.dot(p.astype(BF16), v_ref[:, cols], preferred_element_type=F32)
        outs.append(o / den)
    o_ref[...] = jnp.concatenate(outs, axis=1).astype(o_ref.dtype)


def _memattn(proj, kv, batch, seq):
    t = batch * seq
    nq = seq // TQ_MEM
    return pl.pallas_call(
        _memattn_body,
        grid=(batch, nq),
        in_specs=[
            pl.BlockSpec((TQ_MEM, MEM_WIDTH), lambda b, i: (b * nq + i, COL_MQ // MEM_WIDTH)),
            pl.BlockSpec((MEM_LEN, MEM_WIDTH), lambda b, i: (b, 0)),
            pl.BlockSpec((MEM_LEN, MEM_WIDTH), lambda b, i: (b, 1)),
        ],
        out_specs=pl.BlockSpec((TQ_MEM, MEM_WIDTH), lambda b, i: (b * nq + i, 0)),
        out_shape=jax.ShapeDtypeStruct((t, MEM_WIDTH), BF16),
        compiler_params=_cparams(("parallel", "arbitrary")),
        name="memattn",
    )(proj, kv, kv)


def _branches_body(yf_ref, att_ref, mo_ref, g0_ref, g1_ref, g2_ref, wao_ref, wmo_ref, o_ref):
    y_a = jnp.dot(att_ref[...], wao_ref[...], preferred_element_type=F32)
    y_m = jnp.dot(mo_ref[...], wmo_ref[...], preferred_element_type=F32)
    mixed = (g0_ref[...].astype(F32) * yf_ref[...].astype(F32)
             + g1_ref[...].astype(F32) * y_a + g2_ref[...].astype(F32) * y_m)
    o_ref[...] = mixed.astype(o_ref.dtype)


def _branches(y_f, att, mo, proj, w_ao, w_mo):
    t = y_f.shape[0]
    tm = TM_MERGE
    row = lambda i: (i, 0)
    const = lambda i: (0, 0)
    resident = dict(pipeline_mode=pl.Buffered(1))
    return pl.pallas_call(
        _branches_body,
        grid=(t // tm,),
        in_specs=[
            pl.BlockSpec((tm, D_MODEL), row),
            pl.BlockSpec((tm, ATTN_WIDTH), row),
            pl.BlockSpec((tm, MEM_WIDTH), row),
            pl.BlockSpec((tm, D_MODEL), lambda i: (i, 0)),
            pl.BlockSpec((tm, D_MODEL), lambda i: (i, 1)),
            pl.BlockSpec((tm, D_MODEL), lambda i: (i, 2)),
            pl.BlockSpec((ATTN_WIDTH, D_MODEL), const, **resident),
            pl.BlockSpec((MEM_WIDTH, D_MODEL), const, **resident),
        ],
        out_specs=pl.BlockSpec((tm, D_MODEL), row),
        out_shape=jax.ShapeDtypeStruct((t, D_MODEL), BF16),
        compiler_params=_cparams(("parallel",)),
        name="branches",
    )(y_f, att, mo, proj, proj, proj, w_ao, w_mo)


def _merge_body(mixed_ref, x_ref, wo_ref, n2_ref, wr_ref, br_ref,
                x1_ref, h2s_ref, te_ref, tp_ref, tw_ref, cnt_ref, carry_ref):
    i = pl.program_id(0)
    tm = x_ref.shape[0]

    @pl.when(i == 0)
    def _init():
        carry_ref[...] = jnp.zeros_like(carry_ref)

    x1 = x_ref[...] + jnp.dot(mixed_ref[...], wo_ref[...], preferred_element_type=F32)
    x1_ref[...] = x1

    inv = lax.rsqrt(jnp.mean(x1 * x1, axis=-1, keepdims=True) + RMS_EPS)
    h2 = x1 * inv * n2_ref[...]
    _pack_rows(h2, h2s_ref, 0, tm)

    h_hi = h2.astype(BF16)
    h_lo = (h2 - h_hi.astype(F32)).astype(BF16)
    w_r = wr_ref[...]
    w_hi = w_r.astype(BF16)
    w_lo = (w_r - w_hi.astype(F32)).astype(BF16)
    part = jnp.dot(h_hi, jnp.concatenate([w_hi, w_lo], axis=1), preferred_element_type=F32)
    logits = (part[:, :N_EXPERTS] + part[:, N_EXPERTS:]
              + jnp.dot(h_lo, w_hi, preferred_element_type=F32) + br_ref[...])
    lane = lax.broadcasted_iota(jnp.int32, (tm, N_EXPERTS), 1)
    vals, idxs, hots = [], [], []
    rest = logits
    for _ in range(TOP_K):
        mk = jnp.max(rest, axis=-1, keepdims=True)
        ik = jnp.min(jnp.where(rest == mk, lane, N_EXPERTS), axis=-1, keepdims=True)
        hot = lane == ik
        vals.append(mk)
        idxs.append(ik)
        hots.append(hot)
        rest = jnp.where(hot, -jnp.inf, rest)
    exps = [jnp.exp(v - vals[0]) for v in vals]
    den = exps[0] + exps[1] + exps[2] + exps[3]

    sel = (hots[0] | hots[1] | hots[2] | hots[3]).astype(F32)
    r_i = lax.broadcasted_iota(jnp.int32, (tm, tm), 0)
    c_i = lax.broadcasted_iota(jnp.int32, (tm, tm), 1)
    tri = (c_i < r_i).astype(BF16)
    pos = jnp.dot(tri, sel.astype(BF16), preferred_element_type=F32) + carry_ref[...]
    poss = [jnp.sum(jnp.where(hot, pos, 0.0), axis=-1, keepdims=True) for hot in hots]
    carry = carry_ref[...] + jnp.sum(sel, axis=0, keepdims=True)
    carry_ref[...] = carry

    te_ref[...] = jnp.concatenate(idxs, axis=1)
    tp_ref[...] = jnp.concatenate(poss, axis=1).astype(jnp.int32)
    tw_ref[...] = jnp.concatenate([e / den for e in exps], axis=1)
    cnt_ref[...] = carry.astype(jnp.int32)


def _merge(mixed, x2d, w_o, n2, w_r, b_r):
    t = x2d.shape[0]
    tm = TM_MERGE
    row = lambda i: (i, 0)
    const = lambda i: (0, 0)
    resident = dict(pipeline_mode=pl.Buffered(1))
    return pl.pallas_call(
        _merge_body,
        grid=(t // tm,),
        in_specs=[
            pl.BlockSpec((tm, D_MODEL), row),
            pl.BlockSpec((tm, D_MODEL), row),
            pl.BlockSpec((D_MODEL, D_MODEL), const, **resident),
            pl.BlockSpec((1, D_MODEL), const),
            pl.BlockSpec((D_MODEL, N_EXPERTS), const),
            pl.BlockSpec((1, N_EXPERTS), const),
        ],
        out_specs=[
            pl.BlockSpec((tm, D_MODEL), row),
            pl.BlockSpec((tm * PACK_ROWS, LANES), row),
            pl.BlockSpec((tm, TOP_K), row),
            pl.BlockSpec((tm, TOP_K), row),
            pl.BlockSpec((tm, TOP_K), row),
            pl.BlockSpec((1, N_EXPERTS), const),
        ],
        out_shape=[
            jax.ShapeDtypeStruct((t, D_MODEL), F32),
            jax.ShapeDtypeStruct((t * PACK_ROWS, LANES), jnp.uint32),
            jax.ShapeDtypeStruct((t, TOP_K), jnp.int32),
            jax.ShapeDtypeStruct((t, TOP_K), jnp.int32),
            jax.ShapeDtypeStruct((t, TOP_K), F32),
            jax.ShapeDtypeStruct((1, N_EXPERTS), jnp.int32),
        ],
        scratch_shapes=[pltpu.VMEM((1, N_EXPERTS), F32)],
        compiler_params=_cparams(("arbitrary",)),
        name="merge",
    )(mixed, x2d, w_o, n2, w_r, b_r)


def _row_slice(ref, row, n_rows=1):
    start = pl.multiple_of(row * PACK_ROWS, PACK_ROWS)
    return ref.at[pl.ds(start, n_rows * PACK_ROWS), :]


def _dispatch_body(zstart_ref, dest_ref, h_ref, xs_ref, zero_ref, zsem, sem):
    i = pl.program_id(0)
    tm = h_ref.shape[0] // PACK_ROWS

    @pl.when(i == 0)
    def _zero_tails():
        zero_ref[...] = jnp.zeros_like(zero_ref)

        def tail_copy(e):
            return pltpu.make_async_copy(zero_ref, _row_slice(xs_ref, zstart_ref[e], SUB_E), zsem)

        def start(e, c):
            @pl.when(zstart_ref[e] >= 0)
            def _():
                tail_copy(e).start()
            return c

        def wait(e, c):
            @pl.when(zstart_ref[e] >= 0)
            def _():
                tail_copy(e).wait()
            return c

        lax.fori_loop(0, zstart_ref.shape[0], start, 0)
        lax.fori_loop(0, zstart_ref.shape[0], wait, 0)

    def scatter(t, c):
        for k in range(TOP_K):
            d = dest_ref[t * TOP_K + k]
            pltpu.make_async_copy(_row_slice(h_ref, t), _row_slice(xs_ref, d), sem).start(priority=k % 2)
        return c

    lax.fori_loop(0, tm, scatter, 0)
    for _ in range(TOP_K):
        pltpu.make_async_copy(h_ref, _row_slice(xs_ref, 0, tm), sem).wait()


def _dispatch(zstart, dest_flat, h2s, n_slots):
    t = h2s.shape[0] // PACK_ROWS
    tm = TM_DISP
    return pl.pallas_call(
        _dispatch_body,
        grid_spec=pltpu.PrefetchScalarGridSpec(
            num_scalar_prefetch=1,
            grid=(t // tm,),
            in_specs=[
                pl.BlockSpec((tm * TOP_K,), lambda i, z: (i,), memory_space=pltpu.SMEM),
                pl.BlockSpec((tm * PACK_ROWS, LANES), lambda i, z: (i, 0)),
            ],
            out_specs=pl.BlockSpec(memory_space=pl.ANY),
            scratch_shapes=[
                pltpu.VMEM((SUB_E * PACK_ROWS, LANES), U32),
                pltpu.SemaphoreType.DMA(()),
                pltpu.SemaphoreType.DMA(()),
            ],
        ),
        out_shape=jax.ShapeDtypeStruct((n_slots * PACK_ROWS, LANES), U32),
        compiler_params=_cparams(("arbitrary",)),
        name="dispatch",
    )(zstart, dest_flat, h2s)


def _experts_body(ge_ref, gn_ref, nu_ref, xs_ref, wgu_ref, wd_ref, bgu_ref, bd_ref, o_ref,
                  xb_ref, wgu_bf_ref, wd_bf_ref, gu_even_ref, gu_odd_ref, act_ref):
    g = pl.program_id(0)
    f = pl.program_id(1)
    nsub = gn_ref[g]
    half = TF_E // 2
    gu_bufs = (gu_even_ref, gu_odd_ref)

    @pl.when((nsub == 0) & (f == 0))
    def _unused_group():
        o_ref[...] = jnp.zeros_like(o_ref)

    def load_rows():
        for r in range(PACK_ROWS):
            w = xs_ref[pl.ds(r, GROUP_E, stride=PACK_ROWS), :]
            lo = pltpu.bitcast(lax.shift_left(w, U32(16)), F32)
            hi = pltpu.bitcast(w & U32(0xFFFF0000), F32)
            xb_ref[:, (2 * r) * LANES:(2 * r + 1) * LANES] = lo.astype(BF16)
            xb_ref[:, (2 * r + 1) * LANES:(2 * r + 2) * LANES] = hi.astype(BF16)

    def convert_weights():
        wgu_bf_ref[...] = wgu_ref[...].astype(BF16)
        lo = _bf16_bits(wd_ref[0:half, :])
        hi = _bf16_bits(wd_ref[half:TF_E, :])
        chunk_rows = pl.ds(pl.multiple_of(f * TF_E, TF_E), TF_E)
        wd_bf_ref[chunk_rows, :] = pltpu.bitcast(lax.shift_right_logical(lo, U32(16)) | hi, BF16)

    def up_matmul(rows, parity):
        gu_bufs[parity][rows, :] = (jnp.dot(xb_ref[rows, :], wgu_bf_ref[...], preferred_element_type=F32)
                                    + bgu_ref[...])

    def activation(rows, parity):
        gu = gu_bufs[1 - parity][rows, :]
        gate = jnp.minimum(gu, SWIGLU_LIMIT)
        glu = gate * (1.0 / (1.0 + jnp.exp(-gate * SWIGLU_ALPHA)))
        up1 = jnp.clip(gu, -SWIGLU_LIMIT, SWIGLU_LIMIT) + 1.0
        prod = glu * pltpu.roll(up1, 2 * TF_E - 1, 1)
        even = (lax.broadcasted_iota(jnp.int32, (1, TF_E), 1) & 1) == 0
        act_ref[f - 1, rows, :] = jnp.where(even, prod[:, :TF_E], pltpu.roll(prod[:, TF_E:], 1, 1)).astype(BF16)

    def down_and_store(first_row, n_rows):
        rows = pl.ds(first_row, n_rows)
        act = jnp.concatenate([act_ref[c, rows, :] for c in range(NF_E)], axis=1)
        y = jnp.dot(act, wd_bf_ref[...], preferred_element_type=F32) + bd_ref[...]
        _pack_rows(y, o_ref, first_row, n_rows)

    def over_rows(parity, first, last):
        @pl.when(nsub > NSUB_E // 2)
        def _mostly_full():
            rows = pl.ds(0, MAIN_E)
            if first:
                load_rows()
            if not last:
                convert_weights()
                up_matmul(rows, parity)
            if not first:
                activation(rows, parity)
            if last:
                for s in range(MAIN_E // DOWN_E):
                    down_and_store(s * DOWN_E, DOWN_E)

            @pl.when(nsub == NSUB_E)
            def _overflow_rows():
                over = pl.ds(MAIN_E, OVER_E)
                if not last:
                    up_matmul(over, parity)
                if not first:
                    activation(over, parity)
                if last:
                    down_and_store(MAIN_E, OVER_E)

            if last:
                @pl.when(nsub < NSUB_E)
                def _no_overflow_rows():
                    o_ref[pl.ds(MAIN_E * PACK_ROWS, OVER_E * PACK_ROWS), :] = jnp.zeros(
                        (OVER_E * PACK_ROWS, LANES), o_ref.dtype)

        @pl.when(nsub <= NSUB_E // 2)
        def _few_rows():
            if first:
                load_rows()
            if not last:
                convert_weights()
            if last:
                o_ref[...] = jnp.zeros_like(o_ref)

            def body(s, c):
                first_row = pl.multiple_of(s * SUB_E, SUB_E)
                rows = pl.ds(first_row, SUB_E)
                if not last:
                    up_matmul(rows, parity)
                if not first:
                    activation(rows, parity)
                if last:
                    down_and_store(first_row, SUB_E)
                return c
            lax.fori_loop(0, nsub, body, 0)

    @pl.when(nsub > 0)
    def _group():
        @pl.when(f == 0)
        def _first_chunk():
            over_rows(0, True, False)

        for parity in range(2):
            @pl.when((f > 0) & (f < NF_E) & (f % 2 == parity))
            def _middle_chunk():
                over_rows(parity, False, False)

        @pl.when(f == NF_E)
        def _drain():
            over_rows(NF_E % 2, False, True)


def _experts(group_e, group_nsub, n_used, xs, w_gu, w_dn, b_gu, b_dn, layer, n_slots):
    ngrp = n_slots // GROUP_E
    nf = NF_E

    def fchunk(g, f, nu):
        return jnp.where(g < nu[0], jnp.clip(f, 0, nf - 1), nf - 1)

    return pl.pallas_call(
        _experts_body,
        grid_spec=pltpu.PrefetchScalarGridSpec(
            num_scalar_prefetch=3,
            grid=(ngrp, nf + 1),
            in_specs=[
                pl.BlockSpec((GROUP_E * PACK_ROWS, LANES), lambda g, f, ge, gn, nu: (jnp.minimum(g, nu[0] - 1), 0)),
                pl.BlockSpec((None, None, D_MODEL, 2 * TF_E),
                             lambda g, f, ge, gn, nu: (layer, ge[g], 0, fchunk(g, f, nu))),
                pl.BlockSpec((None, None, TF_E, D_MODEL),
                             lambda g, f, ge, gn, nu: (layer, ge[g], fchunk(g, f, nu), 0)),
                pl.BlockSpec((None, None, 1, 2 * TF_E),
                             lambda g, f, ge, gn, nu: (layer, ge[g], 0, fchunk(g, f, nu))),
                pl.BlockSpec((None, None, 1, D_MODEL), lambda g, f, ge, gn, nu: (layer, ge[g], 0, 0)),
            ],
            out_specs=pl.BlockSpec((GROUP_E * PACK_ROWS, LANES), lambda g, f, ge, gn, nu: (g, 0)),
            scratch_shapes=[
                pltpu.VMEM((GROUP_E, D_MODEL), BF16),
                pltpu.VMEM((D_MODEL, 2 * TF_E), BF16),
                pltpu.VMEM((D_FF, D_MODEL), BF16),
                pltpu.VMEM((GROUP_E, 2 * TF_E), F32),
                pltpu.VMEM((GROUP_E, 2 * TF_E), F32),
                pltpu.VMEM((NF_E, GROUP_E, TF_E), BF16),
            ],
        ),
        out_shape=jax.ShapeDtypeStruct((n_slots * PACK_ROWS, LANES), U32),
        compiler_params=_cparams(("arbitrary", "arbitrary"), EXPERTS_VMEM_LIMIT),
        name="experts",
    )(group_e, group_nsub, n_used, xs, w_gu, w_dn, b_gu, b_dn)


def _combine_body(dest_ref, dnext_ref, ys_ref, w_ref, x1_ref, g_ref, o_ref, buf_ref, sems):
    i = pl.program_id(0)
    n = pl.num_programs(0)
    tm = x1_ref.shape[0]
    buf_rows = tm * TOP_K
    slot = i % 2

    def issue_token(d_ref, b, t):
        for k in range(TOP_K):
            src = _row_slice(ys_ref, d_ref[t * TOP_K + k])
            dst = _row_slice(buf_ref, b * buf_rows + k * tm + t)
            pltpu.make_async_copy(src, dst, sems.at[b]).start(priority=k % 2)

    def wait_buffer(b):
        dst = _row_slice(buf_ref, b * buf_rows, buf_rows)
        pltpu.make_async_copy(_row_slice(ys_ref, 0, buf_rows), dst, sems.at[b]).wait()

    @pl.when(i == 0)
    def _first_tile():
        def body(t, c):
            issue_token(dest_ref, 0, t)
            return c
        lax.fori_loop(0, tm, body, 0)

    base = slot * buf_rows
    wait_buffer(slot)

    w = w_ref[...]
    wk = [jnp.broadcast_to(w[:, k:k + 1], (tm, LANES)) for k in range(TOP_K)]
    ssq = jnp.zeros((tm, 1), F32)
    ahead = tm // PACK_ROWS
    for r in range(PACK_ROWS):
        for t in range(r * ahead, (r + 1) * ahead):
            issue_token(dnext_ref, 1 - slot, t)
        acc_lo = acc_hi = None
        for k in range(TOP_K):
            word = buf_ref[pl.ds((base + k * tm) * PACK_ROWS + r, tm, stride=PACK_ROWS), :]
            lo = pltpu.bitcast(lax.shift_left(word, U32(16)), F32) * wk[k]
            hi = pltpu.bitcast(word & U32(0xFFFF0000), F32) * wk[k]
            acc_lo = lo if acc_lo is None else acc_lo + lo
            acc_hi = hi if acc_hi is None else acc_hi + hi
        for c, moe in ((2 * r, acc_lo), (2 * r + 1, acc_hi)):
            cols = slice(c * LANES, (c + 1) * LANES)
            v = x1_ref[:, cols] + moe
            o_ref[:, cols] = v
            ssq = ssq + jnp.sum(v * v, axis=-1, keepdims=True)
    inv = lax.rsqrt(ssq * np.float32(1.0 / D_MODEL) + RMS_EPS)
    o_ref[...] = o_ref[...] * inv * g_ref[...]

    @pl.when(i == n - 1)
    def _drain_extra_gather():
        wait_buffer(1 - slot)


def _combine(dest_flat, ys, top_w, x1, g):
    t = x1.shape[0]
    tm = TM_COMB
    nt = t // tm
    return pl.pallas_call(
        _combine_body,
        grid=(nt,),
        in_specs=[
            pl.BlockSpec((tm * TOP_K,), lambda i: (i,), memory_space=pltpu.SMEM),
            pl.BlockSpec((tm * TOP_K,), lambda i: (jnp.minimum(i + 1, nt - 1),), memory_space=pltpu.SMEM),
            pl.BlockSpec(memory_space=pl.ANY),
            pl.BlockSpec((tm, TOP_K), lambda i: (i, 0)),
            pl.BlockSpec((tm, D_MODEL), lambda i: (i, 0)),
            pl.BlockSpec((1, D_MODEL), lambda i: (0, 0)),
        ],
        out_specs=pl.BlockSpec((tm, D_MODEL), lambda i: (i, 0)),
        out_shape=jax.ShapeDtypeStruct((t, D_MODEL), F32),
        scratch_shapes=[pltpu.VMEM((2 * tm * TOP_K * PACK_ROWS, LANES), U32), pltpu.SemaphoreType.DMA((2,))],
        compiler_params=_cparams(("arbitrary",)),
        name="combine",
    )(dest_flat, dest_flat, ys, top_w, x1, g)


def _routing_tables(counts, top_e, top_pos, n_slots):
    i32 = jnp.int32
    ngrp = n_slots // GROUP_E
    experts = jnp.arange(N_EXPERTS, dtype=i32)
    padded = (counts + GROUP_E - 1) // GROUP_E * GROUP_E
    ends = jnp.cumsum(padded)
    starts = ends - padded
    total = ends[-1]
    dest = jnp.sum(jnp.where(top_e[..., None] == experts, starts, 0), axis=-1) + top_pos
    gstart = jnp.arange(ngrp, dtype=i32) * GROUP_E
    used = gstart < total
    of_group = gstart[:, None] >= ends[None, :]
    last_e = jnp.max(jnp.where(padded > 0, experts, 0))
    group_e = jnp.where(used, jnp.minimum(jnp.sum(of_group, axis=1), N_EXPERTS - 1), last_e)
    valid_end = jnp.sum(jnp.where(group_e[:, None] == experts, starts + counts, 0), axis=-1)
    rows = jnp.clip(valid_end - gstart, 0, GROUP_E)
    group_nsub = jnp.where(used, (rows + SUB_E - 1) // SUB_E, 0)
    n_used = (total // GROUP_E).reshape(1)
    sub = jnp.arange(NSUB_E, dtype=i32) * SUB_E
    tail = (starts + counts // SUB_E * SUB_E)[:, None] + sub
    tail = jnp.where(tail < ends[:, None], tail, -1)
    spare = (n_used[0] + experts)[:, None] * GROUP_E + sub
    spare = jnp.where(spare < n_slots, spare, -1)
    zstart = jnp.concatenate([tail.reshape(-1), spare.reshape(-1)])
    return (dest.reshape(-1).astype(i32), group_e.astype(i32), group_nsub.astype(i32), n_used.astype(i32),
            zstart.astype(i32))


def kernel(x, mem, rel_bias_table, norm1_g, w_in, attn_sink, w_mem_kv, w_fourier_out, w_attn_out, w_mem_out,
           w_o, norm2_g, w_router, b_router, w_gate_up, b_gate_up, w_down, b_down, final_g):
    batch, seq, _ = x.shape
    t = batch * seq
    depth = norm1_g.shape[0]
    assert depth == 1, "the combine kernel applies the final RMSNorm, so a single layer is supported"
    x2d = x.reshape(t, D_MODEL)
    mem2d = mem.reshape(batch * MEM_LEN, D_MODEL)
    bias = _band_bias(rel_bias_table)
    n_slots = pl.cdiv(t * TOP_K + N_EXPERTS * (GROUP_E - 1), GROUP_E) * GROUP_E

    for l in range(depth):
        wi = w_in[l].astype(BF16)
        wi = jnp.concatenate([wi[:, _OFF_GATE:], wi[:, :_OFF_Q], wi[:, _OFF_MQ:_OFF_GATE],
                              wi[:, _OFF_Q:_OFF_K], wi[:, _OFF_K:_OFF_V], wi[:, _OFF_V:_OFF_MQ]], axis=1)
        proj = _inproj(x2d, norm1_g[l].reshape(1, D_MODEL), wi)
        y_f = _fourier(proj, w_fourier_out[l].astype(BF16), batch, seq)
        att = _swa(proj, attn_sink[l], bias, batch, seq)
        kv = _memkv(mem2d, w_mem_kv[l].astype(BF16))
        mo = _memattn(proj, kv, batch, seq)
        mixed = _branches(y_f, att, mo, proj, w_attn_out[l].astype(BF16), w_mem_out[l].astype(BF16))
        x1, h2s, top_e, top_pos, top_w, counts = _merge(
            mixed, x2d, w_o[l].astype(BF16),
            norm2_g[l].reshape(1, D_MODEL), w_router[l], b_router[l].reshape(1, N_EXPERTS))
        dest, group_e, group_nsub, n_used, zstart = _routing_tables(counts[0], top_e, top_pos, n_slots)
        xs = _dispatch(zstart, dest, h2s, n_slots)
        ys = _experts(group_e, group_nsub, n_used, xs, w_gate_up, w_down,
                      b_gate_up.reshape(depth, N_EXPERTS, 1, 2 * D_FF), b_down.reshape(depth, N_EXPERTS, 1, D_MODEL),
                      l, n_slots)
        x2d = _combine(dest, ys, top_w, x1, final_g.reshape(1, D_MODEL))
    return x2d.reshape(batch, seq, D_MODEL)
```

```python
import functools

import numpy as np
import jax
import jax.numpy as jnp
from jax import lax
from jax.experimental import pallas as pl
from jax.experimental.pallas import tpu as pltpu

D_MODEL = 2048
MEM_LEN = 256
FOURIER_GROUPS = 4
FOURIER_GROUP_DIM = 256
FOURIER_WIDTH = 1024
N_HEADS = 32
N_KV_HEADS = 4
HEAD_DIM = 64
ATTN_WIDTH = 2048
KV_WIDTH = 256
WINDOW = 128
ATT_BLOCK = 128
MEM_HEADS = 4
MEM_HEAD_DIM = 256
MEM_WIDTH = 1024
N_BRANCHES = 3
NUM_BUCKETS = 32
MAX_DISTANCE = 128
N_EXPERTS = 32
TOP_K = 4
D_FF = 2048
SWIGLU_LIMIT = 7.0
SWIGLU_ALPHA = 1.702
RMS_EPS = 1e-5
NEG_INF = -1e30

GATE_WIDTH = N_BRANCHES * D_MODEL
COL_GATE = 0
COL_F = GATE_WIDTH
COL_MQ = COL_F + FOURIER_WIDTH
COL_Q = COL_MQ + MEM_WIDTH
COL_K = COL_Q + ATTN_WIDTH
COL_V = COL_K + KV_WIDTH
IN_WIDTH = COL_V + KV_WIDTH

_OFF_Q = FOURIER_WIDTH
_OFF_K = _OFF_Q + ATTN_WIDTH
_OFF_V = _OFF_K + KV_WIDTH
_OFF_MQ = _OFF_V + KV_WIDTH
_OFF_GATE = _OFF_MQ + MEM_WIDTH

LANES = 128
PACK_ROWS = D_MODEL // (2 * LANES)
VMEM_CAPACITY_V7X = 64 * 1024 * 1024
VMEM_LIMIT = VMEM_CAPACITY_V7X // 8 * 7

REORDER_WIDTH = 512
TM_IN, TN_IN = 512, 1536
GATE_SLICES = 3
TS_F = 512
TQ_MEM = 512
TM_MERGE = 512
TM_DISP = 256
DISP_UNROLL = 8
MAIN_E, SUB_E, TF_E = 1024, 128, 256
OVER_E = SUB_E
GROUP_E = MAIN_E + OVER_E
NSUB_E = GROUP_E // SUB_E
NF_E = D_FF // TF_E
DOWN_E = 512
_EXPERTS_WINDOWS = 2 * (2 * GROUP_E * D_MODEL * 2 + (D_MODEL * 2 * TF_E + TF_E * D_MODEL) * 4)
_EXPERTS_SCRATCH = (GROUP_E * D_MODEL * 2 + D_MODEL * 2 * TF_E * 2 + D_FF * D_MODEL * 2
                    + 2 * GROUP_E * 2 * TF_E * 4 + NF_E * GROUP_E * TF_E * 2)
EXPERTS_VMEM_LIMIT = _EXPERTS_WINDOWS + _EXPERTS_SCRATCH + 4 * 1024 * 1024
TM_COMB = 256

BF16 = jnp.bfloat16
F32 = jnp.float32
U32 = jnp.uint32


def _cparams(sem, vmem_limit=VMEM_LIMIT):
    return pltpu.CompilerParams(dimension_semantics=sem, vmem_limit_bytes=vmem_limit)


def _bf16_bits(v):
    return pltpu.bitcast(v.astype(BF16).astype(F32), U32)


def _pack_rows(v, out_ref, first_row, rows):
    for r in range(PACK_ROWS):
        lo = _bf16_bits(v[:, (2 * r) * LANES:(2 * r + 1) * LANES])
        hi = _bf16_bits(v[:, (2 * r + 1) * LANES:(2 * r + 2) * LANES])
        out_ref[pl.ds(first_row * PACK_ROWS + r, rows, stride=PACK_ROWS), :] = lax.shift_right_logical(lo, U32(16)) | hi


def _unpack_rows(ref, first_row, rows):
    chunks = []
    for r in range(PACK_ROWS):
        w = ref[pl.ds(first_row * PACK_ROWS + r, rows, stride=PACK_ROWS), :]
        chunks.append(pltpu.bitcast(lax.shift_left(w, U32(16)), F32))
        chunks.append(pltpu.bitcast(w & U32(0xFFFF0000), F32))
    return chunks


def _reorder_body(src_ref, w_ref, o_ref):
    o_ref[...] = w_ref[...].astype(o_ref.dtype)


def _reorder_w_in(w_in, layer):
    width = REORDER_WIDTH
    segments = [(_OFF_GATE, IN_WIDTH), (0, _OFF_Q), (_OFF_MQ, _OFF_GATE), (_OFF_Q, _OFF_K), (_OFF_K, _OFF_MQ)]
    assert all(a % width == 0 and b % width == 0 for a, b in segments)
    src = np.concatenate([np.arange(a // width, b // width) for a, b in segments]).astype(np.int32)
    return pl.pallas_call(
        _reorder_body,
        grid_spec=pltpu.PrefetchScalarGridSpec(
            num_scalar_prefetch=1,
            grid=(IN_WIDTH // width,),
            in_specs=[pl.BlockSpec((None, D_MODEL, width), lambda j, src: (layer, 0, src[j]))],
            out_specs=pl.BlockSpec((D_MODEL, width), lambda j, src: (0, j)),
        ),
        out_shape=jax.ShapeDtypeStruct((D_MODEL, IN_WIDTH), BF16),
        compiler_params=_cparams(("arbitrary",)),
        name="reorder_w_in",
    )(jnp.asarray(src), w_in)


def _inproj_body(x_ref, g_ref, w_ref, o_ref, h_ref, *, n_gate_tiles):
    j = pl.program_id(1)

    @pl.when(j == 0)
    def _norm():
        x = x_ref[...]
        inv = lax.rsqrt(jnp.mean(x * x, axis=-1, keepdims=True) + RMS_EPS)
        h_ref[...] = (x * inv * g_ref[...]).astype(BF16)

    @pl.when(j < n_gate_tiles)
    def _gate():
        width = w_ref.shape[1] // GATE_SLICES
        for c in range(GATE_SLICES):
            cols = slice(c * width, (c + 1) * width)
            acc = jnp.dot(h_ref[...], w_ref[:, cols], preferred_element_type=F32)
            o_ref[:, cols] = (1.0 / (1.0 + jnp.exp(-acc))).astype(o_ref.dtype)

    @pl.when(j >= n_gate_tiles)
    def _plain():
        o_ref[...] = jnp.dot(h_ref[...], w_ref[...], preferred_element_type=F32).astype(o_ref.dtype)


def _inproj(x2d, g, w):
    t = x2d.shape[0]
    return pl.pallas_call(
        functools.partial(_inproj_body, n_gate_tiles=GATE_WIDTH // TN_IN),
        grid=(t // TM_IN, IN_WIDTH // TN_IN),
        in_specs=[
            pl.BlockSpec((TM_IN, D_MODEL), lambda i, j: (i, 0)),
            pl.BlockSpec((1, D_MODEL), lambda i, j: (0, 0)),
            pl.BlockSpec((D_MODEL, TN_IN), lambda i, j: (0, j)),
        ],
        out_specs=pl.BlockSpec((TM_IN, TN_IN), lambda i, j: (i, j)),
        out_shape=jax.ShapeDtypeStruct((t, IN_WIDTH), BF16),
        scratch_shapes=[pltpu.VMEM((TM_IN, D_MODEL), BF16)],
        compiler_params=_cparams(("parallel", "arbitrary")),
        name="inproj",
    )(x2d, g, w)


def _fourier_body(f_ref, cc_ref, sc_ref, cs_ref, wfo_ref, o_ref, z_ref, *, seq):
    st = pl.program_id(1)

    @pl.when(st == 0)
    def _channel_dft():
        for g in range(FOURIER_GROUPS):
            cols = slice(g * FOURIER_GROUP_DIM, (g + 1) * FOURIER_GROUP_DIM)
            xg = f_ref[:, cols]
            z_ref[0:seq, cols] = jnp.dot(xg, cc_ref[...], preferred_element_type=F32).astype(BF16)
            z_ref[seq:2 * seq, cols] = jnp.dot(xg, sc_ref[...], preferred_element_type=F32).astype(BF16)

    y = jnp.dot(cs_ref[...], z_ref[...], preferred_element_type=F32)
    o_ref[...] = jnp.dot(y.astype(BF16), wfo_ref[...], preferred_element_type=F32).astype(o_ref.dtype)


def _dft_tables(seq):
    c = FOURIER_GROUP_DIM
    kc = (np.arange(c)[:, None] * np.arange(c)[None, :]) % c
    ac = 2.0 * np.pi * kc / c
    ks = (np.arange(seq)[:, None] * np.arange(seq)[None, :]) % seq
    a_s = 2.0 * np.pi * ks / seq
    cc = (np.cos(ac) / np.sqrt(c)).astype(np.float32)
    sc = (np.sin(ac) / np.sqrt(c)).astype(np.float32)
    cs = np.concatenate([np.cos(a_s), -np.sin(a_s)], axis=1) / np.sqrt(seq)
    return cc, sc, cs.astype(np.float32)


def _fourier(proj, w_fo, batch, seq):
    cc, sc, cs = _dft_tables(seq)
    cc = jnp.asarray(cc).astype(BF16)
    sc = jnp.asarray(sc).astype(BF16)
    cs = jnp.asarray(cs).astype(BF16)
    t = batch * seq
    return pl.pallas_call(
        functools.partial(_fourier_body, seq=seq),
        grid=(batch, seq // TS_F),
        in_specs=[
            pl.BlockSpec((seq, FOURIER_WIDTH), lambda b, s: (b, COL_F // FOURIER_WIDTH)),
            pl.BlockSpec((FOURIER_GROUP_DIM, FOURIER_GROUP_DIM), lambda b, s: (0, 0)),
            pl.BlockSpec((FOURIER_GROUP_DIM, FOURIER_GROUP_DIM), lambda b, s: (0, 0)),
            pl.BlockSpec((TS_F, 2 * seq), lambda b, s: (s, 0)),
            pl.BlockSpec((FOURIER_WIDTH, D_MODEL), lambda b, s: (0, 0)),
        ],
        out_specs=pl.BlockSpec((TS_F, D_MODEL), lambda b, s: (b * (seq // TS_F) + s, 0)),
        out_shape=jax.ShapeDtypeStruct((t, D_MODEL), BF16),
        scratch_shapes=[pltpu.VMEM((2 * seq, FOURIER_WIDTH), BF16)],
        compiler_params=_cparams(("parallel", "arbitrary")),
        name="fourier",
    )(proj, cc, sc, cs, w_fo)


def _swa_body(sink_ref, q_ref, k_ref, v_ref, bias_ref, o_ref):
    n = pl.program_id(1)
    nb = pl.num_programs(1)
    prev = pl.multiple_of(jnp.maximum(n - 1, 0) * ATT_BLOCK, ATT_BLOCK)
    own = pl.multiple_of(n * ATT_BLOCK, ATT_BLOCK)
    nxt = pl.multiple_of(jnp.minimum(n + 1, nb - 1) * ATT_BLOCK, ATT_BLOCK)

    def band(ref):
        return jnp.concatenate(
            [ref[pl.ds(prev, ATT_BLOCK), :], ref[pl.ds(own, ATT_BLOCK), :], ref[pl.ds(nxt, ATT_BLOCK), :]], axis=0)

    kb = band(k_ref).astype(F32)
    vb = band(v_ref).astype(F32)
    key = lax.broadcasted_iota(jnp.int32, (3 * ATT_BLOCK, 1), 0)
    valid = ((key >= ATT_BLOCK) | (n > 0)) & ((key < 2 * ATT_BLOCK) | (n < nb - 1))
    low = lax.broadcasted_iota(jnp.int32, (1, LANES), 1) < HEAD_DIM
    low_row = lax.broadcasted_iota(jnp.int32, (LANES, 1), 0) < HEAD_DIM

    def halves(x, c, kv_is_low):
        a = x[:, c * LANES:(c + 1) * LANES]
        r = pltpu.roll(a, HEAD_DIM, 1)
        in_low, in_high = (a, r) if kv_is_low else (r, a)
        return jnp.where(low, in_low, 0.0), jnp.where(low, 0.0, in_high)

    pairs = N_HEADS // N_KV_HEADS // 2
    for g in range(N_KV_HEADS):
        k_par = [h.astype(BF16) for h in halves(kb, g // 2, g % 2 == 0)]
        vt_par = [h.T.astype(BF16) for h in halves(vb, g // 2, g % 2 == 0)]
        base = g * pairs * LANES
        qp = jnp.concatenate([q_ref[:, base + p * LANES:base + (p + 1) * LANES] for p in range(pairs)], axis=0)
        qp = qp * jnp.asarray(HEAD_DIM ** -0.5, BF16)
        probs, inv_den = [], []
        for e in range(2):
            s = lax.dot_general(k_par[e], qp, (((1,), (1,)), ((), ())), preferred_element_type=F32)
            s = jnp.where(valid, s + bias_ref[g, e], NEG_INF)
            sink = jnp.concatenate(
                [jnp.full((1, ATT_BLOCK), sink_ref[(g * pairs + p) * 2 + e], F32) for p in range(pairs)], axis=1)
            m = jnp.maximum(jnp.max(s, axis=0, keepdims=True), sink)
            p_un = jnp.exp(s - m)
            den = jnp.sum(p_un, axis=0, keepdims=True) + jnp.exp(sink - m)
            probs.append(p_un.astype(BF16))
            inv_den.append(1.0 / den)
        o_t = (jnp.dot(vt_par[0], probs[0], preferred_element_type=F32)
               + jnp.dot(vt_par[1], probs[1], preferred_element_type=F32))
        o = (o_t * jnp.where(low_row, inv_den[0], inv_den[1])).T
        for p in range(pairs):
            cols = slice(base + p * LANES, base + (p + 1) * LANES)
            o_ref[:, cols] = o[p * ATT_BLOCK:(p + 1) * ATT_BLOCK].astype(o_ref.dtype)


def _t5_bucket(rel):
    nb = NUM_BUCKETS // 2
    max_exact = nb // 2
    ret = (rel > 0).astype(jnp.int32) * nb
    n = jnp.abs(rel)
    nf = jnp.maximum(n, 1).astype(jnp.float32)
    large = max_exact + (jnp.log(nf / max_exact) / np.float32(np.log(MAX_DISTANCE / max_exact))
                         * (nb - max_exact)).astype(jnp.int32)
    large = jnp.minimum(large, nb - 1)
    return ret + jnp.where(n < max_exact, n, large)


def _band_bias(rel_table):
    q_off = jnp.arange(ATT_BLOCK)[:, None]
    k_off = jnp.arange(3 * ATT_BLOCK)[None, :]
    rel = k_off - ATT_BLOCK - q_off
    hot = (_t5_bucket(rel)[..., None] == jnp.arange(NUM_BUCKETS)).astype(F32)
    bias = jnp.einsum("qkb,bh->hqk", hot, rel_table.astype(F32), precision=lax.Precision.HIGHEST)
    bias = jnp.where(jnp.abs(rel) <= WINDOW, bias, NEG_INF)
    pairs = N_HEADS // N_KV_HEADS // 2
    bias = bias.reshape(N_KV_HEADS, pairs, 2, ATT_BLOCK, 3 * ATT_BLOCK).transpose(0, 2, 4, 1, 3)
    return bias.reshape(N_KV_HEADS, 2, 3 * ATT_BLOCK, pairs * ATT_BLOCK)


def _swa(proj, sink, bias, batch, seq):
    nb = seq // ATT_BLOCK
    t = batch * seq
    return pl.pallas_call(
        _swa_body,
        grid_spec=pltpu.PrefetchScalarGridSpec(
            num_scalar_prefetch=0,
            grid=(batch, nb),
            in_specs=[
                pl.BlockSpec(memory_space=pltpu.SMEM),
                pl.BlockSpec((ATT_BLOCK, ATTN_WIDTH), lambda b, n: (b * nb + n, COL_Q // ATTN_WIDTH)),
                pl.BlockSpec((seq, KV_WIDTH), lambda b, n: (b, COL_K // KV_WIDTH)),
                pl.BlockSpec((seq, KV_WIDTH), lambda b, n: (b, COL_V // KV_WIDTH)),
                pl.BlockSpec(bias.shape, lambda b, n: (0, 0, 0, 0)),
            ],
            out_specs=pl.BlockSpec((ATT_BLOCK, ATTN_WIDTH), lambda b, n: (b * nb + n, 0)),
        ),
        out_shape=jax.ShapeDtypeStruct((t, ATTN_WIDTH), BF16),
        compiler_params=_cparams(("parallel", "arbitrary")),
        name="swa",
    )(sink, proj, proj, proj, bias)


def _memkv_body(m_ref, w_ref, o_ref):
    o_ref[...] = jnp.dot(m_ref[...].astype(BF16), w_ref[...], preferred_element_type=F32).astype(o_ref.dtype)


def _memkv(mem2d, w):
    rows = mem2d.shape[0]
    tm, tn = 512, 1024
    return pl.pallas_call(
        _memkv_body,
        grid=(rows // tm, (2 * MEM_WIDTH) // tn),
        in_specs=[
            pl.BlockSpec((tm, D_MODEL), lambda i, j: (i, 0)),
            pl.BlockSpec((D_MODEL, tn), lambda i, j: (0, j)),
        ],
        out_specs=pl.BlockSpec((tm, tn), lambda i, j: (i, j)),
        out_shape=jax.ShapeDtypeStruct((rows, 2 * MEM_WIDTH), BF16),
        compiler_params=_cparams(("parallel", "arbitrary")),
        name="memkv",
    )(mem2d, w)


def _memattn_body(q_ref, k_ref, v_ref, o_ref):
    outs = []
    for h in range(MEM_HEADS):
        cols = slice(h * MEM_HEAD_DIM, (h + 1) * MEM_HEAD_DIM)
        s = lax.dot_general(q_ref[:, cols], k_ref[:, cols], (((1,), (1,)), ((), ())), preferred_element_type=F32)
        s = s * np.float32(MEM_HEAD_DIM ** -0.5)
        m = jnp.max(s, axis=-1, keepdims=True)
        p = jnp.exp(s - m)
        den = jnp.sum(p, axis=-1, keepdims=True)
        o = jnp.dot(p.astype(BF16), v_ref[:, cols], preferred_element_type=F32)
        outs.append(o / den)
    o_ref[...] = jnp.concatenate(outs, axis=1).astype(o_ref.dtype)


def _memattn(proj, kv, batch, seq):
    t = batch * seq
    nq = seq // TQ_MEM
    return pl.pallas_call(
        _memattn_body,
        grid=(batch, nq),
        in_specs=[
            pl.BlockSpec((TQ_MEM, MEM_WIDTH), lambda b, i: (b * nq + i, COL_MQ // MEM_WIDTH)),
            pl.BlockSpec((MEM_LEN, MEM_WIDTH), lambda b, i: (b, 0)),
            pl.BlockSpec((MEM_LEN, MEM_WIDTH), lambda b, i: (b, 1)),
        ],
        out_specs=pl.BlockSpec((TQ_MEM, MEM_WIDTH), lambda b, i: (b * nq + i, 0)),
        out_shape=jax.ShapeDtypeStruct((t, MEM_WIDTH), BF16),
        compiler_params=_cparams(("parallel", "arbitrary")),
        name="memattn",
    )(proj, kv, kv)


def _branches_body(yf_ref, att_ref, mo_ref, g0_ref, g1_ref, g2_ref, wao_ref, wmo_ref, o_ref):
    y_a = jnp.dot(att_ref[...], wao_ref[...], preferred_element_type=F32)
    y_m = jnp.dot(mo_ref[...], wmo_ref[...], preferred_element_type=F32)
    mixed = (g0_ref[...].astype(F32) * yf_ref[...].astype(F32)
             + g1_ref[...].astype(F32) * y_a + g2_ref[...].astype(F32) * y_m)
    o_ref[...] = mixed.astype(o_ref.dtype)


def _branches(y_f, att, mo, proj, w_ao, w_mo):
    t = y_f.shape[0]
    tm = TM_MERGE
    row = lambda i: (i, 0)
    const = lambda i: (0, 0)
    resident = dict(pipeline_mode=pl.Buffered(1))
    return pl.pallas_call(
        _branches_body,
        grid=(t // tm,),
        in_specs=[
            pl.BlockSpec((tm, D_MODEL), row),
            pl.BlockSpec((tm, ATTN_WIDTH), row),
            pl.BlockSpec((tm, MEM_WIDTH), row),
            pl.BlockSpec((tm, D_MODEL), lambda i: (i, 0)),
            pl.BlockSpec((tm, D_MODEL), lambda i: (i, 1)),
            pl.BlockSpec((tm, D_MODEL), lambda i: (i, 2)),
            pl.BlockSpec((ATTN_WIDTH, D_MODEL), const, **resident),
            pl.BlockSpec((MEM_WIDTH, D_MODEL), const, **resident),
        ],
        out_specs=pl.BlockSpec((tm, D_MODEL), row),
        out_shape=jax.ShapeDtypeStruct((t, D_MODEL), BF16),
        compiler_params=_cparams(("parallel",)),
        name="branches",
    )(y_f, att, mo, proj, proj, proj, w_ao, w_mo)


def _merge_body(mixed_ref, x_ref, wo_ref, n2_ref, wr_ref, br_ref,
                x1_ref, h2s_ref, te_ref, tp_ref, tw_ref, cnt_ref, carry_ref):
    i = pl.program_id(0)
    tm = x_ref.shape[0]

    @pl.when(i == 0)
    def _init():
        carry_ref[...] = jnp.zeros_like(carry_ref)

    x1 = x_ref[...] + jnp.dot(mixed_ref[...], wo_ref[...], preferred_element_type=F32)
    x1_ref[...] = x1

    inv = lax.rsqrt(jnp.mean(x1 * x1, axis=-1, keepdims=True) + RMS_EPS)
    h2 = x1 * inv * n2_ref[...]
    _pack_rows(h2, h2s_ref, 0, tm)

    h_hi = h2.astype(BF16)
    h_lo = (h2 - h_hi.astype(F32)).astype(BF16)
    w_r = wr_ref[...]
    w_hi = w_r.astype(BF16)
    w_lo = (w_r - w_hi.astype(F32)).astype(BF16)
    part = jnp.dot(h_hi, jnp.concatenate([w_hi, w_lo], axis=1), preferred_element_type=F32)
    logits = (part[:, :N_EXPERTS] + part[:, N_EXPERTS:]
              + jnp.dot(h_lo, w_hi, preferred_element_type=F32) + br_ref[...])
    lane = lax.broadcasted_iota(jnp.int32, (tm, N_EXPERTS), 1)
    vals, idxs, hots = [], [], []
    rest = logits
    for _ in range(TOP_K):
        mk = jnp.max(rest, axis=-1, keepdims=True)
        ik = jnp.min(jnp.where(rest == mk, lane, N_EXPERTS), axis=-1, keepdims=True)
        hot = lane == ik
        vals.append(mk)
        idxs.append(ik)
        hots.append(hot)
        rest = jnp.where(hot, -jnp.inf, rest)
    exps = [jnp.exp(v - vals[0]) for v in vals]
    den = exps[0] + exps[1] + exps[2] + exps[3]

    sel = (hots[0] | hots[1] | hots[2] | hots[3]).astype(F32)
    r_i = lax.broadcasted_iota(jnp.int32, (tm, tm), 0)
    c_i = lax.broadcasted_iota(jnp.int32, (tm, tm), 1)
    tri = (c_i < r_i).astype(BF16)
    pos = jnp.dot(tri, sel.astype(BF16), preferred_element_type=F32) + carry_ref[...]
    poss = [jnp.sum(jnp.where(hot, pos, 0.0), axis=-1, keepdims=True) for hot in hots]
    carry = carry_ref[...] + jnp.sum(sel, axis=0, keepdims=True)
    carry_ref[...] = carry

    te_ref[...] = jnp.concatenate(idxs, axis=1)
    tp_ref[...] = jnp.concatenate(poss, axis=1).astype(jnp.int32)
    tw_ref[...] = jnp.concatenate([e / den for e in exps], axis=1)
    cnt_ref[...] = carry.astype(jnp.int32)


def _merge(mixed, x2d, w_o, n2, w_r, b_r):
    t = x2d.shape[0]
    tm = TM_MERGE
    row = lambda i: (i, 0)
    const = lambda i: (0, 0)
    resident = dict(pipeline_mode=pl.Buffered(1))
    return pl.pallas_call(
        _merge_body,
        grid=(t // tm,),
        in_specs=[
            pl.BlockSpec((tm, D_MODEL), row),
            pl.BlockSpec((tm, D_MODEL), row),
            pl.BlockSpec((D_MODEL, D_MODEL), const, **resident),
            pl.BlockSpec((1, D_MODEL), const),
            pl.BlockSpec((D_MODEL, N_EXPERTS), const),
            pl.BlockSpec((1, N_EXPERTS), const),
        ],
        out_specs=[
            pl.BlockSpec((tm, D_MODEL), row),
            pl.BlockSpec((tm * PACK_ROWS, LANES), row),
            pl.BlockSpec((tm, TOP_K), row),
            pl.BlockSpec((tm, TOP_K), row),
            pl.BlockSpec((tm, TOP_K), row),
            pl.BlockSpec((1, N_EXPERTS), const),
        ],
        out_shape=[
            jax.ShapeDtypeStruct((t, D_MODEL), F32),
            jax.ShapeDtypeStruct((t * PACK_ROWS, LANES), jnp.uint32),
            jax.ShapeDtypeStruct((t, TOP_K), jnp.int32),
            jax.ShapeDtypeStruct((t, TOP_K), jnp.int32),
            jax.ShapeDtypeStruct((t, TOP_K), F32),
            jax.ShapeDtypeStruct((1, N_EXPERTS), jnp.int32),
        ],
        scratch_shapes=[pltpu.VMEM((1, N_EXPERTS), F32)],
        compiler_params=_cparams(("arbitrary",)),
        name="merge",
    )(mixed, x2d, w_o, n2, w_r, b_r)


def _row_slice(ref, row, n_rows=1):
    start = pl.multiple_of(row * PACK_ROWS, PACK_ROWS)
    return ref.at[pl.ds(start, n_rows * PACK_ROWS), :]


def _dispatch_body(zstart_ref, dest_ref, h_ref, xs_ref, zero_ref, zsem, sem):
    i = pl.program_id(0)
    tm = h_ref.shape[0] // PACK_ROWS

    @pl.when(i == 0)
    def _zero_tails():
        zero_ref[...] = jnp.zeros_like(zero_ref)

        def tail_copy(e):
            return pltpu.make_async_copy(zero_ref, _row_slice(xs_ref, zstart_ref[e], SUB_E), zsem)

        def start(e, c):
            @pl.when(zstart_ref[e] >= 0)
            def _():
                tail_copy(e).start()
            return c

        def wait(e, c):
            @pl.when(zstart_ref[e] >= 0)
            def _():
                tail_copy(e).wait()
            return c

        lax.fori_loop(0, zstart_ref.shape[0], start, 0)
        lax.fori_loop(0, zstart_ref.shape[0], wait, 0)

    def scatter(block, c):
        for u in range(DISP_UNROLL):
            t = block * DISP_UNROLL + u
            for k in range(TOP_K):
                d = dest_ref[t * TOP_K + k]
                pltpu.make_async_copy(_row_slice(h_ref, t), _row_slice(xs_ref, d), sem).start(priority=k % 2)
        return c

    lax.fori_loop(0, tm // DISP_UNROLL, scatter, 0)
    for _ in range(TOP_K):
        pltpu.make_async_copy(h_ref, _row_slice(xs_ref, 0, tm), sem).wait()


def _dispatch(zstart, dest_flat, h2s, n_slots):
    t = h2s.shape[0] // PACK_ROWS
    tm = TM_DISP
    return pl.pallas_call(
        _dispatch_body,
        grid_spec=pltpu.PrefetchScalarGridSpec(
            num_scalar_prefetch=1,
            grid=(t // tm,),
            in_specs=[
                pl.BlockSpec((tm * TOP_K,), lambda i, z: (i,), memory_space=pltpu.SMEM),
                pl.BlockSpec((tm * PACK_ROWS, LANES), lambda i, z: (i, 0)),
            ],
            out_specs=pl.BlockSpec(memory_space=pl.ANY),
            scratch_shapes=[
                pltpu.VMEM((SUB_E * PACK_ROWS, LANES), U32),
                pltpu.SemaphoreType.DMA(()),
                pltpu.SemaphoreType.DMA(()),
            ],
        ),
        out_shape=jax.ShapeDtypeStruct((n_slots * PACK_ROWS, LANES), U32),
        compiler_params=_cparams(("arbitrary",)),
        name="dispatch",
    )(zstart, dest_flat, h2s)


def _experts_body(ge_ref, gn_ref, nu_ref, xs_ref, wgu_ref, wd_ref, bgu_ref, bd_ref, o_ref,
                  xb_ref, wgu_bf_ref, wd_bf_ref, gu_even_ref, gu_odd_ref, act_ref):
    g = pl.program_id(0)
    f = pl.program_id(1)
    nsub = gn_ref[g]
    half = TF_E // 2
    gu_bufs = (gu_even_ref, gu_odd_ref)

    @pl.when((nsub == 0) & (f == 0))
    def _unused_group():
        o_ref[...] = jnp.zeros_like(o_ref)

    def load_rows():
        for r in range(PACK_ROWS):
            w = xs_ref[pl.ds(r, GROUP_E, stride=PACK_ROWS), :]
            lo = pltpu.bitcast(lax.shift_left(w, U32(16)), F32)
            hi = pltpu.bitcast(w & U32(0xFFFF0000), F32)
            xb_ref[:, (2 * r) * LANES:(2 * r + 1) * LANES] = lo.astype(BF16)
            xb_ref[:, (2 * r + 1) * LANES:(2 * r + 2) * LANES] = hi.astype(BF16)

    def convert_weights():
        wgu_bf_ref[...] = wgu_ref[...].astype(BF16)
        lo = _bf16_bits(wd_ref[0:half, :])
        hi = _bf16_bits(wd_ref[half:TF_E, :])
        chunk_rows = pl.ds(pl.multiple_of(f * TF_E, TF_E), TF_E)
        wd_bf_ref[chunk_rows, :] = pltpu.bitcast(lax.shift_right_logical(lo, U32(16)) | hi, BF16)

    def up_matmul(rows, parity):
        gu_bufs[parity][rows, :] = (jnp.dot(xb_ref[rows, :], wgu_bf_ref[...], preferred_element_type=F32)
                                    + bgu_ref[...])

    def activation(rows, parity):
        gu = gu_bufs[1 - parity][rows, :]
        gate = jnp.minimum(gu, SWIGLU_LIMIT)
        glu = gate * (1.0 / (1.0 + jnp.exp(-gate * SWIGLU_ALPHA)))
        up1 = jnp.clip(gu, -SWIGLU_LIMIT, SWIGLU_LIMIT) + 1.0
        prod = glu * pltpu.roll(up1, 2 * TF_E - 1, 1)
        even = (lax.broadcasted_iota(jnp.int32, (1, TF_E), 1) & 1) == 0
        act_ref[f - 1, rows, :] = jnp.where(even, prod[:, :TF_E], pltpu.roll(prod[:, TF_E:], 1, 1)).astype(BF16)

    def down_and_store(first_row, n_rows):
        rows = pl.ds(first_row, n_rows)
        act = jnp.concatenate([act_ref[c, rows, :] for c in range(NF_E)], axis=1)
        y = jnp.dot(act, wd_bf_ref[...], preferred_element_type=F32) + bd_ref[...]
        _pack_rows(y, o_ref, first_row, n_rows)

    def over_rows(parity, first, last):
        @pl.when(nsub > NSUB_E // 2)
        def _mostly_full():
            rows = pl.ds(0, MAIN_E)
            if first:
                load_rows()
            if not last:
                convert_weights()
                up_matmul(rows, parity)
            if not first:
                activation(rows, parity)
            if last:
                for s in range(MAIN_E // DOWN_E):
                    down_and_store(s * DOWN_E, DOWN_E)

            @pl.when(nsub == NSUB_E)
            def _overflow_rows():
                over = pl.ds(MAIN_E, OVER_E)
                if not last:
                    up_matmul(over, parity)
                if not first:
                    activation(over, parity)
                if last:
                    down_and_store(MAIN_E, OVER_E)

            if last:
                @pl.when(nsub < NSUB_E)
                def _no_overflow_rows():
                    o_ref[pl.ds(MAIN_E * PACK_ROWS, OVER_E * PACK_ROWS), :] = jnp.zeros(
                        (OVER_E * PACK_ROWS, LANES), o_ref.dtype)

        @pl.when(nsub <= NSUB_E // 2)
        def _few_rows():
            if first:
                load_rows()
            if not last:
                convert_weights()
            if last:
                o_ref[...] = jnp.zeros_like(o_ref)

            def body(s, c):
                first_row = pl.multiple_of(s * SUB_E, SUB_E)
                rows = pl.ds(first_row, SUB_E)
                if not last:
                    up_matmul(rows, parity)
                if not first:
                    activation(rows, parity)
                if last:
                    down_and_store(first_row, SUB_E)
                return c
            lax.fori_loop(0, nsub, body, 0)

    @pl.when(nsub > 0)
    def _group():
        @pl.when(f == 0)
        def _first_chunk():
            over_rows(0, True, False)

        for parity in range(2):
            @pl.when((f > 0) & (f < NF_E) & (f % 2 == parity))
            def _middle_chunk():
                over_rows(parity, False, False)

        @pl.when(f == NF_E)
        def _drain():
            over_rows(NF_E % 2, False, True)


def _experts(group_e, group_nsub, n_used, xs, w_gu, w_dn, b_gu, b_dn, layer, n_slots):
    ngrp = n_slots // GROUP_E
    nf = NF_E

    def fchunk(g, f, nu):
        return jnp.where(g < nu[0], jnp.clip(f, 0, nf - 1), nf - 1)

    return pl.pallas_call(
        _experts_body,
        grid_spec=pltpu.PrefetchScalarGridSpec(
            num_scalar_prefetch=3,
            grid=(ngrp, nf + 1),
            in_specs=[
                pl.BlockSpec((GROUP_E * PACK_ROWS, LANES), lambda g, f, ge, gn, nu: (jnp.minimum(g, nu[0] - 1), 0)),
                pl.BlockSpec((None, None, D_MODEL, 2 * TF_E),
                             lambda g, f, ge, gn, nu: (layer, ge[g], 0, fchunk(g, f, nu))),
                pl.BlockSpec((None, None, TF_E, D_MODEL),
                             lambda g, f, ge, gn, nu: (layer, ge[g], fchunk(g, f, nu), 0)),
                pl.BlockSpec((None, None, 1, 2 * TF_E),
                             lambda g, f, ge, gn, nu: (layer, ge[g], 0, fchunk(g, f, nu))),
                pl.BlockSpec((None, None, 1, D_MODEL), lambda g, f, ge, gn, nu: (layer, ge[g], 0, 0)),
            ],
            out_specs=pl.BlockSpec((GROUP_E * PACK_ROWS, LANES), lambda g, f, ge, gn, nu: (g, 0)),
            scratch_shapes=[
                pltpu.VMEM((GROUP_E, D_MODEL), BF16),
                pltpu.VMEM((D_MODEL, 2 * TF_E), BF16),
                pltpu.VMEM((D_FF, D_MODEL), BF16),
                pltpu.VMEM((GROUP_E, 2 * TF_E), F32),
                pltpu.VMEM((GROUP_E, 2 * TF_E), F32),
                pltpu.VMEM((NF_E, GROUP_E, TF_E), BF16),
            ],
        ),
        out_shape=jax.ShapeDtypeStruct((n_slots * PACK_ROWS, LANES), U32),
        compiler_params=_cparams(("arbitrary", "arbitrary"), EXPERTS_VMEM_LIMIT),
        name="experts",
    )(group_e, group_nsub, n_used, xs, w_gu, w_dn, b_gu, b_dn)


def _combine_body(dest_ref, dnext_ref, ys_ref, w_ref, x1_ref, g_ref, o_ref, buf_ref, sems):
    i = pl.program_id(0)
    n = pl.num_programs(0)
    tm = x1_ref.shape[0]
    buf_rows = tm * TOP_K
    slot = i % 2

    def issue_token(d_ref, b, t):
        for k in range(TOP_K):
            src = _row_slice(ys_ref, d_ref[t * TOP_K + k])
            dst = _row_slice(buf_ref, b * buf_rows + k * tm + t)
            pltpu.make_async_copy(src, dst, sems.at[b]).start(priority=k % 2)

    def wait_buffer(b):
        dst = _row_slice(buf_ref, b * buf_rows, buf_rows)
        pltpu.make_async_copy(_row_slice(ys_ref, 0, buf_rows), dst, sems.at[b]).wait()

    @pl.when(i == 0)
    def _first_tile():
        def body(t, c):
            issue_token(dest_ref, 0, t)
            return c
        lax.fori_loop(0, tm, body, 0)

    base = slot * buf_rows
    wait_buffer(slot)

    w = w_ref[...]
    wk = [jnp.broadcast_to(w[:, k:k + 1], (tm, LANES)) for k in range(TOP_K)]
    ssq = jnp.zeros((tm, 1), F32)
    ahead = tm // PACK_ROWS
    for r in range(PACK_ROWS):
        for t in range(r * ahead, (r + 1) * ahead):
            issue_token(dnext_ref, 1 - slot, t)
        acc_lo = acc_hi = None
        for k in range(TOP_K):
            word = buf_ref[pl.ds((base + k * tm) * PACK_ROWS + r, tm, stride=PACK_ROWS), :]
            lo = pltpu.bitcast(lax.shift_left(word, U32(16)), F32) * wk[k]
            hi = pltpu.bitcast(word & U32(0xFFFF0000), F32) * wk[k]
            acc_lo = lo if acc_lo is None else acc_lo + lo
            acc_hi = hi if acc_hi is None else acc_hi + hi
        for c, moe in ((2 * r, acc_lo), (2 * r + 1, acc_hi)):
            cols = slice(c * LANES, (c + 1) * LANES)
            v = x1_ref[:, cols] + moe
            o_ref[:, cols] = v
            ssq = ssq + jnp.sum(v * v, axis=-1, keepdims=True)
    inv = lax.rsqrt(ssq * np.float32(1.0 / D_MODEL) + RMS_EPS)
    o_ref[...] = o_ref[...] * inv * g_ref[...]

    @pl.when(i == n - 1)
    def _drain_extra_gather():
        wait_buffer(1 - slot)


def _combine(dest_flat, ys, top_w, x1, g):
    t = x1.shape[0]
    tm = TM_COMB
    nt = t // tm
    return pl.pallas_call(
        _combine_body,
        grid=(nt,),
        in_specs=[
            pl.BlockSpec((tm * TOP_K,), lambda i: (i,), memory_space=pltpu.SMEM),
            pl.BlockSpec((tm * TOP_K,), lambda i: (jnp.minimum(i + 1, nt - 1),), memory_space=pltpu.SMEM),
            pl.BlockSpec(memory_space=pl.ANY),
            pl.BlockSpec((tm, TOP_K), lambda i: (i, 0)),
            pl.BlockSpec((tm, D_MODEL), lambda i: (i, 0)),
            pl.BlockSpec((1, D_MODEL), lambda i: (0, 0)),
        ],
        out_specs=pl.BlockSpec((tm, D_MODEL), lambda i: (i, 0)),
        out_shape=jax.ShapeDtypeStruct((t, D_MODEL), F32),
        scratch_shapes=[pltpu.VMEM((2 * tm * TOP_K * PACK_ROWS, LANES), U32), pltpu.SemaphoreType.DMA((2,))],
        compiler_params=_cparams(("arbitrary",)),
        name="combine",
    )(dest_flat, dest_flat, ys, top_w, x1, g)


def _routing_tables(counts, top_e, top_pos, n_slots):
    i32 = jnp.int32
    ngrp = n_slots // GROUP_E
    experts = jnp.arange(N_EXPERTS, dtype=i32)
    padded = (counts + GROUP_E - 1) // GROUP_E * GROUP_E
    ends = jnp.cumsum(padded)
    starts = ends - padded
    total = ends[-1]
    dest = jnp.sum(jnp.where(top_e[..., None] == experts, starts, 0), axis=-1) + top_pos
    gstart = jnp.arange(ngrp, dtype=i32) * GROUP_E
    used = gstart < total
    of_group = gstart[:, None] >= ends[None, :]
    last_e = jnp.max(jnp.where(padded > 0, experts, 0))
    group_e = jnp.where(used, jnp.minimum(jnp.sum(of_group, axis=1), N_EXPERTS - 1), last_e)
    valid_end = jnp.sum(jnp.where(group_e[:, None] == experts, starts + counts, 0), axis=-1)
    rows = jnp.clip(valid_end - gstart, 0, GROUP_E)
    group_nsub = jnp.where(used, (rows + SUB_E - 1) // SUB_E, 0)
    n_used = (total // GROUP_E).reshape(1)
    sub = jnp.arange(NSUB_E, dtype=i32) * SUB_E
    tail = (starts + counts // SUB_E * SUB_E)[:, None] + sub
    tail = jnp.where(tail < ends[:, None], tail, -1)
    spare = (n_used[0] + experts)[:, None] * GROUP_E + sub
    spare = jnp.where(spare < n_slots, spare, -1)
    zstart = jnp.concatenate([tail.reshape(-1), spare.reshape(-1)])
    return (dest.reshape(-1).astype(i32), group_e.astype(i32), group_nsub.astype(i32), n_used.astype(i32),
            zstart.astype(i32))


def kernel(x, mem, rel_bias_table, norm1_g, w_in, attn_sink, w_mem_kv, w_fourier_out, w_attn_out, w_mem_out,
           w_o, norm2_g, w_router, b_router, w_gate_up, b_gate_up, w_down, b_down, final_g):
    batch, seq, _ = x.shape
    t = batch * seq
    depth = norm1_g.shape[0]
    assert depth == 1, "the combine kernel applies the final RMSNorm, so a single layer is supported"
    x2d = x.reshape(t, D_MODEL)
    mem2d = mem.reshape(batch * MEM_LEN, D_MODEL)
    bias = _band_bias(rel_bias_table)
    n_slots = pl.cdiv(t * TOP_K + N_EXPERTS * (GROUP_E - 1), GROUP_E) * GROUP_E

    for l in range(depth):
        proj = _inproj(x2d, norm1_g[l].reshape(1, D_MODEL), _reorder_w_in(w_in, l))
        y_f = _fourier(proj, w_fourier_out[l].astype(BF16), batch, seq)
        att = _swa(proj, attn_sink[l], bias, batch, seq)
        kv = _memkv(mem2d, w_mem_kv[l].astype(BF16))
        mo = _memattn(proj, kv, batch, seq)
        mixed = _branches(y_f, att, mo, proj, w_attn_out[l].astype(BF16), w_mem_out[l].astype(BF16))
        x1, h2s, top_e, top_pos, top_w, counts = _merge(
            mixed, x2d, w_o[l].astype(BF16),
            norm2_g[l].reshape(1, D_MODEL), w_router[l], b_router[l].reshape(1, N_EXPERTS))
        dest, group_e, group_nsub, n_used, zstart = _routing_tables(counts[0], top_e, top_pos, n_slots)
        xs = _dispatch(zstart, dest, h2s, n_slots)
        ys = _experts(group_e, group_nsub, n_used, xs, w_gate_up, w_down,
                      b_gate_up.reshape(depth, N_EXPERTS, 1, 2 * D_FF), b_down.reshape(depth, N_EXPERTS, 1, D_MODEL),
                      l, n_slots)
        x2d = _combine(dest, ys, top_w, x1, final_g.reshape(1, D_MODEL))
    return x2d.reshape(batch, seq, D_MODEL)
```

```python
import functools

import numpy as np
import jax
import jax.numpy as jnp
from jax import lax
from jax.experimental import pallas as pl
from jax.experimental.pallas import tpu as pltpu

D_MODEL = 2048
MEM_LEN = 256
FOURIER_GROUPS = 4
FOURIER_GROUP_DIM = 256
FOURIER_WIDTH = 1024
N_HEADS = 32
N_KV_HEADS = 4
HEAD_DIM = 64
ATTN_WIDTH = 2048
KV_WIDTH = 256
WINDOW = 128
ATT_BLOCK = 128
MEM_HEADS = 4
MEM_HEAD_DIM = 256
MEM_WIDTH = 1024
N_BRANCHES = 3
NUM_BUCKETS = 32
MAX_DISTANCE = 128
N_EXPERTS = 32
TOP_K = 4
D_FF = 2048
SWIGLU_LIMIT = 7.0
SWIGLU_ALPHA = 1.702
RMS_EPS = 1e-5
NEG_INF = -1e30

GATE_WIDTH = N_BRANCHES * D_MODEL
COL_GATE = 0
COL_F = GATE_WIDTH
COL_MQ = COL_F + FOURIER_WIDTH
COL_Q = COL_MQ + MEM_WIDTH
COL_K = COL_Q + ATTN_WIDTH
COL_V = COL_K + KV_WIDTH
IN_WIDTH = COL_V + KV_WIDTH

_OFF_Q = FOURIER_WIDTH
_OFF_K = _OFF_Q + ATTN_WIDTH
_OFF_V = _OFF_K + KV_WIDTH
_OFF_MQ = _OFF_V + KV_WIDTH
_OFF_GATE = _OFF_MQ + MEM_WIDTH

LANES = 128
PACK_ROWS = D_MODEL // (2 * LANES)
VMEM_CAPACITY_V7X = 64 * 1024 * 1024
VMEM_LIMIT = VMEM_CAPACITY_V7X // 8 * 7

REORDER_WIDTH = 512
TM_IN, TN_IN = 512, 1536
GATE_SLICES = 3
TS_F = 512
TQ_MEM = 512
TM_MERGE = 512
TM_DISP = 256
DISP_UNROLL = 8
MAIN_E, SUB_E, TF_E = 1024, 128, 256
OVER_E = SUB_E
GROUP_E = MAIN_E + OVER_E
NSUB_E = GROUP_E // SUB_E
NF_E = D_FF // TF_E
DOWN_E = 512
_EXPERTS_WINDOWS = 2 * (2 * GROUP_E * D_MODEL * 2 + (D_MODEL * 2 * TF_E + TF_E * D_MODEL) * 4)
_EXPERTS_SCRATCH = (GROUP_E * D_MODEL * 2 + D_MODEL * 2 * TF_E * 2 + D_FF * D_MODEL * 2
                    + 2 * GROUP_E * 2 * TF_E * 4 + NF_E * GROUP_E * TF_E * 2)
EXPERTS_VMEM_LIMIT = _EXPERTS_WINDOWS + _EXPERTS_SCRATCH + 4 * 1024 * 1024
TM_COMB = 256

BF16 = jnp.bfloat16
F32 = jnp.float32
U32 = jnp.uint32


def _cparams(sem, vmem_limit=VMEM_LIMIT):
    return pltpu.CompilerParams(dimension_semantics=sem, vmem_limit_bytes=vmem_limit)


def _bf16_bits(v):
    return pltpu.bitcast(v.astype(BF16).astype(F32), U32)


def _pack_rows(v, out_ref, first_row, rows):
    for r in range(PACK_ROWS):
        lo = _bf16_bits(v[:, (2 * r) * LANES:(2 * r + 1) * LANES])
        hi = _bf16_bits(v[:, (2 * r + 1) * LANES:(2 * r + 2) * LANES])
        out_ref[pl.ds(first_row * PACK_ROWS + r, rows, stride=PACK_ROWS), :] = lax.shift_right_logical(lo, U32(16)) | hi


def _reorder_body(src_ref, w_ref, o_ref):
    o_ref[...] = w_ref[...].astype(o_ref.dtype)


def _reorder_w_in(w_in, layer):
    width = REORDER_WIDTH
    segments = [(_OFF_GATE, IN_WIDTH), (0, _OFF_Q), (_OFF_MQ, _OFF_GATE), (_OFF_Q, _OFF_K), (_OFF_K, _OFF_MQ)]
    assert all(a % width == 0 and b % width == 0 for a, b in segments)
    src = np.concatenate([np.arange(a // width, b // width) for a, b in segments]).astype(np.int32)
    return pl.pallas_call(
        _reorder_body,
        grid_spec=pltpu.PrefetchScalarGridSpec(
            num_scalar_prefetch=1,
            grid=(IN_WIDTH // width,),
            in_specs=[pl.BlockSpec((None, D_MODEL, width), lambda j, src: (layer, 0, src[j]))],
            out_specs=pl.BlockSpec((D_MODEL, width), lambda j, src: (0, j)),
        ),
        out_shape=jax.ShapeDtypeStruct((D_MODEL, IN_WIDTH), BF16),
        compiler_params=_cparams(("arbitrary",)),
        name="reorder_w_in",
    )(jnp.asarray(src), w_in)


def _inproj_body(x_ref, g_ref, w_ref, o_ref, h_ref, *, n_gate_tiles):
    j = pl.program_id(1)

    @pl.when(j == 0)
    def _norm():
        x = x_ref[...]
        inv = lax.rsqrt(jnp.mean(x * x, axis=-1, keepdims=True) + RMS_EPS)
        h_ref[...] = (x * inv * g_ref[...]).astype(BF16)

    @pl.when(j < n_gate_tiles)
    def _gate():
        width = w_ref.shape[1] // GATE_SLICES
        for c in range(GATE_SLICES):
            cols = slice(c * width, (c + 1) * width)
            acc = jnp.dot(h_ref[...], w_ref[:, cols], preferred_element_type=F32)
            o_ref[:, cols] = (1.0 / (1.0 + jnp.exp(-acc))).astype(o_ref.dtype)

    @pl.when(j >= n_gate_tiles)
    def _plain():
        o_ref[...] = jnp.dot(h_ref[...], w_ref[...], preferred_element_type=F32).astype(o_ref.dtype)


def _inproj(x2d, g, w):
    t = x2d.shape[0]
    return pl.pallas_call(
        functools.partial(_inproj_body, n_gate_tiles=GATE_WIDTH // TN_IN),
        grid=(t // TM_IN, IN_WIDTH // TN_IN),
        in_specs=[
            pl.BlockSpec((TM_IN, D_MODEL), lambda i, j: (i, 0)),
            pl.BlockSpec((1, D_MODEL), lambda i, j: (0, 0)),
            pl.BlockSpec((D_MODEL, TN_IN), lambda i, j: (0, j)),
        ],
        out_specs=pl.BlockSpec((TM_IN, TN_IN), lambda i, j: (i, j)),
        out_shape=jax.ShapeDtypeStruct((t, IN_WIDTH), BF16),
        scratch_shapes=[pltpu.VMEM((TM_IN, D_MODEL), BF16)],
        compiler_params=_cparams(("parallel", "arbitrary")),
        name="inproj",
    )(x2d, g, w)


def _fourier_body(f_ref, cc_ref, sc_ref, cs_ref, wfo_ref, o_ref, z_ref, *, seq):
    st = pl.program_id(1)

    @pl.when(st == 0)
    def _channel_dft():
        for g in range(FOURIER_GROUPS):
            cols = slice(g * FOURIER_GROUP_DIM, (g + 1) * FOURIER_GROUP_DIM)
            xg = f_ref[:, cols]
            z_ref[0:seq, cols] = jnp.dot(xg, cc_ref[...], preferred_element_type=F32).astype(BF16)
            z_ref[seq:2 * seq, cols] = jnp.dot(xg, sc_ref[...], preferred_element_type=F32).astype(BF16)

    y = jnp.dot(cs_ref[...], z_ref[...], preferred_element_type=F32)
    o_ref[...] = jnp.dot(y.astype(BF16), wfo_ref[...], preferred_element_type=F32).astype(o_ref.dtype)


def _dft_tables(seq):
    c = FOURIER_GROUP_DIM
    kc = (np.arange(c)[:, None] * np.arange(c)[None, :]) % c
    ac = 2.0 * np.pi * kc / c
    ks = (np.arange(seq)[:, None] * np.arange(seq)[None, :]) % seq
    a_s = 2.0 * np.pi * ks / seq
    cc = (np.cos(ac) / np.sqrt(c)).astype(np.float32)
    sc = (np.sin(ac) / np.sqrt(c)).astype(np.float32)
    cs = np.concatenate([np.cos(a_s), -np.sin(a_s)], axis=1) / np.sqrt(seq)
    return cc, sc, cs.astype(np.float32)


def _fourier(proj, w_fo, batch, seq):
    cc, sc, cs = _dft_tables(seq)
    cc = jnp.asarray(cc).astype(BF16)
    sc = jnp.asarray(sc).astype(BF16)
    cs = jnp.asarray(cs).astype(BF16)
    t = batch * seq
    return pl.pallas_call(
        functools.partial(_fourier_body, seq=seq),
        grid=(batch, seq // TS_F),
        in_specs=[
            pl.BlockSpec((seq, FOURIER_WIDTH), lambda b, s: (b, COL_F // FOURIER_WIDTH)),
            pl.BlockSpec((FOURIER_GROUP_DIM, FOURIER_GROUP_DIM), lambda b, s: (0, 0)),
            pl.BlockSpec((FOURIER_GROUP_DIM, FOURIER_GROUP_DIM), lambda b, s: (0, 0)),
            pl.BlockSpec((TS_F, 2 * seq), lambda b, s: (s, 0)),
            pl.BlockSpec((FOURIER_WIDTH, D_MODEL), lambda b, s: (0, 0)),
        ],
        out_specs=pl.BlockSpec((TS_F, D_MODEL), lambda b, s: (b * (seq // TS_F) + s, 0)),
        out_shape=jax.ShapeDtypeStruct((t, D_MODEL), BF16),
        scratch_shapes=[pltpu.VMEM((2 * seq, FOURIER_WIDTH), BF16)],
        compiler_params=_cparams(("parallel", "arbitrary")),
        name="fourier",
    )(proj, cc, sc, cs, w_fo)


def _swa_body(sink_ref, q_ref, k_ref, v_ref, bias_ref, o_ref):
    n = pl.program_id(1)
    nb = pl.num_programs(1)
    prev = pl.multiple_of(jnp.maximum(n - 1, 0) * ATT_BLOCK, ATT_BLOCK)
    own = pl.multiple_of(n * ATT_BLOCK, ATT_BLOCK)
    nxt = pl.multiple_of(jnp.minimum(n + 1, nb - 1) * ATT_BLOCK, ATT_BLOCK)

    def band(ref):
        return jnp.concatenate(
            [ref[pl.ds(prev, ATT_BLOCK), :], ref[pl.ds(own, ATT_BLOCK), :], ref[pl.ds(nxt, ATT_BLOCK), :]], axis=0)

    kb = band(k_ref).astype(F32)
    vb = band(v_ref).astype(F32)
    key = lax.broadcasted_iota(jnp.int32, (3 * ATT_BLOCK, 1), 0)
    valid = ((key >= ATT_BLOCK) | (n > 0)) & ((key < 2 * ATT_BLOCK) | (n < nb - 1))
    low = lax.broadcasted_iota(jnp.int32, (1, LANES), 1) < HEAD_DIM
    low_row = lax.broadcasted_iota(jnp.int32, (LANES, 1), 0) < HEAD_DIM

    def halves(x, c, kv_is_low):
        a = x[:, c * LANES:(c + 1) * LANES]
        r = pltpu.roll(a, HEAD_DIM, 1)
        in_low, in_high = (a, r) if kv_is_low else (r, a)
        return jnp.where(low, in_low, 0.0), jnp.where(low, 0.0, in_high)

    pairs = N_HEADS // N_KV_HEADS // 2
    for g in range(N_KV_HEADS):
        k_par = [h.astype(BF16) for h in halves(kb, g // 2, g % 2 == 0)]
        vt_par = [h.T.astype(BF16) for h in halves(vb, g // 2, g % 2 == 0)]
        base = g * pairs * LANES
        qp = jnp.concatenate([q_ref[:, base + p * LANES:base + (p + 1) * LANES] for p in range(pairs)], axis=0)
        qp = qp * jnp.asarray(HEAD_DIM ** -0.5, BF16)
        probs, inv_den = [], []
        for e in range(2):
            s = lax.dot_general(k_par[e], qp, (((1,), (1,)), ((), ())), preferred_element_type=F32)
            s = jnp.where(valid, s + bias_ref[g, e], NEG_INF)
            sink = jnp.concatenate(
                [jnp.full((1, ATT_BLOCK), sink_ref[(g * pairs + p) * 2 + e], F32) for p in range(pairs)], axis=1)
            m = jnp.maximum(jnp.max(s, axis=0, keepdims=True), sink)
            p_un = jnp.exp(s - m)
            den = jnp.sum(p_un, axis=0, keepdims=True) + jnp.exp(sink - m)
            probs.append(p_un.astype(BF16))
            inv_den.append(1.0 / den)
        o_t = (jnp.dot(vt_par[0], probs[0], preferred_element_type=F32)
               + jnp.dot(vt_par[1], probs[1], preferred_element_type=F32))
        o = (o_t * jnp.where(low_row, inv_den[0], inv_den[1])).T
        for p in range(pairs):
            cols = slice(base + p * LANES, base + (p + 1) * LANES)
            o_ref[:, cols] = o[p * ATT_BLOCK:(p + 1) * ATT_BLOCK].astype(o_ref.dtype)


def _t5_bucket(rel):
    nb = NUM_BUCKETS // 2
    max_exact = nb // 2
    ret = (rel > 0).astype(jnp.int32) * nb
    n = jnp.abs(rel)
    nf = jnp.maximum(n, 1).astype(jnp.float32)
    large = max_exact + (jnp.log(nf / max_exact) / np.float32(np.log(MAX_DISTANCE / max_exact))
                         * (nb - max_exact)).astype(jnp.int32)
    large = jnp.minimum(large, nb - 1)
    return ret + jnp.where(n < max_exact, n, large)


def _band_bias(rel_table):
    q_off = jnp.arange(ATT_BLOCK)[:, None]
    k_off = jnp.arange(3 * ATT_BLOCK)[None, :]
    rel = k_off - ATT_BLOCK - q_off
    hot = (_t5_bucket(rel)[..., None] == jnp.arange(NUM_BUCKETS)).astype(F32)
    bias = jnp.einsum("qkb,bh->hqk", hot, rel_table.astype(F32), precision=lax.Precision.HIGHEST)
    bias = jnp.where(jnp.abs(rel) <= WINDOW, bias, NEG_INF)
    pairs = N_HEADS // N_KV_HEADS // 2
    bias = bias.reshape(N_KV_HEADS, pairs, 2, ATT_BLOCK, 3 * ATT_BLOCK).transpose(0, 2, 4, 1, 3)
    return bias.reshape(N_KV_HEADS, 2, 3 * ATT_BLOCK, pairs * ATT_BLOCK)


def _swa(proj, sink, bias, batch, seq):
    nb = seq // ATT_BLOCK
    t = batch * seq
    return pl.pallas_call(
        _swa_body,
        grid_spec=pltpu.PrefetchScalarGridSpec(
            num_scalar_prefetch=0,
            grid=(batch, nb),
            in_specs=[
                pl.BlockSpec(memory_space=pltpu.SMEM),
                pl.BlockSpec((ATT_BLOCK, ATTN_WIDTH), lambda b, n: (b * nb + n, COL_Q // ATTN_WIDTH)),
                pl.BlockSpec((seq, KV_WIDTH), lambda b, n: (b, COL_K // KV_WIDTH)),
                pl.BlockSpec((seq, KV_WIDTH), lambda b, n: (b, COL_V // KV_WIDTH)),
                pl.BlockSpec(bias.shape, lambda b, n: (0, 0, 0, 0)),
            ],
            out_specs=pl.BlockSpec((ATT_BLOCK, ATTN_WIDTH), lambda b, n: (b * nb + n, 0)),
        ),
        out_shape=jax.ShapeDtypeStruct((t, ATTN_WIDTH), BF16),
        compiler_params=_cparams(("parallel", "arbitrary")),
        name="swa",
    )(sink, proj, proj, proj, bias)


def _memkv_body(m_ref, w_ref, o_ref):
    o_ref[...] = jnp.dot(m_ref[...].astype(BF16), w_ref[...], preferred_element_type=F32).astype(o_ref.dtype)


def _memkv(mem2d, w):
    rows = mem2d.shape[0]
    tm, tn = 512, 1024
    return pl.pallas_call(
        _memkv_body,
        grid=(rows // tm, (2 * MEM_WIDTH) // tn),
        in_specs=[
            pl.BlockSpec((tm, D_MODEL), lambda i, j: (i, 0)),
            pl.BlockSpec((D_MODEL, tn), lambda i, j: (0, j)),
        ],
        out_specs=pl.BlockSpec((tm, tn), lambda i, j: (i, j)),
        out_shape=jax.ShapeDtypeStruct((rows, 2 * MEM_WIDTH), BF16),
        compiler_params=_cparams(("parallel", "arbitrary")),
        name="memkv",
    )(mem2d, w)


def _memattn_body(q_ref, k_ref, v_ref, o_ref):
    outs = []
    for h in range(MEM_HEADS):
        cols = slice(h * MEM_HEAD_DIM, (h + 1) * MEM_HEAD_DIM)
        s = lax.dot_general(q_ref[:, cols], k_ref[:, cols], (((1,), (1,)), ((), ())), preferred_element_type=F32)
        s = s * np.float32(MEM_HEAD_DIM ** -0.5)
        m = jnp.max(s, axis=-1, keepdims=True)
        p = jnp.exp(s - m)
        den = jnp.sum(p, axis=-1, keepdims=True)
        o = jnp.dot(p.astype(BF16), v_ref[:, cols], preferred_element_type=F32)
        outs.append(o / den)
    o_ref[...] = jnp.concatenate(outs, axis=1).astype(o_ref.dtype)


def _memattn(proj, kv, batch, seq):
    t = batch * seq
    nq = seq // TQ_MEM
    return pl.pallas_call(
        _memattn_body,
        grid=(batch, nq),
        in_specs=[
            pl.BlockSpec((TQ_MEM, MEM_WIDTH), lambda b, i: (b * nq + i, COL_MQ // MEM_WIDTH)),
            pl.BlockSpec((MEM_LEN, MEM_WIDTH), lambda b, i: (b, 0)),
            pl.BlockSpec((MEM_LEN, MEM_WIDTH), lambda b, i: (b, 1)),
        ],
        out_specs=pl.BlockSpec((TQ_MEM, MEM_WIDTH), lambda b, i: (b * nq + i, 0)),
        out_shape=jax.ShapeDtypeStruct((t, MEM_WIDTH), BF16),
        compiler_params=_cparams(("parallel", "arbitrary")),
        name="memattn",
    )(proj, kv, kv)


def _branches_body(yf_ref, att_ref, mo_ref, g0_ref, g1_ref, g2_ref, wao_ref, wmo_ref, o_ref):
    y_a = jnp.dot(att_ref[...], wao_ref[...], preferred_element_type=F32)
    y_m = jnp.dot(mo_ref[...], wmo_ref[...], preferred_element_type=F32)
    mixed = (g0_ref[...].astype(F32) * yf_ref[...].astype(F32)
             + g1_ref[...].astype(F32) * y_a + g2_ref[...].astype(F32) * y_m)
    o_ref[...] = mixed.astype(o_ref.dtype)


def _branches(y_f, att, mo, proj, w_ao, w_mo):
    t = y_f.shape[0]
    tm = TM_MERGE
    row = lambda i: (i, 0)
    const = lambda i: (0, 0)
    resident = dict(pipeline_mode=pl.Buffered(1))
    return pl.pallas_call(
        _branches_body,
        grid=(t // tm,),
        in_specs=[
            pl.BlockSpec((tm, D_MODEL), row),
            pl.BlockSpec((tm, ATTN_WIDTH), row),
            pl.BlockSpec((tm, MEM_WIDTH), row),
            pl.BlockSpec((tm, D_MODEL), lambda i: (i, 0)),
            pl.BlockSpec((tm, D_MODEL), lambda i: (i, 1)),
            pl.BlockSpec((tm, D_MODEL), lambda i: (i, 2)),
            pl.BlockSpec((ATTN_WIDTH, D_MODEL), const, **resident),
            pl.BlockSpec((MEM_WIDTH, D_MODEL), const, **resident),
        ],
        out_specs=pl.BlockSpec((tm, D_MODEL), row),
        out_shape=jax.ShapeDtypeStruct((t, D_MODEL), BF16),
        compiler_params=_cparams(("parallel",)),
        name="branches",
    )(y_f, att, mo, proj, proj, proj, w_ao, w_mo)


def _merge_body(mixed_ref, x_ref, wo_ref, n2_ref, wr_ref, br_ref,
                x1_ref, h2s_ref, te_ref, tp_ref, tw_ref, cnt_ref, carry_ref):
    i = pl.program_id(0)
    tm = x_ref.shape[0]

    @pl.when(i == 0)
    def _init():
        carry_ref[...] = jnp.zeros_like(carry_ref)

    x1 = x_ref[...] + jnp.dot(mixed_ref[...], wo_ref[...], preferred_element_type=F32)
    x1_ref[...] = x1

    inv = lax.rsqrt(jnp.mean(x1 * x1, axis=-1, keepdims=True) + RMS_EPS)
    h2 = x1 * inv * n2_ref[...]
    _pack_rows(h2, h2s_ref, 0, tm)

    h_hi = h2.astype(BF16)
    h_lo = (h2 - h_hi.astype(F32)).astype(BF16)
    w_r = wr_ref[...]
    w_hi = w_r.astype(BF16)
    w_lo = (w_r - w_hi.astype(F32)).astype(BF16)
    part = jnp.dot(h_hi, jnp.concatenate([w_hi, w_lo], axis=1), preferred_element_type=F32)
    logits = (part[:, :N_EXPERTS] + part[:, N_EXPERTS:]
              + jnp.dot(h_lo, w_hi, preferred_element_type=F32) + br_ref[...])
    lane = lax.broadcasted_iota(jnp.int32, (tm, N_EXPERTS), 1)
    vals, idxs, hots = [], [], []
    rest = logits
    for _ in range(TOP_K):
        mk = jnp.max(rest, axis=-1, keepdims=True)
        ik = jnp.min(jnp.where(rest == mk, lane, N_EXPERTS), axis=-1, keepdims=True)
        hot = lane == ik
        vals.append(mk)
        idxs.append(ik)
        hots.append(hot)
        rest = jnp.where(hot, -jnp.inf, rest)
    exps = [jnp.exp(v - vals[0]) for v in vals]
    den = exps[0] + exps[1] + exps[2] + exps[3]

    sel = (hots[0] | hots[1] | hots[2] | hots[3]).astype(F32)
    r_i = lax.broadcasted_iota(jnp.int32, (tm, tm), 0)
    c_i = lax.broadcasted_iota(jnp.int32, (tm, tm), 1)
    tri = (c_i < r_i).astype(BF16)
    pos = jnp.dot(tri, sel.astype(BF16), preferred_element_type=F32) + carry_ref[...]
    poss = [jnp.sum(jnp.where(hot, pos, 0.0), axis=-1, keepdims=True) for hot in hots]
    carry = carry_ref[...] + jnp.sum(sel, axis=0, keepdims=True)
    carry_ref[...] = carry

    te_ref[...] = jnp.concatenate(idxs, axis=1)
    tp_ref[...] = jnp.concatenate(poss, axis=1).astype(jnp.int32)
    tw_ref[...] = jnp.concatenate([e / den for e in exps], axis=1)
    cnt_ref[...] = carry.astype(jnp.int32)


def _merge(mixed, x2d, w_o, n2, w_r, b_r):
    t = x2d.shape[0]
    tm = TM_MERGE
    row = lambda i: (i, 0)
    const = lambda i: (0, 0)
    resident = dict(pipeline_mode=pl.Buffered(1))
    return pl.pallas_call(
        _merge_body,
        grid=(t // tm,),
        in_specs=[
            pl.BlockSpec((tm, D_MODEL), row),
            pl.BlockSpec((tm, D_MODEL), row),
            pl.BlockSpec((D_MODEL, D_MODEL), const, **resident),
            pl.BlockSpec((1, D_MODEL), const),
            pl.BlockSpec((D_MODEL, N_EXPERTS), const),
            pl.BlockSpec((1, N_EXPERTS), const),
        ],
        out_specs=[
            pl.BlockSpec((tm, D_MODEL), row),
            pl.BlockSpec((tm * PACK_ROWS, LANES), row),
            pl.BlockSpec((tm, TOP_K), row),
            pl.BlockSpec((tm, TOP_K), row),
            pl.BlockSpec((tm, TOP_K), row),
            pl.BlockSpec((1, N_EXPERTS), const),
        ],
        out_shape=[
            jax.ShapeDtypeStruct((t, D_MODEL), F32),
            jax.ShapeDtypeStruct((t * PACK_ROWS, LANES), jnp.uint32),
            jax.ShapeDtypeStruct((t, TOP_K), jnp.int32),
            jax.ShapeDtypeStruct((t, TOP_K), jnp.int32),
            jax.ShapeDtypeStruct((t, TOP_K), F32),
            jax.ShapeDtypeStruct((1, N_EXPERTS), jnp.int32),
        ],
        scratch_shapes=[pltpu.VMEM((1, N_EXPERTS), F32)],
        compiler_params=_cparams(("arbitrary",)),
        name="merge",
    )(mixed, x2d, w_o, n2, w_r, b_r)


def _row_slice(ref, row, n_rows=1):
    start = pl.multiple_of(row * PACK_ROWS, PACK_ROWS)
    return ref.at[pl.ds(start, n_rows * PACK_ROWS), :]


def _dispatch_body(zstart_ref, dest_ref, h_ref, xs_ref, zero_ref, zsem, sem):
    i = pl.program_id(0)
    tm = h_ref.shape[0] // PACK_ROWS

    @pl.when(i == 0)
    def _zero_tails():
        zero_ref[...] = jnp.zeros_like(zero_ref)

        def tail_copy(e):
            return pltpu.make_async_copy(zero_ref, _row_slice(xs_ref, zstart_ref[e], SUB_E), zsem)

        def start(e, c):
            @pl.when(zstart_ref[e] >= 0)
            def _():
                tail_copy(e).start()
            return c

        def wait(e, c):
            @pl.when(zstart_ref[e] >= 0)
            def _():
                tail_copy(e).wait()
            return c

        lax.fori_loop(0, zstart_ref.shape[0], start, 0)
        lax.fori_loop(0, zstart_ref.shape[0], wait, 0)

    def scatter(block, c):
        for u in range(DISP_UNROLL):
            t = block * DISP_UNROLL + u
            for k in range(TOP_K):
                d = dest_ref[t * TOP_K + k]
                pltpu.make_async_copy(_row_slice(h_ref, t), _row_slice(xs_ref, d), sem).start(priority=k % 2)
        return c

    lax.fori_loop(0, tm // DISP_UNROLL, scatter, 0)
    for _ in range(TOP_K):
        pltpu.make_async_copy(h_ref, _row_slice(xs_ref, 0, tm), sem).wait()


def _dispatch(zstart, dest_flat, h2s, n_slots):
    t = h2s.shape[0] // PACK_ROWS
    tm = TM_DISP
    return pl.pallas_call(
        _dispatch_body,
        grid_spec=pltpu.PrefetchScalarGridSpec(
            num_scalar_prefetch=1,
            grid=(t // tm,),
            in_specs=[
                pl.BlockSpec((tm * TOP_K,), lambda i, z: (i,), memory_space=pltpu.SMEM),
                pl.BlockSpec((tm * PACK_ROWS, LANES), lambda i, z: (i, 0)),
            ],
            out_specs=pl.BlockSpec(memory_space=pl.ANY),
            scratch_shapes=[
                pltpu.VMEM((SUB_E * PACK_ROWS, LANES), U32),
                pltpu.SemaphoreType.DMA(()),
                pltpu.SemaphoreType.DMA(()),
            ],
        ),
        out_shape=jax.ShapeDtypeStruct((n_slots * PACK_ROWS, LANES), U32),
        compiler_params=_cparams(("arbitrary",)),
        name="dispatch",
    )(zstart, dest_flat, h2s)


def _experts_body(ge_ref, gn_ref, nu_ref, xs_ref, wgu_ref, wd_ref, bgu_ref, bd_ref, o_ref,
                  xb_ref, wgu_bf_ref, wd_bf_ref, gu_even_ref, gu_odd_ref, act_ref):
    g = pl.program_id(0)
    f = pl.program_id(1)
    nsub = gn_ref[g]
    half = TF_E // 2
    gu_bufs = (gu_even_ref, gu_odd_ref)

    @pl.when((nsub == 0) & (f == 0))
    def _unused_group():
        o_ref[...] = jnp.zeros_like(o_ref)

    def load_rows():
        for r in range(PACK_ROWS):
            w = xs_ref[pl.ds(r, GROUP_E, stride=PACK_ROWS), :]
            lo = pltpu.bitcast(lax.shift_left(w, U32(16)), F32)
            hi = pltpu.bitcast(w & U32(0xFFFF0000), F32)
            xb_ref[:, (2 * r) * LANES:(2 * r + 1) * LANES] = lo.astype(BF16)
            xb_ref[:, (2 * r + 1) * LANES:(2 * r + 2) * LANES] = hi.astype(BF16)

    def convert_weights():
        wgu_bf_ref[...] = wgu_ref[...].astype(BF16)
        lo = _bf16_bits(wd_ref[0:half, :])
        hi = _bf16_bits(wd_ref[half:TF_E, :])
        chunk_rows = pl.ds(pl.multiple_of(f * TF_E, TF_E), TF_E)
        wd_bf_ref[chunk_rows, :] = pltpu.bitcast(lax.shift_right_logical(lo, U32(16)) | hi, BF16)

    def up_matmul(rows, parity):
        gu_bufs[parity][rows, :] = (jnp.dot(xb_ref[rows, :], wgu_bf_ref[...], preferred_element_type=F32)
                                    + bgu_ref[...])

    def activation(rows, parity):
        gu = gu_bufs[1 - parity][rows, :]
        gate = jnp.minimum(gu, SWIGLU_LIMIT)
        glu = gate * (1.0 / (1.0 + jnp.exp(-gate * SWIGLU_ALPHA)))
        up1 = jnp.clip(gu, -SWIGLU_LIMIT, SWIGLU_LIMIT) + 1.0
        prod = glu * pltpu.roll(up1, 2 * TF_E - 1, 1)
        even = (lax.broadcasted_iota(jnp.int32, (1, TF_E), 1) & 1) == 0
        act_ref[f - 1, rows, :] = jnp.where(even, prod[:, :TF_E], pltpu.roll(prod[:, TF_E:], 1, 1)).astype(BF16)

    def down_and_store(first_row, n_rows):
        rows = pl.ds(first_row, n_rows)
        act = jnp.concatenate([act_ref[c, rows, :] for c in range(NF_E)], axis=1)
        y = jnp.dot(act, wd_bf_ref[...], preferred_element_type=F32) + bd_ref[...]
        _pack_rows(y, o_ref, first_row, n_rows)

    def over_rows(parity, first, last):
        @pl.when(nsub > NSUB_E // 2)
        def _mostly_full():
            rows = pl.ds(0, MAIN_E)
            if first:
                load_rows()
            if not last:
                convert_weights()
                up_matmul(rows, parity)
            if not first:
                activation(rows, parity)
            if last:
                for s in range(MAIN_E // DOWN_E):
                    down_and_store(s * DOWN_E, DOWN_E)

            @pl.when(nsub == NSUB_E)
            def _overflow_rows():
                over = pl.ds(MAIN_E, OVER_E)
                if not last:
                    up_matmul(over, parity)
                if not first:
                    activation(over, parity)
                if last:
                    down_and_store(MAIN_E, OVER_E)

            if last:
                @pl.when(nsub < NSUB_E)
                def _no_overflow_rows():
                    o_ref[pl.ds(MAIN_E * PACK_ROWS, OVER_E * PACK_ROWS), :] = jnp.zeros(
                        (OVER_E * PACK_ROWS, LANES), o_ref.dtype)

        @pl.when(nsub <= NSUB_E // 2)
        def _few_rows():
            if first:
                load_rows()
            if not last:
                convert_weights()
            if last:
                o_ref[...] = jnp.zeros_like(o_ref)

            def body(s, c):
                first_row = pl.multiple_of(s * SUB_E, SUB_E)
                rows = pl.ds(first_row, SUB_E)
                if not last:
                    up_matmul(rows, parity)
                if not first:
                    activation(rows, parity)
                if last:
                    down_and_store(first_row, SUB_E)
                return c
            lax.fori_loop(0, nsub, body, 0)

    @pl.when(nsub > 0)
    def _group():
        @pl.when(f == 0)
        def _first_chunk():
            over_rows(0, True, False)

        for parity in range(2):
            @pl.when((f > 0) & (f < NF_E) & (f % 2 == parity))
            def _middle_chunk():
                over_rows(parity, False, False)

        @pl.when(f == NF_E)
        def _drain():
            over_rows(NF_E % 2, False, True)


def _experts(group_e, group_nsub, n_used, xs, w_gu, w_dn, b_gu, b_dn, layer, n_slots):
    ngrp = n_slots // GROUP_E
    nf = NF_E

    def fchunk(g, f, nu):
        return jnp.where(g < nu[0], jnp.clip(f, 0, nf - 1), nf - 1)

    return pl.pallas_call(
        _experts_body,
        grid_spec=pltpu.PrefetchScalarGridSpec(
            num_scalar_prefetch=3,
            grid=(ngrp, nf + 1),
            in_specs=[
                pl.BlockSpec((GROUP_E * PACK_ROWS, LANES), lambda g, f, ge, gn, nu: (jnp.minimum(g, nu[0] - 1), 0)),
                pl.BlockSpec((None, None, D_MODEL, 2 * TF_E),
                             lambda g, f, ge, gn, nu: (layer, ge[g], 0, fchunk(g, f, nu))),
                pl.BlockSpec((None, None, TF_E, D_MODEL),
                             lambda g, f, ge, gn, nu: (layer, ge[g], fchunk(g, f, nu), 0)),
                pl.BlockSpec((None, None, 1, 2 * TF_E),
                             lambda g, f, ge, gn, nu: (layer, ge[g], 0, fchunk(g, f, nu))),
                pl.BlockSpec((None, None, 1, D_MODEL), lambda g, f, ge, gn, nu: (layer, ge[g], 0, 0)),
            ],
            out_specs=pl.BlockSpec((GROUP_E * PACK_ROWS, LANES), lambda g, f, ge, gn, nu: (g, 0)),
            scratch_shapes=[
                pltpu.VMEM((GROUP_E, D_MODEL), BF16),
                pltpu.VMEM((D_MODEL, 2 * TF_E), BF16),
                pltpu.VMEM((D_FF, D_MODEL), BF16),
                pltpu.VMEM((GROUP_E, 2 * TF_E), F32),
                pltpu.VMEM((GROUP_E, 2 * TF_E), F32),
                pltpu.VMEM((NF_E, GROUP_E, TF_E), BF16),
            ],
        ),
        out_shape=jax.ShapeDtypeStruct((n_slots * PACK_ROWS, LANES), U32),
        compiler_params=_cparams(("arbitrary", "arbitrary"), EXPERTS_VMEM_LIMIT),
        name="experts",
    )(group_e, group_nsub, n_used, xs, w_gu, w_dn, b_gu, b_dn)


def _combine_body(dest_ref, dnext_ref, ys_ref, w_ref, x1_ref, g_ref, o_ref, buf_ref, sems):
    i = pl.program_id(0)
    n = pl.num_programs(0)
    tm = x1_ref.shape[0]
    buf_rows = tm * TOP_K
    slot = i % 2

    def issue_token(d_ref, b, t):
        for k in range(TOP_K):
            src = _row_slice(ys_ref, d_ref[t * TOP_K + k])
            dst = _row_slice(buf_ref, b * buf_rows + k * tm + t)
            pltpu.make_async_copy(src, dst, sems.at[b]).start(priority=k % 2)

    def wait_buffer(b):
        dst = _row_slice(buf_ref, b * buf_rows, buf_rows)
        pltpu.make_async_copy(_row_slice(ys_ref, 0, buf_rows), dst, sems.at[b]).wait()

    @pl.when(i == 0)
    def _first_tile():
        def body(t, c):
            issue_token(dest_ref, 0, t)
            return c
        lax.fori_loop(0, tm, body, 0)

    base = slot * buf_rows
    wait_buffer(slot)

    w = w_ref[...]
    wk = [jnp.broadcast_to(w[:, k:k + 1], (tm, LANES)) for k in range(TOP_K)]
    ssq = jnp.zeros((tm, 1), F32)
    ahead = tm // PACK_ROWS
    for r in range(PACK_ROWS):
        for t in range(r * ahead, (r + 1) * ahead):
            issue_token(dnext_ref, 1 - slot, t)
        acc_lo = acc_hi = None
        for k in range(TOP_K):
            word = buf_ref[pl.ds((base + k * tm) * PACK_ROWS + r, tm, stride=PACK_ROWS), :]
            lo = pltpu.bitcast(lax.shift_left(word, U32(16)), F32) * wk[k]
            hi = pltpu.bitcast(word & U32(0xFFFF0000), F32) * wk[k]
            acc_lo = lo if acc_lo is None else acc_lo + lo
            acc_hi = hi if acc_hi is None else acc_hi + hi
        for c, moe in ((2 * r, acc_lo), (2 * r + 1, acc_hi)):
            cols = slice(c * LANES, (c + 1) * LANES)
            v = x1_ref[:, cols] + moe
            o_ref[:, cols] = v
            ssq = ssq + jnp.sum(v * v, axis=-1, keepdims=True)
    inv = lax.rsqrt(ssq * np.float32(1.0 / D_MODEL) + RMS_EPS)
    o_ref[...] = o_ref[...] * inv * g_ref[...]

    @pl.when(i == n - 1)
    def _drain_extra_gather():
        wait_buffer(1 - slot)


def _combine(dest_flat, ys, top_w, x1, g):
    t = x1.shape[0]
    tm = TM_COMB
    nt = t // tm
    return pl.pallas_call(
        _combine_body,
        grid=(nt,),
        in_specs=[
            pl.BlockSpec((tm * TOP_K,), lambda i: (i,), memory_space=pltpu.SMEM),
            pl.BlockSpec((tm * TOP_K,), lambda i: (jnp.minimum(i + 1, nt - 1),), memory_space=pltpu.SMEM),
            pl.BlockSpec(memory_space=pl.ANY),
            pl.BlockSpec((tm, TOP_K), lambda i: (i, 0)),
            pl.BlockSpec((tm, D_MODEL), lambda i: (i, 0)),
            pl.BlockSpec((1, D_MODEL), lambda i: (0, 0)),
        ],
        out_specs=pl.BlockSpec((tm, D_MODEL), lambda i: (i, 0)),
        out_shape=jax.ShapeDtypeStruct((t, D_MODEL), F32),
        scratch_shapes=[pltpu.VMEM((2 * tm * TOP_K * PACK_ROWS, LANES), U32), pltpu.SemaphoreType.DMA((2,))],
        compiler_params=_cparams(("arbitrary",)),
        name="combine",
    )(dest_flat, dest_flat, ys, top_w, x1, g)


def _routing_tables(counts, top_e, top_pos, n_slots):
    i32 = jnp.int32
    ngrp = n_slots // GROUP_E
    experts = jnp.arange(N_EXPERTS, dtype=i32)
    padded = (counts + GROUP_E - 1) // GROUP_E * GROUP_E
    ends = jnp.cumsum(padded)
    starts = ends - padded
    total = ends[-1]
    dest = jnp.sum(jnp.where(top_e[..., None] == experts, starts, 0), axis=-1) + top_pos
    gstart = jnp.arange(ngrp, dtype=i32) * GROUP_E
    used = gstart < total
    of_group = gstart[:, None] >= ends[None, :]
    last_e = jnp.max(jnp.where(padded > 0, experts, 0))
    group_e = jnp.where(used, jnp.minimum(jnp.sum(of_group, axis=1), N_EXPERTS - 1), last_e)
    valid_end = jnp.sum(jnp.where(group_e[:, None] == experts, starts + counts, 0), axis=-1)
    rows = jnp.clip(valid_end - gstart, 0, GROUP_E)
    group_nsub = jnp.where(used, (rows + SUB_E - 1) // SUB_E, 0)
    n_used = (total // GROUP_E).reshape(1)
    sub = jnp.arange(NSUB_E, dtype=i32) * SUB_E
    tail = (starts + counts // SUB_E * SUB_E)[:, None] + sub
    tail = jnp.where(tail < ends[:, None], tail, -1)
    spare = (n_used[0] + experts)[:, None] * GROUP_E + sub
    spare = jnp.where(spare < n_slots, spare, -1)
    zstart = jnp.concatenate([tail.reshape(-1), spare.reshape(-1)])
    return (dest.reshape(-1).astype(i32), group_e.astype(i32), group_nsub.astype(i32), n_used.astype(i32),
            zstart.astype(i32))


def kernel(x, mem, rel_bias_table, norm1_g, w_in, attn_sink, w_mem_kv, w_fourier_out, w_attn_out, w_mem_out,
           w_o, norm2_g, w_router, b_router, w_gate_up, b_gate_up, w_down, b_down, final_g):
    batch, seq, _ = x.shape
    t = batch * seq
    depth = norm1_g.shape[0]
    assert depth == 1, "the combine kernel applies the final RMSNorm, so a single layer is supported"
    x2d = x.reshape(t, D_MODEL)
    mem2d = mem.reshape(batch * MEM_LEN, D_MODEL)
    bias = _band_bias(rel_bias_table)
    n_slots = pl.cdiv(t * TOP_K + N_EXPERTS * (GROUP_E - 1), GROUP_E) * GROUP_E

    for l in range(depth):
        proj = _inproj(x2d, norm1_g[l].reshape(1, D_MODEL), _reorder_w_in(w_in, l))
        y_f = _fourier(proj, w_fourier_out[l].astype(BF16), batch, seq)
        att = _swa(proj, attn_sink[l], bias, batch, seq)
        kv = _memkv(mem2d, w_mem_kv[l].astype(BF16))
        mo = _memattn(proj, kv, batch, seq)
        mixed = _branches(y_f, att, mo, proj, w_attn_out[l].astype(BF16), w_mem_out[l].astype(BF16))
        x1, h2s, top_e, top_pos, top_w, counts = _merge(
            mixed, x2d, w_o[l].astype(BF16),
            norm2_g[l].reshape(1, D_MODEL), w_router[l], b_router[l].reshape(1, N_EXPERTS))
        dest, group_e, group_nsub, n_used, zstart = _routing_tables(counts[0], top_e, top_pos, n_slots)
        xs = _dispatch(zstart, dest, h2s, n_slots)
        ys = _experts(group_e, group_nsub, n_used, xs, w_gate_up, w_down,
                      b_gate_up.reshape(depth, N_EXPERTS, 1, 2 * D_FF), b_down.reshape(depth, N_EXPERTS, 1, D_MODEL),
                      l, n_slots)
        x2d = _combine(dest, ys, top_w, x1, final_g.reshape(1, D_MODEL))
    return x2d.reshape(batch, seq, D_MODEL)
```

```python
import functools

import numpy as np
import jax
import jax.numpy as jnp
from jax import lax
from jax.experimental import pallas as pl
from jax.experimental.pallas import tpu as pltpu

D_MODEL = 2048
MEM_LEN = 256
FOURIER_GROUPS = 4
FOURIER_GROUP_DIM = 256
FOURIER_WIDTH = 1024
N_HEADS = 32
N_KV_HEADS = 4
HEAD_DIM = 64
ATTN_WIDTH = 2048
KV_WIDTH = 256
WINDOW = 128
ATT_BLOCK = 128
MEM_HEADS = 4
MEM_HEAD_DIM = 256
MEM_WIDTH = 1024
N_BRANCHES = 3
NUM_BUCKETS = 32
MAX_DISTANCE = 128
N_EXPERTS = 32
TOP_K = 4
D_FF = 2048
SWIGLU_LIMIT = 7.0
SWIGLU_ALPHA = 1.702
RMS_EPS = 1e-5
NEG_INF = -1e30

GATE_WIDTH = N_BRANCHES * D_MODEL
COL_GATE = 0
COL_F = GATE_WIDTH
COL_MQ = COL_F + FOURIER_WIDTH
COL_Q = COL_MQ + MEM_WIDTH
COL_K = COL_Q + ATTN_WIDTH
COL_V = COL_K + KV_WIDTH
IN_WIDTH = COL_V + KV_WIDTH

_OFF_Q = FOURIER_WIDTH
_OFF_K = _OFF_Q + ATTN_WIDTH
_OFF_V = _OFF_K + KV_WIDTH
_OFF_MQ = _OFF_V + KV_WIDTH
_OFF_GATE = _OFF_MQ + MEM_WIDTH

LANES = 128
PACK_ROWS = D_MODEL // (2 * LANES)
VMEM_CAPACITY_V7X = 64 * 1024 * 1024
VMEM_LIMIT = VMEM_CAPACITY_V7X // 8 * 7

REORDER_WIDTH = 512
TM_IN, TN_IN = 512, 1536
GATE_SLICES = 3
TS_F = 512
TQ_MEM = 512
TM_MERGE = 512
TM_DISP = 256
DISP_UNROLL = 8
MAIN_E, SUB_E, TF_E = 1024, 128, 256
OVER_E = SUB_E
GROUP_E = MAIN_E + OVER_E
NSUB_E = GROUP_E // SUB_E
NF_E = D_FF // TF_E
DOWN_E = 512
_EXPERTS_WINDOWS = 2 * (2 * GROUP_E * D_MODEL * 2 + (D_MODEL * 2 * TF_E + TF_E * D_MODEL) * 4)
_EXPERTS_SCRATCH = (GROUP_E * D_MODEL * 2 + D_MODEL * 2 * TF_E * 2 + D_FF * D_MODEL * 2
                    + 2 * GROUP_E * 2 * TF_E * 4 + NF_E * GROUP_E * TF_E * 2)
EXPERTS_VMEM_LIMIT = _EXPERTS_WINDOWS + _EXPERTS_SCRATCH + 4 * 1024 * 1024
TM_COMB = 256

BF16 = jnp.bfloat16
F32 = jnp.float32
U32 = jnp.uint32


def _cparams(sem, vmem_limit=VMEM_LIMIT):
    return pltpu.CompilerParams(dimension_semantics=sem, vmem_limit_bytes=vmem_limit)


def _bf16_bits(v):
    return pltpu.bitcast(v.astype(BF16).astype(F32), U32)


def _pack_rows(v, out_ref, first_row, rows):
    for r in range(PACK_ROWS):
        lo = _bf16_bits(v[:, (2 * r) * LANES:(2 * r + 1) * LANES])
        hi = _bf16_bits(v[:, (2 * r + 1) * LANES:(2 * r + 2) * LANES])
        out_ref[pl.ds(first_row * PACK_ROWS + r, rows, stride=PACK_ROWS), :] = lax.shift_right_logical(lo, U32(16)) | hi


def _unpack_rows(ref, first_row, rows):
    chunks = []
    for r in range(PACK_ROWS):
        w = ref[pl.ds(first_row * PACK_ROWS + r, rows, stride=PACK_ROWS), :]
        chunks.append(pltpu.bitcast(lax.shift_left(w, U32(16)), F32))
        chunks.append(pltpu.bitcast(w & U32(0xFFFF0000), F32))
    return chunks


def _reorder_body(src_ref, w_ref, o_ref):
    o_ref[...] = w_ref[...].astype(o_ref.dtype)


def _reorder_w_in(w_in, layer):
    width = REORDER_WIDTH
    segments = [(_OFF_GATE, IN_WIDTH), (0, _OFF_Q), (_OFF_MQ, _OFF_GATE), (_OFF_Q, _OFF_K), (_OFF_K, _OFF_MQ)]
    assert all(a % width == 0 and b % width == 0 for a, b in segments)
    src = np.concatenate([np.arange(a // width, b // width) for a, b in segments]).astype(np.int32)
    return pl.pallas_call(
        _reorder_body,
        grid_spec=pltpu.PrefetchScalarGridSpec(
            num_scalar_prefetch=1,
            grid=(IN_WIDTH // width,),
            in_specs=[pl.BlockSpec((None, D_MODEL, width), lambda j, src: (layer, 0, src[j]))],
            out_specs=pl.BlockSpec((D_MODEL, width), lambda j, src: (0, j)),
        ),
        out_shape=jax.ShapeDtypeStruct((D_MODEL, IN_WIDTH), BF16),
        compiler_params=_cparams(("arbitrary",)),
        name="reorder_w_in",
    )(jnp.asarray(src), w_in)


def _inproj_body(x_ref, g_ref, w_ref, o_ref, h_ref, *, n_gate_tiles):
    j = pl.program_id(1)

    @pl.when(j == 0)
    def _norm():
        x = x_ref[...]
        inv = lax.rsqrt(jnp.mean(x * x, axis=-1, keepdims=True) + RMS_EPS)
        h_ref[...] = (x * inv * g_ref[...]).astype(BF16)

    @pl.when(j < n_gate_tiles)
    def _gate():
        width = w_ref.shape[1] // GATE_SLICES
        for c in range(GATE_SLICES):
            cols = slice(c * width, (c + 1) * width)
            acc = jnp.dot(h_ref[...], w_ref[:, cols], preferred_element_type=F32)
            o_ref[:, cols] = (1.0 / (1.0 + jnp.exp(-acc))).astype(o_ref.dtype)

    @pl.when(j >= n_gate_tiles)
    def _plain():
        o_ref[...] = jnp.dot(h_ref[...], w_ref[...], preferred_element_type=F32).astype(o_ref.dtype)


def _inproj(x2d, g, w):
    t = x2d.shape[0]
    return pl.pallas_call(
        functools.partial(_inproj_body, n_gate_tiles=GATE_WIDTH // TN_IN),
        grid=(t // TM_IN, IN_WIDTH // TN_IN),
        in_specs=[
            pl.BlockSpec((TM_IN, D_MODEL), lambda i, j: (i, 0)),
            pl.BlockSpec((1, D_MODEL), lambda i, j: (0, 0)),
            pl.BlockSpec((D_MODEL, TN_IN), lambda i, j: (0, j)),
        ],
        out_specs=pl.BlockSpec((TM_IN, TN_IN), lambda i, j: (i, j)),
        out_shape=jax.ShapeDtypeStruct((t, IN_WIDTH), BF16),
        scratch_shapes=[pltpu.VMEM((TM_IN, D_MODEL), BF16)],
        compiler_params=_cparams(("parallel", "arbitrary")),
        name="inproj",
    )(x2d, g, w)


def _fourier_body(f_ref, cc_ref, sc_ref, cs_ref, wfo_ref, o_ref, z_ref, *, seq):
    st = pl.program_id(1)

    @pl.when(st == 0)
    def _channel_dft():
        for g in range(FOURIER_GROUPS):
            cols = slice(g * FOURIER_GROUP_DIM, (g + 1) * FOURIER_GROUP_DIM)
            xg = f_ref[:, cols]
            z_ref[0:seq, cols] = jnp.dot(xg, cc_ref[...], preferred_element_type=F32).astype(BF16)
            z_ref[seq:2 * seq, cols] = jnp.dot(xg, sc_ref[...], preferred_element_type=F32).astype(BF16)

    y = jnp.dot(cs_ref[...], z_ref[...], preferred_element_type=F32)
    o_ref[...] = jnp.dot(y.astype(BF16), wfo_ref[...], preferred_element_type=F32).astype(o_ref.dtype)


def _dft_tables(seq):
    c = FOURIER_GROUP_DIM
    kc = (np.arange(c)[:, None] * np.arange(c)[None, :]) % c
    ac = 2.0 * np.pi * kc / c
    ks = (np.arange(seq)[:, None] * np.arange(seq)[None, :]) % seq
    a_s = 2.0 * np.pi * ks / seq
    cc = (np.cos(ac) / np.sqrt(c)).astype(np.float32)
    sc = (np.sin(ac) / np.sqrt(c)).astype(np.float32)
    cs = np.concatenate([np.cos(a_s), -np.sin(a_s)], axis=1) / np.sqrt(seq)
    return cc, sc, cs.astype(np.float32)


def _fourier(proj, w_fo, batch, seq):
    cc, sc, cs = _dft_tables(seq)
    cc = jnp.asarray(cc).astype(BF16)
    sc = jnp.asarray(sc).astype(BF16)
    cs = jnp.asarray(cs).astype(BF16)
    t = batch * seq
    return pl.pallas_call(
        functools.partial(_fourier_body, seq=seq),
        grid=(batch, seq // TS_F),
        in_specs=[
            pl.BlockSpec((seq, FOURIER_WIDTH), lambda b, s: (b, COL_F // FOURIER_WIDTH)),
            pl.BlockSpec((FOURIER_GROUP_DIM, FOURIER_GROUP_DIM), lambda b, s: (0, 0)),
            pl.BlockSpec((FOURIER_GROUP_DIM, FOURIER_GROUP_DIM), lambda b, s: (0, 0)),
            pl.BlockSpec((TS_F, 2 * seq), lambda b, s: (s, 0)),
            pl.BlockSpec((FOURIER_WIDTH, D_MODEL), lambda b, s: (0, 0)),
        ],
        out_specs=pl.BlockSpec((TS_F, D_MODEL), lambda b, s: (b * (seq // TS_F) + s, 0)),
        out_shape=jax.ShapeDtypeStruct((t, D_MODEL), BF16),
        scratch_shapes=[pltpu.VMEM((2 * seq, FOURIER_WIDTH), BF16)],
        compiler_params=_cparams(("parallel", "arbitrary")),
        name="fourier",
    )(proj, cc, sc, cs, w_fo)


def _swa_body(sink_ref, q_ref, k_ref, v_ref, bias_ref, o_ref):
    n = pl.program_id(1)
    nb = pl.num_programs(1)
    prev = pl.multiple_of(jnp.maximum(n - 1, 0) * ATT_BLOCK, ATT_BLOCK)
    own = pl.multiple_of(n * ATT_BLOCK, ATT_BLOCK)
    nxt = pl.multiple_of(jnp.minimum(n + 1, nb - 1) * ATT_BLOCK, ATT_BLOCK)

    def band(ref):
        return jnp.concatenate(
            [ref[pl.ds(prev, ATT_BLOCK), :], ref[pl.ds(own, ATT_BLOCK), :], ref[pl.ds(nxt, ATT_BLOCK), :]], axis=0)

    kb = band(k_ref).astype(F32)
    vb = band(v_ref).astype(F32)
    key = lax.broadcasted_iota(jnp.int32, (3 * ATT_BLOCK, 1), 0)
    valid = ((key >= ATT_BLOCK) | (n > 0)) & ((key < 2 * ATT_BLOCK) | (n < nb - 1))
    low = lax.broadcasted_iota(jnp.int32, (1, LANES), 1) < HEAD_DIM
    low_row = lax.broadcasted_iota(jnp.int32, (LANES, 1), 0) < HEAD_DIM

    def halves(x, c, kv_is_low):
        a = x[:, c * LANES:(c + 1) * LANES]
        r = pltpu.roll(a, HEAD_DIM, 1)
        in_low, in_high = (a, r) if kv_is_low else (r, a)
        return jnp.where(low, in_low, 0.0), jnp.where(low, 0.0, in_high)

    pairs = N_HEADS // N_KV_HEADS // 2
    for g in range(N_KV_HEADS):
        k_par = [h.astype(BF16) for h in halves(kb, g // 2, g % 2 == 0)]
        vt_par = [h.T.astype(BF16) for h in halves(vb, g // 2, g % 2 == 0)]
        base = g * pairs * LANES
        qp = jnp.concatenate([q_ref[:, base + p * LANES:base + (p + 1) * LANES] for p in range(pairs)], axis=0)
        qp = qp * jnp.asarray(HEAD_DIM ** -0.5, BF16)
        probs, inv_den = [], []
        for e in range(2):
            s = lax.dot_general(k_par[e], qp, (((1,), (1,)), ((), ())), preferred_element_type=F32)
            s = jnp.where(valid, s + bias_ref[g, e], NEG_INF)
            sink = jnp.concatenate(
                [jnp.full((1, ATT_BLOCK), sink_ref[(g * pairs + p) * 2 + e], F32) for p in range(pairs)], axis=1)
            m = jnp.maximum(jnp.max(s, axis=0, keepdims=True), sink)
            p_un = jnp.exp(s - m)
            den = jnp.sum(p_un, axis=0, keepdims=True) + jnp.exp(sink - m)
            probs.append(p_un.astype(BF16))
            inv_den.append(1.0 / den)
        o_t = (jnp.dot(vt_par[0], probs[0], preferred_element_type=F32)
               + jnp.dot(vt_par[1], probs[1], preferred_element_type=F32))
        o = (o_t * jnp.where(low_row, inv_den[0], inv_den[1])).T
        for p in range(pairs):
            cols = slice(base + p * LANES, base + (p + 1) * LANES)
            o_ref[:, cols] = o[p * ATT_BLOCK:(p + 1) * ATT_BLOCK].astype(o_ref.dtype)


def _t5_bucket(rel):
    nb = NUM_BUCKETS // 2
    max_exact = nb // 2
    ret = (rel > 0).astype(jnp.int32) * nb
    n = jnp.abs(rel)
    nf = jnp.maximum(n, 1).astype(jnp.float32)
    large = max_exact + (jnp.log(nf / max_exact) / np.float32(np.log(MAX_DISTANCE / max_exact))
                         * (nb - max_exact)).astype(jnp.int32)
    large = jnp.minimum(large, nb - 1)
    return ret + jnp.where(n < max_exact, n, large)


def _band_bias(rel_table):
    q_off = jnp.arange(ATT_BLOCK)[:, None]
    k_off = jnp.arange(3 * ATT_BLOCK)[None, :]
    rel = k_off - ATT_BLOCK - q_off
    hot = (_t5_bucket(rel)[..., None] == jnp.arange(NUM_BUCKETS)).astype(F32)
    bias = jnp.einsum("qkb,bh->hqk", hot, rel_table.astype(F32), precision=lax.Precision.HIGHEST)
    bias = jnp.where(jnp.abs(rel) <= WINDOW, bias, NEG_INF)
    pairs = N_HEADS // N_KV_HEADS // 2
    bias = bias.reshape(N_KV_HEADS, pairs, 2, ATT_BLOCK, 3 * ATT_BLOCK).transpose(0, 2, 4, 1, 3)
    return bias.reshape(N_KV_HEADS, 2, 3 * ATT_BLOCK, pairs * ATT_BLOCK)


def _swa(proj, sink, bias, batch, seq):
    nb = seq // ATT_BLOCK
    t = batch * seq
    return pl.pallas_call(
        _swa_body,
        grid_spec=pltpu.PrefetchScalarGridSpec(
            num_scalar_prefetch=0,
            grid=(batch, nb),
            in_specs=[
                pl.BlockSpec(memory_space=pltpu.SMEM),
                pl.BlockSpec((ATT_BLOCK, ATTN_WIDTH), lambda b, n: (b * nb + n, COL_Q // ATTN_WIDTH)),
                pl.BlockSpec((seq, KV_WIDTH), lambda b, n: (b, COL_K // KV_WIDTH)),
                pl.BlockSpec((seq, KV_WIDTH), lambda b, n: (b, COL_V // KV_WIDTH)),
                pl.BlockSpec(bias.shape, lambda b, n: (0, 0, 0, 0)),
            ],
            out_specs=pl.BlockSpec((ATT_BLOCK, ATTN_WIDTH), lambda b, n: (b * nb + n, 0)),
        ),
        out_shape=jax.ShapeDtypeStruct((t, ATTN_WIDTH), BF16),
        compiler_params=_cparams(("parallel", "arbitrary")),
        name="swa",
    )(sink, proj, proj, proj, bias)


def _memkv_body(m_ref, w_ref, o_ref):
    o_ref[...] = jnp.dot(m_ref[...].astype(BF16), w_ref[...], preferred_element_type=F32).astype(o_ref.dtype)


def _memkv(mem2d, w):
    rows = mem2d.shape[0]
    tm, tn = 512, 1024
    return pl.pallas_call(
        _memkv_body,
        grid=(rows // tm, (2 * MEM_WIDTH) // tn),
        in_specs=[
            pl.BlockSpec((tm, D_MODEL), lambda i, j: (i, 0)),
            pl.BlockSpec((D_MODEL, tn), lambda i, j: (0, j)),
        ],
        out_specs=pl.BlockSpec((tm, tn), lambda i, j: (i, j)),
        out_shape=jax.ShapeDtypeStruct((rows, 2 * MEM_WIDTH), BF16),
        compiler_params=_cparams(("parallel", "arbitrary")),
        name="memkv",
    )(mem2d, w)


def _memattn_body(q_ref, k_ref, v_ref, o_ref):
    outs = []
    for h in range(MEM_HEADS):
        cols = slice(h * MEM_HEAD_DIM, (h + 1) * MEM_HEAD_DIM)
        s = lax.dot_general(q_ref[:, cols], k_ref[:, cols], (((1,), (1,)), ((), ())), preferred_element_type=F32)
        s = s * np.float32(MEM_HEAD_DIM ** -0.5)
        m = jnp.max(s, axis=-1, keepdims=True)
        p = jnp.exp(s - m)
        den = jnp.sum(p, axis=-1, keepdims=True)
        o = jnp.dot(p.astype(BF16), v_ref[:, cols], preferred_element_type=F32)
        outs.append(o / den)
    o_ref[...] = jnp.concatenate(outs, axis=1).astype(o_ref.dtype)


def _memattn(proj, kv, batch, seq):
    t = batch * seq
    nq = seq // TQ_MEM
    return pl.pallas_call(
        _memattn_body,
        grid=(batch, nq),
        in_specs=[
            pl.BlockSpec((TQ_MEM, MEM_WIDTH), lambda b, i: (b * nq + i, COL_MQ // MEM_WIDTH)),
            pl.BlockSpec((MEM_LEN, MEM_WIDTH), lambda b, i: (b, 0)),
            pl.BlockSpec((MEM_LEN, MEM_WIDTH), lambda b, i: (b, 1)),
        ],
        out_specs=pl.BlockSpec((TQ_MEM, MEM_WIDTH), lambda b, i: (b * nq + i, 0)),
        out_shape=jax.ShapeDtypeStruct((t, MEM_WIDTH), BF16),
        compiler_params=_cparams(("parallel", "arbitrary")),
        name="memattn",
    )(proj, kv, kv)


def _branches_body(yf_ref, att_ref, mo_ref, g0_ref, g1_ref, g2_ref, wao_ref, wmo_ref, o_ref):
    y_a = jnp.dot(att_ref[...], wao_ref[...], preferred_element_type=F32)
    y_m = jnp.dot(mo_ref[...], wmo_ref[...], preferred_element_type=F32)
    mixed = (g0_ref[...].astype(F32) * yf_ref[...].astype(F32)
             + g1_ref[...].astype(F32) * y_a + g2_ref[...].astype(F32) * y_m)
    o_ref[...] = mixed.astype(o_ref.dtype)


def _branches(y_f, att, mo, proj, w_ao, w_mo):
    t = y_f.shape[0]
    tm = TM_MERGE
    row = lambda i: (i, 0)
    const = lambda i: (0, 0)
    resident = dict(pipeline_mode=pl.Buffered(1))
    return pl.pallas_call(
        _branches_body,
        grid=(t // tm,),
        in_specs=[
            pl.BlockSpec((tm, D_MODEL), row),
            pl.BlockSpec((tm, ATTN_WIDTH), row),
            pl.BlockSpec((tm, MEM_WIDTH), row),
            pl.BlockSpec((tm, D_MODEL), lambda i: (i, 0)),
            pl.BlockSpec((tm, D_MODEL), lambda i: (i, 1)),
            pl.BlockSpec((tm, D_MODEL), lambda i: (i, 2)),
            pl.BlockSpec((ATTN_WIDTH, D_MODEL), const, **resident),
            pl.BlockSpec((MEM_WIDTH, D_MODEL), const, **resident),
        ],
        out_specs=pl.BlockSpec((tm, D_MODEL), row),
        out_shape=jax.ShapeDtypeStruct((t, D_MODEL), BF16),
        compiler_params=_cparams(("parallel",)),
        name="branches",
    )(y_f, att, mo, proj, proj, proj, w_ao, w_mo)


def _merge_body(mixed_ref, x_ref, wo_ref, n2_ref, wr_ref, br_ref,
                x1_ref, h2s_ref, te_ref, tp_ref, tw_ref, cnt_ref, carry_ref):
    i = pl.program_id(0)
    tm = x_ref.shape[0]

    @pl.when(i == 0)
    def _init():
        carry_ref[...] = jnp.zeros_like(carry_ref)

    x1 = x_ref[...] + jnp.dot(mixed_ref[...], wo_ref[...], preferred_element_type=F32)
    x1_ref[...] = x1

    inv = lax.rsqrt(jnp.mean(x1 * x1, axis=-1, keepdims=True) + RMS_EPS)
    h2 = x1 * inv * n2_ref[...]
    _pack_rows(h2, h2s_ref, 0, tm)

    h_hi = h2.astype(BF16)
    h_lo = (h2 - h_hi.astype(F32)).astype(BF16)
    w_r = wr_ref[...]
    w_hi = w_r.astype(BF16)
    w_lo = (w_r - w_hi.astype(F32)).astype(BF16)
    part = jnp.dot(h_hi, jnp.concatenate([w_hi, w_lo], axis=1), preferred_element_type=F32)
    logits = (part[:, :N_EXPERTS] + part[:, N_EXPERTS:]
              + jnp.dot(h_lo, w_hi, preferred_element_type=F32) + br_ref[...])
    lane = lax.broadcasted_iota(jnp.int32, (tm, N_EXPERTS), 1)
    vals, idxs, hots = [], [], []
    rest = logits
    for _ in range(TOP_K):
        mk = jnp.max(rest, axis=-1, keepdims=True)
        ik = jnp.min(jnp.where(rest == mk, lane, N_EXPERTS), axis=-1, keepdims=True)
        hot = lane == ik
        vals.append(mk)
        idxs.append(ik)
        hots.append(hot)
        rest = jnp.where(hot, -jnp.inf, rest)
    exps = [jnp.exp(v - vals[0]) for v in vals]
    den = exps[0] + exps[1] + exps[2] + exps[3]

    sel = (hots[0] | hots[1] | hots[2] | hots[3]).astype(F32)
    r_i = lax.broadcasted_iota(jnp.int32, (tm, tm), 0)
    c_i = lax.broadcasted_iota(jnp.int32, (tm, tm), 1)
    tri = (c_i < r_i).astype(BF16)
    pos = jnp.dot(tri, sel.astype(BF16), preferred_element_type=F32) + carry_ref[...]
    poss = [jnp.sum(jnp.where(hot, pos, 0.0), axis=-1, keepdims=True) for hot in hots]
    carry = carry_ref[...] + jnp.sum(sel, axis=0, keepdims=True)
    carry_ref[...] = carry

    te_ref[...] = jnp.concatenate(idxs, axis=1)
    tp_ref[...] = jnp.concatenate(poss, axis=1).astype(jnp.int32)
    tw_ref[...] = jnp.concatenate([e / den for e in exps], axis=1)
    cnt_ref[...] = carry.astype(jnp.int32)


def _merge(mixed, x2d, w_o, n2, w_r, b_r):
    t = x2d.shape[0]
    tm = TM_MERGE
    row = lambda i: (i, 0)
    const = lambda i: (0, 0)
    resident = dict(pipeline_mode=pl.Buffered(1))
    return pl.pallas_call(
        _merge_body,
        grid=(t // tm,),
        in_specs=[
            pl.BlockSpec((tm, D_MODEL), row),
            pl.BlockSpec((tm, D_MODEL), row),
            pl.BlockSpec((D_MODEL, D_MODEL), const, **resident),
            pl.BlockSpec((1, D_MODEL), const),
            pl.BlockSpec((D_MODEL, N_EXPERTS), const),
            pl.BlockSpec((1, N_EXPERTS), const),
        ],
        out_specs=[
            pl.BlockSpec((tm, D_MODEL), row),
            pl.BlockSpec((tm * PACK_ROWS, LANES), row),
            pl.BlockSpec((tm, TOP_K), row),
            pl.BlockSpec((tm, TOP_K), row),
            pl.BlockSpec((tm, TOP_K), row),
            pl.BlockSpec((1, N_EXPERTS), const),
        ],
        out_shape=[
            jax.ShapeDtypeStruct((t, D_MODEL), F32),
            jax.ShapeDtypeStruct((t * PACK_ROWS, LANES), jnp.uint32),
            jax.ShapeDtypeStruct((t, TOP_K), jnp.int32),
            jax.ShapeDtypeStruct((t, TOP_K), jnp.int32),
            jax.ShapeDtypeStruct((t, TOP_K), F32),
            jax.ShapeDtypeStruct((1, N_EXPERTS), jnp.int32),
        ],
        scratch_shapes=[pltpu.VMEM((1, N_EXPERTS), F32)],
        compiler_params=_cparams(("arbitrary",)),
        name="merge",
    )(mixed, x2d, w_o, n2, w_r, b_r)


def _row_slice(ref, row, n_rows=1):
    start = pl.multiple_of(row * PACK_ROWS, PACK_ROWS)
    return ref.at[pl.ds(start, n_rows * PACK_ROWS), :]


def _dispatch_body(zstart_ref, dest_ref, h_ref, xs_ref, zero_ref, zsem, sem):
    i = pl.program_id(0)
    tm = h_ref.shape[0] // PACK_ROWS

    @pl.when(i == 0)
    def _zero_tails():
        zero_ref[...] = jnp.zeros_like(zero_ref)

        def tail_copy(e):
            return pltpu.make_async_copy(zero_ref, _row_slice(xs_ref, zstart_ref[e], SUB_E), zsem)

        def start(e, c):
            @pl.when(zstart_ref[e] >= 0)
            def _():
                tail_copy(e).start()
            return c

        def wait(e, c):
            @pl.when(zstart_ref[e] >= 0)
            def _():
                tail_copy(e).wait()
            return c

        lax.fori_loop(0, zstart_ref.shape[0], start, 0)
        lax.fori_loop(0, zstart_ref.shape[0], wait, 0)

    def scatter(block, c):
        for u in range(DISP_UNROLL):
            t = block * DISP_UNROLL + u
            for k in range(TOP_K):
                d = dest_ref[t * TOP_K + k]
                pltpu.make_async_copy(_row_slice(h_ref, t), _row_slice(xs_ref, d), sem).start(priority=k % 2)
        return c

    lax.fori_loop(0, tm // DISP_UNROLL, scatter, 0)
    for _ in range(TOP_K):
        pltpu.make_async_copy(h_ref, _row_slice(xs_ref, 0, tm), sem).wait()


def _dispatch(zstart, dest_flat, h2s, n_slots):
    t = h2s.shape[0] // PACK_ROWS
    tm = TM_DISP
    return pl.pallas_call(
        _dispatch_body,
        grid_spec=pltpu.PrefetchScalarGridSpec(
            num_scalar_prefetch=1,
            grid=(t // tm,),
            in_specs=[
                pl.BlockSpec((tm * TOP_K,), lambda i, z: (i,), memory_space=pltpu.SMEM),
                pl.BlockSpec((tm * PACK_ROWS, LANES), lambda i, z: (i, 0)),
            ],
            out_specs=pl.BlockSpec(memory_space=pl.ANY),
            scratch_shapes=[
                pltpu.VMEM((SUB_E * PACK_ROWS, LANES), U32),
                pltpu.SemaphoreType.DMA(()),
                pltpu.SemaphoreType.DMA(()),
            ],
        ),
        out_shape=jax.ShapeDtypeStruct((n_slots * PACK_ROWS, LANES), U32),
        compiler_params=_cparams(("arbitrary",)),
        name="dispatch",
    )(zstart, dest_flat, h2s)


def _experts_body(ge_ref, gn_ref, nu_ref, xs_ref, wgu_ref, wd_ref, bgu_ref, bd_ref, o_ref,
                  xb_ref, wgu_bf_ref, wd_bf_ref, gu_even_ref, gu_odd_ref, act_ref):
    g = pl.program_id(0)
    f = pl.program_id(1)
    nsub = gn_ref[g]
    half = TF_E // 2
    gu_bufs = (gu_even_ref, gu_odd_ref)

    @pl.when((nsub == 0) & (f == 0))
    def _unused_group():
        o_ref[...] = jnp.zeros_like(o_ref)

    def load_rows():
        for r in range(PACK_ROWS):
            w = xs_ref[pl.ds(r, GROUP_E, stride=PACK_ROWS), :]
            lo = pltpu.bitcast(lax.shift_left(w, U32(16)), F32)
            hi = pltpu.bitcast(w & U32(0xFFFF0000), F32)
            xb_ref[:, (2 * r) * LANES:(2 * r + 1) * LANES] = lo.astype(BF16)
            xb_ref[:, (2 * r + 1) * LANES:(2 * r + 2) * LANES] = hi.astype(BF16)

    def convert_down():
        @pl.when((g == 0) | (ge_ref[g] != ge_ref[jnp.maximum(g - 1, 0)]))
        def _new_expert():
            lo = _bf16_bits(wd_ref[0:half, :])
            hi = _bf16_bits(wd_ref[half:TF_E, :])
            chunk_rows = pl.ds(pl.multiple_of(f * TF_E, TF_E), TF_E)
            wd_bf_ref[chunk_rows, :] = pltpu.bitcast(lax.shift_right_logical(lo, U32(16)) | hi, BF16)

    def convert_up():
        wgu_bf_ref[...] = wgu_ref[...].astype(BF16)

    def up_matmul(rows, parity):
        gu_bufs[parity][rows, :] = (jnp.dot(xb_ref[rows, :], wgu_bf_ref[...], preferred_element_type=F32)
                                    + bgu_ref[...])

    def activation(rows, parity):
        gu = gu_bufs[1 - parity][rows, :]
        gate = jnp.minimum(gu, SWIGLU_LIMIT)
        glu = gate * (1.0 / (1.0 + jnp.exp(-gate * SWIGLU_ALPHA)))
        up1 = jnp.clip(gu, -SWIGLU_LIMIT, SWIGLU_LIMIT) + 1.0
        prod = glu * pltpu.roll(up1, 2 * TF_E - 1, 1)
        even = (lax.broadcasted_iota(jnp.int32, (1, TF_E), 1) & 1) == 0
        act_ref[f - 1, rows, :] = jnp.where(even, prod[:, :TF_E], pltpu.roll(prod[:, TF_E:], 1, 1)).astype(BF16)

    def down_and_store(first_row, n_rows):
        rows = pl.ds(first_row, n_rows)
        act = jnp.concatenate([act_ref[c, rows, :] for c in range(NF_E)], axis=1)
        y = jnp.dot(act, wd_bf_ref[...], preferred_element_type=F32) + bd_ref[...]
        _pack_rows(y, o_ref, first_row, n_rows)

    def over_rows(parity, first, last):
        @pl.when(nsub > NSUB_E // 2)
        def _mostly_full():
            rows = pl.ds(0, MAIN_E)
            if not last:
                convert_down()
            if first:
                load_rows()
            if not last:
                convert_up()
                up_matmul(rows, parity)
            if not first:
                activation(rows, parity)
            if last:
                for s in range(MAIN_E // DOWN_E):
                    down_and_store(s * DOWN_E, DOWN_E)

            @pl.when(nsub == NSUB_E)
            def _overflow_rows():
                over = pl.ds(MAIN_E, OVER_E)
                if not last:
                    up_matmul(over, parity)
                if not first:
                    activation(over, parity)
                if last:
                    down_and_store(MAIN_E, OVER_E)

            if last:
                @pl.when(nsub < NSUB_E)
                def _no_overflow_rows():
                    o_ref[pl.ds(MAIN_E * PACK_ROWS, OVER_E * PACK_ROWS), :] = jnp.zeros(
                        (OVER_E * PACK_ROWS, LANES), o_ref.dtype)

        @pl.when(nsub <= NSUB_E // 2)
        def _few_rows():
            if first:
                load_rows()
            if not last:
                convert_down()
                convert_up()
            if last:
                o_ref[...] = jnp.zeros_like(o_ref)

            def body(s, c):
                first_row = pl.multiple_of(s * SUB_E, SUB_E)
                rows = pl.ds(first_row, SUB_E)
                if not last:
                    up_matmul(rows, parity)
                if not first:
                    activation(rows, parity)
                if last:
                    down_and_store(first_row, SUB_E)
                return c
            lax.fori_loop(0, nsub, body, 0)

    @pl.when(nsub > 0)
    def _group():
        @pl.when(f == 0)
        def _first_chunk():
            over_rows(0, True, False)

        for parity in range(2):
            @pl.when((f > 0) & (f < NF_E) & (f % 2 == parity))
            def _middle_chunk():
                over_rows(parity, False, False)

        @pl.when(f == NF_E)
        def _drain():
            over_rows(NF_E % 2, False, True)


def _experts(group_e, group_nsub, n_used, xs, w_gu, w_dn, b_gu, b_dn, layer, n_slots):
    ngrp = n_slots // GROUP_E
    nf = NF_E

    def fchunk(g, f, nu):
        return jnp.where(g < nu[0], jnp.clip(f, 0, nf - 1), nf - 1)

    return pl.pallas_call(
        _experts_body,
        grid_spec=pltpu.PrefetchScalarGridSpec(
            num_scalar_prefetch=3,
            grid=(ngrp, nf + 1),
            in_specs=[
                pl.BlockSpec((GROUP_E * PACK_ROWS, LANES), lambda g, f, ge, gn, nu: (jnp.minimum(g, nu[0] - 1), 0)),
                pl.BlockSpec((None, None, D_MODEL, 2 * TF_E),
                             lambda g, f, ge, gn, nu: (layer, ge[g], 0, fchunk(g, f, nu))),
                pl.BlockSpec((None, None, TF_E, D_MODEL),
                             lambda g, f, ge, gn, nu: (layer, ge[g], fchunk(g, f, nu), 0)),
                pl.BlockSpec((None, None, 1, 2 * TF_E),
                             lambda g, f, ge, gn, nu: (layer, ge[g], 0, fchunk(g, f, nu))),
                pl.BlockSpec((None, None, 1, D_MODEL), lambda g, f, ge, gn, nu: (layer, ge[g], 0, 0)),
            ],
            out_specs=pl.BlockSpec((GROUP_E * PACK_ROWS, LANES), lambda g, f, ge, gn, nu: (g, 0)),
            scratch_shapes=[
                pltpu.VMEM((GROUP_E, D_MODEL), BF16),
                pltpu.VMEM((D_MODEL, 2 * TF_E), BF16),
                pltpu.VMEM((D_FF, D_MODEL), BF16),
                pltpu.VMEM((GROUP_E, 2 * TF_E), F32),
                pltpu.VMEM((GROUP_E, 2 * TF_E), F32),
                pltpu.VMEM((NF_E, GROUP_E, TF_E), BF16),
            ],
        ),
        out_shape=jax.ShapeDtypeStruct((n_slots * PACK_ROWS, LANES), U32),
        compiler_params=_cparams(("arbitrary", "arbitrary"), EXPERTS_VMEM_LIMIT),
        name="experts",
    )(group_e, group_nsub, n_used, xs, w_gu, w_dn, b_gu, b_dn)


def _combine_body(dest_ref, dnext_ref, ys_ref, w_ref, x1_ref, g_ref, o_ref, buf_ref, sems):
    i = pl.program_id(0)
    n = pl.num_programs(0)
    tm = x1_ref.shape[0]
    buf_rows = tm * TOP_K
    slot = i % 2

    def issue_token(d_ref, b, t):
        for k in range(TOP_K):
            src = _row_slice(ys_ref, d_ref[t * TOP_K + k])
            dst = _row_slice(buf_ref, b * buf_rows + k * tm + t)
            pltpu.make_async_copy(src, dst, sems.at[b]).start(priority=k % 2)

    def wait_buffer(b):
        dst = _row_slice(buf_ref, b * buf_rows, buf_rows)
        pltpu.make_async_copy(_row_slice(ys_ref, 0, buf_rows), dst, sems.at[b]).wait()

    @pl.when(i == 0)
    def _first_tile():
        def body(t, c):
            issue_token(dest_ref, 0, t)
            return c
        lax.fori_loop(0, tm, body, 0)

    base = slot * buf_rows
    wait_buffer(slot)

    w = w_ref[...]
    wk = [jnp.broadcast_to(w[:, k:k + 1], (tm, LANES)) for k in range(TOP_K)]
    ssq = jnp.zeros((tm, 1), F32)
    ahead = tm // PACK_ROWS
    for r in range(PACK_ROWS):
        for t in range(r * ahead, (r + 1) * ahead):
            issue_token(dnext_ref, 1 - slot, t)
        acc_lo = acc_hi = None
        for k in range(TOP_K):
            word = buf_ref[pl.ds((base + k * tm) * PACK_ROWS + r, tm, stride=PACK_ROWS), :]
            lo = pltpu.bitcast(lax.shift_left(word, U32(16)), F32) * wk[k]
            hi = pltpu.bitcast(word & U32(0xFFFF0000), F32) * wk[k]
            acc_lo = lo if acc_lo is None else acc_lo + lo
            acc_hi = hi if acc_hi is None else acc_hi + hi
        for c, moe in ((2 * r, acc_lo), (2 * r + 1, acc_hi)):
            cols = slice(c * LANES, (c + 1) * LANES)
            v = x1_ref[:, cols] + moe
            o_ref[:, cols] = v
            ssq = ssq + jnp.sum(v * v, axis=-1, keepdims=True)
    inv = lax.rsqrt(ssq * np.float32(1.0 / D_MODEL) + RMS_EPS)
    o_ref[...] = o_ref[...] * inv * g_ref[...]

    @pl.when(i == n - 1)
    def _drain_extra_gather():
        wait_buffer(1 - slot)


def _combine(dest_flat, ys, top_w, x1, g):
    t = x1.shape[0]
    tm = TM_COMB
    nt = t // tm
    return pl.pallas_call(
        _combine_body,
        grid=(nt,),
        in_specs=[
            pl.BlockSpec((tm * TOP_K,), lambda i: (i,), memory_space=pltpu.SMEM),
            pl.BlockSpec((tm * TOP_K,), lambda i: (jnp.minimum(i + 1, nt - 1),), memory_space=pltpu.SMEM),
            pl.BlockSpec(memory_space=pl.ANY),
            pl.BlockSpec((tm, TOP_K), lambda i: (i, 0)),
            pl.BlockSpec((tm, D_MODEL), lambda i: (i, 0)),
            pl.BlockSpec((1, D_MODEL), lambda i: (0, 0)),
        ],
        out_specs=pl.BlockSpec((tm, D_MODEL), lambda i: (i, 0)),
        out_shape=jax.ShapeDtypeStruct((t, D_MODEL), F32),
        scratch_shapes=[pltpu.VMEM((2 * tm * TOP_K * PACK_ROWS, LANES), U32), pltpu.SemaphoreType.DMA((2,))],
        compiler_params=_cparams(("arbitrary",)),
        name="combine",
    )(dest_flat, dest_flat, ys, top_w, x1, g)


def _routing_tables(counts, top_e, top_pos, n_slots):
    i32 = jnp.int32
    ngrp = n_slots // GROUP_E
    experts = jnp.arange(N_EXPERTS, dtype=i32)
    padded = (counts + GROUP_E - 1) // GROUP_E * GROUP_E
    ends = jnp.cumsum(padded)
    starts = ends - padded
    total = ends[-1]
    dest = jnp.sum(jnp.where(top_e[..., None] == experts, starts, 0), axis=-1) + top_pos
    gstart = jnp.arange(ngrp, dtype=i32) * GROUP_E
    used = gstart < total
    of_group = gstart[:, None] >= ends[None, :]
    last_e = jnp.max(jnp.where(padded > 0, experts, 0))
    group_e = jnp.where(used, jnp.minimum(jnp.sum(of_group, axis=1), N_EXPERTS - 1), last_e)
    valid_end = jnp.sum(jnp.where(group_e[:, None] == experts, starts + counts, 0), axis=-1)
    rows = jnp.clip(valid_end - gstart, 0, GROUP_E)
    group_nsub = jnp.where(used, (rows + SUB_E - 1) // SUB_E, 0)
    n_used = (total // GROUP_E).reshape(1)
    sub = jnp.arange(NSUB_E, dtype=i32) * SUB_E
    tail = (starts + counts // SUB_E * SUB_E)[:, None] + sub
    tail = jnp.where(tail < ends[:, None], tail, -1)
    spare = (n_used[0] + experts)[:, None] * GROUP_E + sub
    spare = jnp.where(spare < n_slots, spare, -1)
    zstart = jnp.concatenate([tail.reshape(-1), spare.reshape(-1)])
    return (dest.reshape(-1).astype(i32), group_e.astype(i32), group_nsub.astype(i32), n_used.astype(i32),
            zstart.astype(i32))


def kernel(x, mem, rel_bias_table, norm1_g, w_in, attn_sink, w_mem_kv, w_fourier_out, w_attn_out, w_mem_out,
           w_o, norm2_g, w_router, b_router, w_gate_up, b_gate_up, w_down, b_down, final_g):
    batch, seq, _ = x.shape
    t = batch * seq
    depth = norm1_g.shape[0]
    assert depth == 1, "the combine kernel applies the final RMSNorm, so a single layer is supported"
    x2d = x.reshape(t, D_MODEL)
    mem2d = mem.reshape(batch * MEM_LEN, D_MODEL)
    bias = _band_bias(rel_bias_table)
    n_slots = pl.cdiv(t * TOP_K + N_EXPERTS * (GROUP_E - 1), GROUP_E) * GROUP_E

    for l in range(depth):
        proj = _inproj(x2d, norm1_g[l].reshape(1, D_MODEL), _reorder_w_in(w_in, l))
        y_f = _fourier(proj, w_fourier_out[l].astype(BF16), batch, seq)
        att = _swa(proj, attn_sink[l], bias, batch, seq)
        kv = _memkv(mem2d, w_mem_kv[l].astype(BF16))
        mo = _memattn(proj, kv, batch, seq)
        mixed = _branches(y_f, att, mo, proj, w_attn_out[l].astype(BF16), w_mem_out[l].astype(BF16))
        x1, h2s, top_e, top_pos, top_w, counts = _merge(
            mixed, x2d, w_o[l].astype(BF16),
            norm2_g[l].reshape(1, D_MODEL), w_router[l], b_router[l].reshape(1, N_EXPERTS))
        dest, group_e, group_nsub, n_used, zstart = _routing_tables(counts[0], top_e, top_pos, n_slots)
        xs = _dispatch(zstart, dest, h2s, n_slots)
        ys = _experts(group_e, group_nsub, n_used, xs, w_gate_up, w_down,
                      b_gate_up.reshape(depth, N_EXPERTS, 1, 2 * D_FF), b_down.reshape(depth, N_EXPERTS, 1, D_MODEL),
                      l, n_slots)
        x2d = _combine(dest, ys, top_w, x1, final_g.reshape(1, D_MODEL))
    return x2d.reshape(batch, seq, D_MODEL)
```
